```python
import math
import jax, jax.numpy as jnp
from jax import lax
import numpy as np

D_MODEL = 1024
BATCH = 2
SEQ = 8192
DEPTH = 2

GRID_W = 64
CTX_LEN = 256
EPS = 1e-6
ROPE_BASE = 10000.0
ATTN_BLOCK = 128

A_HEADS = 4
A_DK = 128
A_DV = 128
A_WIDTH = A_HEADS * A_DK
HGRN_CHUNK = 16
B_HEADS = 4
B_HALF = 64
B_DV = 2 * B_HALF
B_WIDTH = B_HEADS * B_DV
EVEN_IN = 5 * A_WIDTH + 3 * B_WIDTH
C_HEADS = 16
C_KV_HEADS = 4
C_HEAD_DIM = 64
C_WINDOW = 128
ODD_IN = (C_HEADS + 2 * C_KV_HEADS) * C_HEAD_DIM
N_EXPERTS = 16
N_GROUPS = 4
EXPERTS_PER_GROUP = N_EXPERTS // N_GROUPS
TOP_K = 2
D_EXPERT = 1024
MOE_BLOCK = 128

N_EVEN = (DEPTH + 1) // 2
N_ODD = DEPTH // 2

kernel_name = 'hybrid_hgrn2_diffattn_swa_groupmoe_dit'

F32 = jnp.float32


def rms_norm(x, g):
    xf = x.astype(F32)
    y = xf * lax.rsqrt(jnp.mean(xf * xf, axis=-1, keepdims=True) + EPS)
    return (y * g.astype(F32)).astype(x.dtype)


def axial_rope_tables(n_tokens, dim):
    n_rows = n_tokens // GRID_W
    row = jnp.repeat(jnp.arange(n_rows), GRID_W).astype(F32)
    col = jnp.tile(jnp.arange(GRID_W), n_rows).astype(F32)
    half = dim // 2
    inv = 1.0 / (ROPE_BASE ** (jnp.arange(0, half, 2, dtype=F32) / half))
    ar = row[:, None] * inv
    ac = col[:, None] * inv
    ang = jnp.concatenate([ar, ar, ac, ac], axis=-1)
    return jnp.cos(ang), jnp.sin(ang)


def _rotate_half(x):
    x1, x2 = jnp.split(x, 2, axis=-1)
    return jnp.concatenate([-x2, x1], axis=-1)


def apply_axial_rope(x, cos, sin):
    xr, xc = jnp.split(x, 2, axis=-1)
    xrot = jnp.concatenate([_rotate_half(xr), _rotate_half(xc)], axis=-1)
    return (x * cos[None, :, None, :] + xrot * sin[None, :, None, :]).astype(x.dtype)


def hgrn2_scan(q, k, v, logf, s0):
    b_, t_, h_, _ = q.shape
    dv = v.shape[-1]
    n = t_ // HGRN_CHUNK

    def chunks(a):
        return a.astype(F32).reshape(b_, n, HGRN_CHUNK, h_, a.shape[-1])

    qc, kc, vc, lf = chunks(q), chunks(k), chunks(v), chunks(logf)
    cum = jnp.cumsum(lf, axis=2)
    total = cum[:, :, -1]
    tri = jnp.tril(jnp.ones((HGRN_CHUNK, HGRN_CHUNK), dtype=bool))
    diff = cum[:, :, :, None] - cum[:, :, None, :]
    decay = jnp.exp(jnp.where(tri[:, :, None, None], diff, -jnp.inf))
    scores = jnp.einsum('bnthk,bntshk,bnshk->bnhts', qc, decay, kc)
    o_intra = jnp.einsum('bnhts,bnshv->bnthv', scores, vc)
    q_in = qc * jnp.exp(cum)
    k_out = kc * jnp.exp(total[:, :, None] - cum)

    def step(state, inp):
        q_n, k_n, v_n, tot_n = inp
        o_n = jnp.einsum('bthk,bhkv->bthv', q_n, state)
        state = state * jnp.exp(tot_n)[..., None] + jnp.einsum('bshk,bshv->bhkv', k_n, v_n)
        return state, o_n

    xs = (jnp.moveaxis(q_in, 1, 0), jnp.moveaxis(k_out, 1, 0), jnp.moveaxis(vc, 1, 0), jnp.moveaxis(total, 1, 0))
    s_final, o_inter = lax.scan(step, s0, xs)
    o = o_intra + jnp.moveaxis(o_inter, 0, 1)
    return o.reshape(b_, t_, h_, dv), s_final


def hgrn2_mixer(p, pc, lb, onorm_g):
    b_ = p.shape[0]

    def heads(a):
        return a.reshape(a.shape[0], a.shape[1], A_HEADS, -1)

    def split(a):
        q, ff, fb, i, g = jnp.split(a, 5, axis=-1)
        q = heads(jax.nn.silu(q.astype(F32)) * (A_DK ** -0.5))
        return q, heads(ff), heads(fb), heads(i), heads(g)

    ql, ffl, fbl, il, gl = split(p)
    qc, ffc, fbc, ic, gc = split(pc)
    lb = lb.astype(F32).reshape(2, A_HEADS, A_DK)

    def gates(f_raw, lb_d):
        f = lb_d + (1.0 - lb_d) * jax.nn.sigmoid(f_raw.astype(F32))
        return 1.0 - f, jnp.log(f)

    def direction(d, reverse, f_lat, f_ctx):
        fl = (lambda a: jnp.flip(a, axis=1)) if reverse else (lambda a: a)
        k_c, lf_c = gates(f_ctx, lb[d])
        k_l, lf_l = gates(f_lat, lb[d])
        s0 = jnp.zeros((b_, A_HEADS, A_DK, A_DV), F32)
        o_c, s_c = hgrn2_scan(fl(qc), fl(k_c), fl(ic), fl(lf_c), s0)
        o_l, _ = hgrn2_scan(fl(ql), fl(k_l), fl(il), fl(lf_l), s_c)
        return fl(o_c), fl(o_l)

    oc_f, ol_f = direction(0, False, ffl, ffc)
    oc_b, ol_b = direction(1, True, fbl, fbc)

    def finish(o, g):
        y = rms_norm(o, onorm_g) * jax.nn.silu(g.astype(F32))
        return y.reshape(y.shape[0], y.shape[1], A_WIDTH)

    return finish(ol_f + ol_b, gl), finish(oc_f + oc_b, gc)


def diff_attention_mixer(p, pc, lam_p, subln_g, lambda_init, rope, need_ctx):
    b_, t_, _ = p.shape

    def split_qkv(a):
        bb, tt = a.shape[0], a.shape[1]
        q, k, v = jnp.split(a, 3, axis=-1)
        return (q.reshape(bb, tt, B_HEADS, 2, B_HALF), k.reshape(bb, tt, B_HEADS, 2, B_HALF),
                v.reshape(bb, tt, B_HEADS, B_DV))

    q, k, v = split_qkv(p)
    qc, kc, vc = split_qkv(pc)
    cos, sin = rope
    q = apply_axial_rope(q.reshape(b_, t_, 2 * B_HEADS, B_HALF), cos, sin).reshape(b_, t_, B_HEADS, 2, B_HALF)
    k = apply_axial_rope(k.reshape(b_, t_, 2 * B_HEADS, B_HALF), cos, sin).reshape(b_, t_, B_HEADS, 2, B_HALF)
    lam_p = lam_p.astype(F32)
    lam = jnp.exp(jnp.sum(lam_p[0] * lam_p[1])) - jnp.exp(jnp.sum(lam_p[2] * lam_p[3])) + lambda_init
    scale = B_HALF ** -0.5

    def attend(qb, keys, vals):
        s = jnp.einsum('bqhmd,bkhmd->bhmqk', qb, keys).astype(F32) * scale
        pm = jax.nn.softmax(s, axis=-1)
        a = pm[:, :, 0] - lam * pm[:, :, 1]
        return jnp.einsum('bhqk,bkhv->bqhv', a, vals)

    keys = jnp.concatenate([kc, k], axis=1)
    vals = jnp.concatenate([vc, v], axis=1)
    nb = t_ // ATTN_BLOCK
    qb = jnp.moveaxis(q.reshape(b_, nb, ATTN_BLOCK, B_HEADS, 2, B_HALF), 1, 0)
    o = lax.map(lambda blk: attend(blk, keys, vals), qb)
    o = jnp.moveaxis(o, 0, 1).reshape(b_, t_, B_HEADS, B_DV)

    def finish(o):
        y = rms_norm(o, subln_g) * (1.0 - lambda_init)
        return y.reshape(y.shape[0], y.shape[1], B_WIDTH)

    y = finish(o)
    yc = finish(attend(qc, kc, vc)) if need_ctx else None
    return y, yc


def even_layer_mixer(h, hc, w_in, w_out, lb, onorm_g, lam_p, subln_g, lambda_init, rope, need_ctx):
    p = h @ w_in
    pc = hc @ w_in
    ya, yac = hgrn2_mixer(p[..., :5 * A_WIDTH], pc[..., :5 * A_WIDTH], lb, onorm_g)
    yb, ybc = diff_attention_mixer(p[..., 5 * A_WIDTH:], pc[..., 5 * A_WIDTH:], lam_p, subln_g, lambda_init, rope, need_ctx)
    y = jnp.concatenate([ya, yb], axis=-1) @ w_out
    yc = (jnp.concatenate([yac, ybc], axis=-1) @ w_out) if need_ctx else None
    return y, yc


def _sink_attend(qb, parts, sink_logit):
    scale = C_HEAD_DIM ** -0.5
    logits = []
    for kk, _, valid in parts:
        s = jnp.einsum('bqkgd,bskd->bkgqs', qb, kk).astype(F32) * scale
        if valid is not None:
            s = jnp.where(valid, s, -jnp.inf)
        logits.append(s)
    b_, nq = qb.shape[0], qb.shape[1]
    sink_col = jnp.broadcast_to(sink_logit[None, :, :, None, None], (b_, C_KV_HEADS, C_HEADS // C_KV_HEADS, nq, 1))
    pr = jax.nn.softmax(jnp.concatenate([sink_col] + logits, axis=-1), axis=-1)
    out = 0.0
    off = 1
    for (_, vv, _), s in zip(parts, logits):
        n = s.shape[-1]
        out = out + jnp.einsum('bkgqs,bskd->bqkgd', pr[..., off:off + n], vv)
        off += n
    return out


def odd_layer_mixer(h, hc, w_qkv, w_out, sink, rope, need_ctx):
    b_, t_, d_ = h.shape
    n_ctx = hc.shape[1]
    grp = C_HEADS // C_KV_HEADS
    q_cols = C_HEADS * C_HEAD_DIM
    p = h @ w_qkv
    q = p[..., :q_cols].reshape(b_, t_, C_HEADS, C_HEAD_DIM)
    k, v = jnp.split(p[..., q_cols:], 2, axis=-1)
    k = k.reshape(b_, t_, C_KV_HEADS, C_HEAD_DIM)
    v = v.reshape(b_, t_, C_KV_HEADS, C_HEAD_DIM)
    if need_ctx:
        pc = hc @ w_qkv
        qc = pc[..., :q_cols].reshape(b_, n_ctx, C_KV_HEADS, grp, C_HEAD_DIM)
        kvc = pc[..., q_cols:]
    else:
        kvc = hc @ w_qkv[:, q_cols:]
    kc, vc = jnp.split(kvc, 2, axis=-1)
    kc = kc.reshape(b_, n_ctx, C_KV_HEADS, C_HEAD_DIM)
    vc = vc.reshape(b_, n_ctx, C_KV_HEADS, C_HEAD_DIM)
    cos, sin = rope
    q = apply_axial_rope(q, cos, sin).reshape(b_, t_, C_KV_HEADS, grp, C_HEAD_DIM)
    k = apply_axial_rope(k, cos, sin)
    sink_logit = sink.astype(F32).reshape(C_KV_HEADS, grp)
    pad = ((0, 0), (C_WINDOW, C_WINDOW), (0, 0), (0, 0))
    kp = jnp.pad(k, pad)
    vp = jnp.pad(v, pad)
    span = ATTN_BLOCK + 2 * C_WINDOW
    nb = t_ // ATTN_BLOCK

    def block(n):
        start = n * ATTN_BLOCK
        qb = lax.dynamic_slice_in_dim(q, start, ATTN_BLOCK, axis=1)
        kb = lax.dynamic_slice_in_dim(kp, start, span, axis=1)
        vb = lax.dynamic_slice_in_dim(vp, start, span, axis=1)
        qpos = start + jnp.arange(ATTN_BLOCK)
        kpos = start - C_WINDOW + jnp.arange(span)
        valid = (jnp.abs(qpos[:, None] - kpos[None, :]) <= C_WINDOW) & (kpos[None, :] >= 0) & (kpos[None, :] < t_)
        return _sink_attend(qb, [(kc, vc, None), (kb, vb, valid)], sink_logit)

    o = lax.map(block, jnp.arange(nb))
    y = jnp.moveaxis(o, 0, 1).reshape(b_, t_, d_) @ w_out
    yc = None
    if need_ctx:
        yc = _sink_attend(qc, [(kc, vc, None)], sink_logit).reshape(b_, n_ctx, d_) @ w_out
    return y, yc


def routed_moe(h2d, router_w, router_b, w_gate, w_up, w_down):
    n_tok, d_ = h2d.shape
    aff = jax.nn.sigmoid(jnp.dot(h2d, router_w).astype(F32))
    sel = (aff + router_b.astype(F32)).reshape(n_tok, N_GROUPS, EXPERTS_PER_GROUP)
    group_score = jnp.sum(lax.top_k(sel, TOP_K)[0], axis=-1)
    grp = jnp.argmax(group_score, axis=-1)
    in_group = jnp.take_along_axis(sel, grp[:, None, None], axis=1)[:, 0]
    _, local = lax.top_k(in_group, TOP_K)
    expert = (grp[:, None] * EXPERTS_PER_GROUP + local).astype(jnp.int32)
    wts = jnp.take_along_axis(aff, expert, axis=1)
    wts = wts / jnp.sum(wts, axis=-1, keepdims=True)
    n_assign = n_tok * TOP_K
    n_blocks = -(-n_assign // MOE_BLOCK) + N_EXPERTS
    n_rows = n_blocks * MOE_BLOCK
    flat_e = expert.reshape(-1)
    flat_t = jnp.repeat(jnp.arange(n_tok, dtype=jnp.int32), TOP_K)
    flat_w = wts.reshape(-1)
    order = jnp.argsort(flat_e)
    e_sorted = flat_e[order]
    counts = jnp.bincount(flat_e, length=N_EXPERTS)
    padded = (counts + MOE_BLOCK - 1) // MOE_BLOCK * MOE_BLOCK
    pad_end = jnp.cumsum(padded)
    pad_start = pad_end - padded
    start = jnp.cumsum(counts) - counts
    dest = pad_start[e_sorted] + jnp.arange(n_assign) - start[e_sorted]
    row_tok = jnp.full((n_rows,), n_tok, jnp.int32).at[dest].set(flat_t[order])
    row_w = jnp.zeros((n_rows,), F32).at[dest].set(flat_w[order])
    block_start = jnp.arange(n_blocks) * MOE_BLOCK
    block_e = jnp.minimum(jnp.sum(pad_end[None, :] <= block_start[:, None], axis=1), N_EXPERTS - 1)
    x_pad = jnp.concatenate([h2d, jnp.zeros((1, d_), h2d.dtype)], axis=0)
    xb = x_pad[row_tok].reshape(n_blocks, MOE_BLOCK, d_)

    def expert_block(args):
        xblk, e = args
        hid = jax.nn.silu(xblk @ w_gate[e]) * (xblk @ w_up[e])
        return hid @ w_down[e]

    yb = lax.map(expert_block, (xb, block_e)).reshape(n_rows, d_)
    out = jnp.zeros((n_tok + 1, d_), F32).at[row_tok].add(yb * row_w[:, None])
    return out[:n_tok]


def setup_inputs(seed: int = 0) -> dict:
    key = jax.random.key(seed)
    ks = jax.random.split(key, 24)
    d = D_MODEL
    nrm = jax.random.normal
    return {
        'x': nrm(ks[0], (BATCH, SEQ, d), F32),
        'c': nrm(ks[1], (BATCH, d), F32),
        'ctx': nrm(ks[2], (BATCH, CTX_LEN, d), F32),
        'c_ctx': nrm(ks[3], (d,), F32),
        'ada_w': nrm(ks[4], (DEPTH, d, 6 * d), F32) * (0.5 * d ** -0.5),
        'ada_b': nrm(ks[5], (DEPTH, 6 * d), F32) * 0.02,
        'norm_mix_g': 1.0 + 0.02 * nrm(ks[6], (DEPTH, d), F32),
        'norm_ffn_g': 1.0 + 0.02 * nrm(ks[7], (DEPTH, d), F32),
        'even_w_in': nrm(ks[8], (N_EVEN, d, EVEN_IN), F32) * d ** -0.5,
        'even_w_out': nrm(ks[9], (N_EVEN, A_WIDTH + B_WIDTH, d), F32) * (A_WIDTH + B_WIDTH) ** -0.5,
        'hgrn_lb_logits': 0.1 * nrm(ks[10], (DEPTH + 1, 2, A_WIDTH), F32),
        'hgrn_onorm_g': 1.0 + 0.02 * nrm(ks[11], (N_EVEN, A_DV), F32),
        'diff_lambda': 0.1 * nrm(ks[12], (N_EVEN, 4, B_HALF), F32),
        'diff_subln_g': 1.0 + 0.02 * nrm(ks[13], (N_EVEN, B_DV), F32),
        'odd_w_qkv': nrm(ks[14], (N_ODD, d, ODD_IN), F32) * d ** -0.5,
        'odd_w_out': nrm(ks[15], (N_ODD, C_HEADS * C_HEAD_DIM, d), F32) * (C_HEADS * C_HEAD_DIM) ** -0.5,
        'swa_sink': 0.5 * nrm(ks[16], (N_ODD, C_HEADS), F32),
        'router_w': nrm(ks[17], (d, N_EXPERTS), F32) * d ** -0.5,
        'router_b': 0.01 * nrm(ks[18], (N_EXPERTS,), F32),
        'moe_w_gate': nrm(ks[19], (DEPTH, N_EXPERTS, d, D_EXPERT), F32) * d ** -0.5,
        'moe_w_up': nrm(ks[20], (DEPTH, N_EXPERTS, d, D_EXPERT), F32) * d ** -0.5,
        'moe_w_down': nrm(ks[21], (DEPTH, N_EXPERTS, D_EXPERT, d), F32) * D_EXPERT ** -0.5,
        'final_norm_g': 1.0 + 0.02 * nrm(ks[22], (d,), F32),
    }


def reference(x, c, ctx, c_ctx, ada_w, ada_b, norm_mix_g, norm_ffn_g, even_w_in, even_w_out,
              hgrn_lb_logits, hgrn_onorm_g, diff_lambda, diff_subln_g, odd_w_qkv, odd_w_out, swa_sink,
              router_w, router_b, moe_w_gate, moe_w_up, moe_w_down, final_norm_g):
    b_, t_, d_ = x.shape
    n_ctx = ctx.shape[1]
    rope_b = axial_rope_tables(t_, B_HALF)
    rope_c = axial_rope_tables(t_, C_HEAD_DIM)
    lower_bounds = jnp.cumsum(jax.nn.softmax(hgrn_lb_logits.astype(F32), axis=0), axis=0)
    xc = ctx
    for layer in range(DEPTH):
        last = layer == DEPTH - 1
        w_ada, b_ada = ada_w[layer], ada_b[layer]
        mod = (jax.nn.silu(c) @ w_ada + b_ada)[:, None, :]
        mod_c = (jax.nn.silu(c_ctx) @ w_ada + b_ada)[None, None, :]
        sh1, sc1, g1, sh2, sc2, g2 = jnp.split(mod, 6, axis=-1)
        csh1, csc1, cg1, csh2, csc2, cg2 = jnp.split(mod_c, 6, axis=-1)
        h = rms_norm(x, norm_mix_g[layer]) * (1.0 + sc1) + sh1
        hc = rms_norm(xc, norm_mix_g[layer]) * (1.0 + csc1) + csh1
        if layer % 2 == 0:
            j = layer // 2
            lambda_init = 0.8 - 0.6 * math.exp(-0.3 * layer)
            y, yc = even_layer_mixer(h, hc, even_w_in[j], even_w_out[j], lower_bounds[layer], hgrn_onorm_g[j],
                                     diff_lambda[j], diff_subln_g[j], lambda_init, rope_b, not last)
        else:
            j = layer // 2
            y, yc = odd_layer_mixer(h, hc, odd_w_qkv[j], odd_w_out[j], swa_sink[j], rope_c, not last)
        x = x + g1 * y
        h2 = rms_norm(x, norm_ffn_g[layer]) * (1.0 + sc2) + sh2
        if last:
            f = routed_moe(h2.reshape(-1, d_), router_w, router_b,
                           moe_w_gate[layer], moe_w_up[layer], moe_w_down[layer])
            x = x + g2 * f.reshape(b_, t_, d_)
        else:
            xc = xc + cg1 * yc
            hc2 = rms_norm(xc, norm_ffn_g[layer]) * (1.0 + csc2) + csh2
            tokens = jnp.concatenate([h2.reshape(-1, d_), hc2.reshape(-1, d_)], axis=0)
            f = routed_moe(tokens, router_w, router_b, moe_w_gate[layer], moe_w_up[layer], moe_w_down[layer])
            x = x + g2 * f[:b_ * t_].reshape(b_, t_, d_)
            xc = xc + cg2 * f[b_ * t_:].reshape(b_, n_ctx, d_)
    return rms_norm(x, final_norm_g)
```

```python
import functools
import math

import jax
import jax.numpy as jnp
from jax import lax
from jax.experimental import pallas as pl
from jax.experimental.pallas import tpu as pltpu

F32 = jnp.float32
BF16 = jnp.bfloat16
EPS = 1e-6
ROPE_BASE = 10000.0
GRID_W = 64

A_HEADS, A_DK = 4, 128
A_WIDTH = A_HEADS * A_DK
HGRN_CHUNK = 16
B_HEADS, B_HALF = 4, 64
B_DV = 2 * B_HALF
B_WIDTH = B_HEADS * B_DV
C_HEADS, C_KV_HEADS, C_HEAD_DIM, C_WINDOW = 16, 4, 64, 128
C_GROUP = C_HEADS // C_KV_HEADS
N_EXPERTS, N_GROUPS, TOP_K = 16, 4, 2
EXPERTS_PER_GROUP = N_EXPERTS // N_GROUPS

LANES = 128
ROW_TILE = 512
HGRN_TILE = 256
ATTN_TQ = 256
DIFF_TK = 512
MOE_TILE = 256
MOE_TOK = 256
VMEM_LIMIT = 56 * 1024 * 1024
HIGHEST = lax.Precision.HIGHEST
NEG_INF = float("-inf")


def _cparams(sem):
    return pltpu.CompilerParams(dimension_semantics=sem, vmem_limit_bytes=VMEM_LIMIT)


def _nt_dot(a, b):
    return lax.dot_general(a, b, (((1,), (1,)), ((), ())), preferred_element_type=F32)


def _silu(x):
    return x * jax.nn.sigmoid(x)


def _ada_kernel(c_ref, w_ref, b_ref, o_ref):
    s = _silu(c_ref[...])
    o_ref[...] = jnp.dot(s, w_ref[...], preferred_element_type=F32, precision=HIGHEST) + b_ref[...]


def ada_modulation(cvec, ada_w, ada_b):
    n_layers, d, n6 = ada_w.shape
    tn = 512
    return pl.pallas_call(
        _ada_kernel,
        grid=(n_layers, n6 // tn),
        in_specs=[
            pl.BlockSpec((8, d), lambda l, j: (0, 0)),
            pl.BlockSpec((None, d, tn), lambda l, j: (l, 0, j)),
            pl.BlockSpec((None, 1, tn), lambda l, j: (l, 0, j)),
        ],
        out_specs=pl.BlockSpec((None, 8, tn), lambda l, j: (l, 0, j)),
        out_shape=jax.ShapeDtypeStruct((n_layers, 8, n6), F32),
        compiler_params=_cparams(("parallel", "parallel")),
        name="ada_modulation",
    )(cvec, ada_w, ada_b.reshape(n_layers, 1, n6))


def _rope_slab(x, cos, sin_a, sin_b):
    return x * cos + pltpu.roll(x, LANES - 16, 1) * sin_a + pltpu.roll(x, 16, 1) * sin_b


def _nmm_kernel(x_ref, g_ref, sc_ref, sh_ref, w_ref, cos_ref, sa_ref, sb_ref, o_ref, h_scr,
                *, rope_lo, rope_hi):
    j = pl.program_id(1)

    @pl.when(j == 0)
    def _():
        x = x_ref[...]
        y = x * lax.rsqrt(jnp.mean(x * x, axis=-1, keepdims=True) + EPS) * g_ref[...]
        h_scr[...] = (y * (1.0 + sc_ref[...]) + sh_ref[...]).astype(BF16)

    acc = jnp.dot(h_scr[...], w_ref[...], preferred_element_type=F32)
    tn = acc.shape[1]
    if rope_hi <= rope_lo:
        o_ref[...] = acc.astype(o_ref.dtype)
    else:
        is_rope = jnp.logical_and(j >= rope_lo, j < rope_hi)

        @pl.when(is_rope)
        def _():
            cos, sa, sb = cos_ref[...], sa_ref[...], sb_ref[...]
            for c in range(tn // LANES):
                sl = slice(c * LANES, (c + 1) * LANES)
                o_ref[:, sl] = _rope_slab(acc[:, sl], cos, sa, sb).astype(o_ref.dtype)

        @pl.when(jnp.logical_not(is_rope))
        def _():
            o_ref[...] = acc.astype(o_ref.dtype)


def norm_mod_matmul(x_all, g, mods, sc_chunk, sh_chunk, w, rope, rope_cols, out_dtype, mod_idx, tn=512):
    na, d = x_all.shape
    n = w.shape[1]
    cos, sa, sb = rope
    rope_lo, rope_hi = rope_cols[0] // tn, rope_cols[1] // tn
    kern = functools.partial(_nmm_kernel, rope_lo=rope_lo, rope_hi=rope_hi)
    return pl.pallas_call(
        kern,
        grid=(na // ROW_TILE, n // tn),
        in_specs=[
            pl.BlockSpec((ROW_TILE, d), lambda i, j: (i, 0)),
            pl.BlockSpec((1, d), lambda i, j: (0, 0)),
            pl.BlockSpec((None, 1, d), lambda i, j: (mod_idx(i), 0, sc_chunk)),
            pl.BlockSpec((None, 1, d), lambda i, j: (mod_idx(i), 0, sh_chunk)),
            pl.BlockSpec((d, tn), lambda i, j: (0, j)),
            pl.BlockSpec((ROW_TILE, LANES), lambda i, j: (i, 0)),
            pl.BlockSpec((ROW_TILE, LANES), lambda i, j: (i, 0)),
            pl.BlockSpec((ROW_TILE, LANES), lambda i, j: (i, 0)),
        ],
        out_specs=pl.BlockSpec((ROW_TILE, tn), lambda i, j: (i, j)),
        out_shape=jax.ShapeDtypeStruct((na, n), out_dtype),
        scratch_shapes=[pltpu.VMEM((ROW_TILE, d), BF16)],
        compiler_params=_cparams(("parallel", "arbitrary")),
        name="norm_mod_matmul",
    )(x_all, g, mods, mods, w, cos, sa, sb)


def _hgrn_direction(q_raw, f_raw, v, lb_row, st, reverse):
    tt = q_raw.shape[0]
    n_chunks = tt // HGRN_CHUNK
    qs = _silu(q_raw) * (A_DK ** -0.5)
    f = lb_row + (1.0 - lb_row) * jax.nn.sigmoid(f_raw)
    kk = 1.0 - f
    lf = jnp.log(f)

    r = lax.broadcasted_iota(jnp.int32, (tt, tt), 0)
    c = lax.broadcasted_iota(jnp.int32, (tt, tt), 1)
    same = (r // HGRN_CHUNK) == (c // HGRN_CHUNK)
    tri = (c >= r) if reverse else (c <= r)
    m_cum = jnp.where(jnp.logical_and(same, tri), 1.0, 0.0).astype(BF16)
    m_tot = jnp.where(same, 1.0, 0.0).astype(BF16)
    hi = lf.astype(BF16)
    r1 = lf - hi.astype(F32)
    mid = r1.astype(BF16)
    lo = (r1 - mid.astype(F32)).astype(BF16)

    def split_dot(m):
        return (jnp.dot(m, hi, preferred_element_type=F32) + jnp.dot(m, mid, preferred_element_type=F32)
                + jnp.dot(m, lo, preferred_element_type=F32))

    cum = split_dot(m_cum)
    tot = split_dot(m_tot)
    q_in = qs * jnp.exp(cum)
    k_out = kk * jnp.exp(tot - cum)

    pos = lax.broadcasted_iota(jnp.int32, (tt, 1), 0) % HGRN_CHUNK
    o = jnp.zeros((tt, A_DK), F32)
    for dist in range(HGRN_CHUNK):
        if dist == 0:
            ks, cs, vs = kk, cum, v
            diff = jnp.zeros_like(cum)
        else:
            shift = (tt - dist) if reverse else dist
            ks = pltpu.roll(kk, shift, 0)
            cs = pltpu.roll(cum, shift, 0)
            vs = pltpu.roll(v, shift, 0)
            valid = (pos <= HGRN_CHUNK - 1 - dist) if reverse else (pos >= dist)
            diff = jnp.where(valid, cum - cs, NEG_INF)
        score = jnp.sum(qs * ks * jnp.exp(diff), axis=-1, keepdims=True)
        o = o + score * vs

    v_t = v.T.astype(BF16)
    chunk_id = lax.broadcasted_iota(jnp.int32, (tt, 1), 0) // HGRN_CHUNK
    k_out_b = k_out.astype(BF16)
    q_in_b = q_in.astype(BF16)
    outs = [None] * n_chunks
    order = range(n_chunks - 1, -1, -1) if reverse else range(n_chunks)
    for ci in order:
        lo_r = ci * HGRN_CHUNK
        outs[ci] = o[lo_r:lo_r + HGRN_CHUNK] + _nt_dot(q_in_b[lo_r:lo_r + HGRN_CHUNK], st.astype(BF16))
        k_sel = jnp.where(chunk_id == ci, k_out_b, jnp.zeros_like(k_out_b))
        upd = jnp.dot(v_t, k_sel, preferred_element_type=F32)
        st = st * jnp.exp(tot[lo_r:lo_r + 1]) + upd
    return jnp.concatenate(outs, axis=0), st


def _hgrn_kernel(qf_ref, ff_ref, vf_ref, qb_ref, fb_ref, vb_ref, lb_ref, of_ref, ob_ref, st_scr):
    @pl.when(pl.program_id(2) == 0)
    def _():
        st_scr[...] = jnp.zeros_like(st_scr)

    o_f, st_f = _hgrn_direction(qf_ref[...], ff_ref[...], vf_ref[...], lb_ref[0:1, :], st_scr[0], False)
    of_ref[...] = o_f
    st_scr[0] = st_f
    o_b, st_b = _hgrn_direction(qb_ref[...], fb_ref[...], vb_ref[...], lb_ref[1:2, :], st_scr[1], True)
    ob_ref[...] = o_b
    st_scr[1] = st_b


def hgrn_scan(p_a, lb, batch, t_len, c_len):
    na = p_a.shape[0]
    tt = HGRN_TILE
    n_c, n_t = c_len // tt, t_len // tt
    off_c = batch * n_t

    def rb_f(b, s):
        return jnp.where(s < n_c, off_c + b * n_c + s, b * n_t + (s - n_c))

    def rb_b(b, s):
        return jnp.where(s < n_c, off_c + b * n_c + (n_c - 1 - s), b * n_t + (n_t - 1 - (s - n_c)))

    def spec(rb, col0):
        return pl.BlockSpec((tt, A_DK), lambda b, h, s: (rb(b, s), col0 + h))

    out_sds = jax.ShapeDtypeStruct((na, A_WIDTH), F32)
    return pl.pallas_call(
        _hgrn_kernel,
        grid=(batch, A_HEADS, n_c + n_t),
        in_specs=[spec(rb_f, 0), spec(rb_f, A_HEADS), spec(rb_f, 3 * A_HEADS),
                  spec(rb_b, 0), spec(rb_b, 2 * A_HEADS), spec(rb_b, 3 * A_HEADS),
                  pl.BlockSpec((2, A_DK), lambda b, h, s: (0, h))],
        out_specs=[spec(rb_f, 0), spec(rb_b, 0)],
        out_shape=[out_sds, out_sds],
        scratch_shapes=[pltpu.VMEM((2, A_DK, A_DK), F32)],
        compiler_params=_cparams(("parallel", "parallel", "arbitrary")),
        name="hgrn_scan",
    )(p_a, p_a, p_a, p_a, p_a, p_a, lb)


def _hgrn_finish_kernel(of_ref, ob_ref, g_ref, gn_ref, o_ref):
    o = of_ref[...] + ob_ref[...]
    gate = _silu(g_ref[...])
    for h in range(A_HEADS):
        sl = slice(h * A_DK, (h + 1) * A_DK)
        oh = o[:, sl]
        y = oh * lax.rsqrt(jnp.mean(oh * oh, axis=-1, keepdims=True) + EPS) * gn_ref[...]
        o_ref[:, sl] = (y * gate[:, sl]).astype(o_ref.dtype)


def hgrn_finish(o_f, o_b, p_a, onorm_g):
    na = o_f.shape[0]
    blk = pl.BlockSpec((ROW_TILE, A_WIDTH), lambda i: (i, 0))
    return pl.pallas_call(
        _hgrn_finish_kernel,
        grid=(na // ROW_TILE,),
        in_specs=[blk, blk, pl.BlockSpec((ROW_TILE, A_WIDTH), lambda i: (i, 4)),
                  pl.BlockSpec((1, A_DK), lambda i: (0, 0))],
        out_specs=blk,
        out_shape=jax.ShapeDtypeStruct((na, A_WIDTH), BF16),
        compiler_params=_cparams(("parallel",)),
        name="hgrn_finish",
    )(o_f, o_b, p_a, onorm_g)


def _diff_attn_kernel(q_ref, kc_ref, vc_ref, k_ref, v_ref, lam_ref, gn_ref, o_ref,
                      m_scr, l_scr, acc_scr, *, n_q, n_k, lambda_init):
    i = pl.program_id(2)
    tq = q_ref.shape[0]
    lane = lax.broadcasted_iota(jnp.int32, (1, B_DV), 1)
    q = q_ref[...] * (B_HALF ** -0.5)
    zero = jnp.zeros_like(q)
    qm = (jnp.where(lane < B_HALF, q, zero), jnp.where(lane >= B_HALF, q, zero))

    m_scr[...] = jnp.full_like(m_scr, NEG_INF)
    l_scr[...] = jnp.zeros_like(l_scr)
    acc_scr[...] = jnp.zeros_like(acc_scr)

    def update(kb, vb):
        for m in range(2):
            s = _nt_dot(qm[m], kb)
            m_old = m_scr[m]
            m_new = jnp.maximum(m_old, jnp.max(s, axis=-1, keepdims=True))
            alpha = jnp.exp(m_old - m_new)
            p = jnp.exp(s - m_new)
            l_scr[m] = alpha * l_scr[m] + jnp.sum(p, axis=-1, keepdims=True)
            acc_scr[m] = alpha * acc_scr[m] + jnp.dot(p.astype(BF16), vb, preferred_element_type=F32)
            m_scr[m] = m_new

    update(kc_ref[...], vc_ref[...])

    def body(j, carry):
        start = pl.multiple_of(j * DIFF_TK, DIFF_TK)
        update(k_ref[pl.ds(start, DIFF_TK), :], v_ref[pl.ds(start, DIFF_TK), :])
        return carry

    lax.fori_loop(0, jnp.where(i < n_q, n_k, 0), body, 0)

    lp = lam_ref[...]
    lam = (jnp.exp(jnp.sum(lp[0:1] * lp[1:2], axis=-1, keepdims=True))
           - jnp.exp(jnp.sum(lp[2:3] * lp[3:4], axis=-1, keepdims=True)) + lambda_init)
    o = acc_scr[0] / l_scr[0] - lam * (acc_scr[1] / l_scr[1])
    y = o * lax.rsqrt(jnp.mean(o * o, axis=-1, keepdims=True) + EPS) * gn_ref[...]
    o_ref[...] = (y * (1.0 - lambda_init)).astype(o_ref.dtype)


def diff_attention(p_b, lam_p, subln_g, lambda_init, batch, t_len, c_len):
    na = p_b.shape[0]
    tq = ATTN_TQ
    assert c_len == tq
    n_q = t_len // tq
    off_c = batch * n_q

    def q_blk(b, i):
        return jnp.where(i < n_q, b * n_q + i, off_c + b)

    kern = functools.partial(_diff_attn_kernel, n_q=n_q, n_k=t_len // DIFF_TK, lambda_init=lambda_init)
    return pl.pallas_call(
        kern,
        grid=(batch, B_HEADS, n_q + 1),
        in_specs=[
            pl.BlockSpec((tq, B_DV), lambda b, h, i: (q_blk(b, i), h)),
            pl.BlockSpec((c_len, B_DV), lambda b, h, i: (off_c + b, B_HEADS + h)),
            pl.BlockSpec((c_len, B_DV), lambda b, h, i: (off_c + b, 2 * B_HEADS + h)),
            pl.BlockSpec((t_len, B_DV), lambda b, h, i: (b, B_HEADS + h)),
            pl.BlockSpec((t_len, B_DV), lambda b, h, i: (b, 2 * B_HEADS + h)),
            pl.BlockSpec((4, B_HALF), lambda b, h, i: (0, 0)),
            pl.BlockSpec((1, B_DV), lambda b, h, i: (0, 0)),
        ],
        out_specs=pl.BlockSpec((tq, B_DV), lambda b, h, i: (q_blk(b, i), h)),
        out_shape=jax.ShapeDtypeStruct((na, B_WIDTH), BF16),
        scratch_shapes=[pltpu.VMEM((2, tq, 1), F32), pltpu.VMEM((2, tq, 1), F32),
                        pltpu.VMEM((2, tq, B_DV), F32)],
        compiler_params=_cparams(("parallel", "parallel", "arbitrary")),
        name="diff_attention",
    )(p_b, p_b, p_b, p_b, p_b, lam_p, subln_g)


def _swa_kernel(q_ref, kc_ref, vc_ref, k_ref, v_ref, sink_ref, o_ref, *, t_len):
    i = pl.program_id(2)
    tq, gw = q_ref.shape
    win = tq + 2 * C_WINDOW
    a = i * tq
    start = pl.multiple_of(jnp.clip(a - C_WINDOW, 0, t_len - win), C_WINDOW)
    kw = k_ref[pl.ds(start, win), :]
    vw = v_ref[pl.ds(start, win), :]
    kc, vc = kc_ref[...], vc_ref[...]
    qpos = a + lax.broadcasted_iota(jnp.int32, (tq, 1), 0)
    kpos = start + lax.broadcasted_iota(jnp.int32, (1, win), 1)
    valid = jnp.abs(qpos - kpos) <= C_WINDOW
    lane = lax.broadcasted_iota(jnp.int32, (1, gw), 1)
    q = q_ref[...] * (C_HEAD_DIM ** -0.5)
    sink_row = sink_ref[...]
    out = jnp.zeros((tq, gw), F32)
    for j in range(C_GROUP):
        in_head = (lane // C_HEAD_DIM) == j
        qj = jnp.where(in_head, q, jnp.zeros_like(q))
        sink = jnp.sum(jnp.where(lane == j * C_HEAD_DIM, sink_row, 0.0), axis=-1, keepdims=True)
        s_c = _nt_dot(qj, kc)
        s_w = jnp.where(valid, _nt_dot(qj, kw), NEG_INF)
        m = jnp.maximum(jnp.maximum(jnp.max(s_c, axis=-1, keepdims=True),
                                    jnp.max(s_w, axis=-1, keepdims=True)), sink)
        p_c = jnp.exp(s_c - m)
        p_w = jnp.exp(s_w - m)
        den = (jnp.exp(sink - m) + jnp.sum(p_c, axis=-1, keepdims=True)
               + jnp.sum(p_w, axis=-1, keepdims=True))
        o = (jnp.dot(p_c.astype(BF16), vc, preferred_element_type=F32)
             + jnp.dot(p_w.astype(BF16), vw, preferred_element_type=F32)) / den
        out = out + jnp.where(in_head, o, 0.0)
    o_ref[...] = out.astype(o_ref.dtype)


def window_attention(p1, sink_exp, batch, t_len, c_len):
    tq = ATTN_TQ
    gw = C_GROUP * C_HEAD_DIM
    n_q = t_len // tq
    off_c = batch * (t_len // c_len)
    kern = functools.partial(_swa_kernel, t_len=t_len)
    return pl.pallas_call(
        kern,
        grid=(batch, C_KV_HEADS, n_q),
        in_specs=[
            pl.BlockSpec((tq, gw), lambda b, g, i: (b * n_q + i, g)),
            pl.BlockSpec((c_len, gw), lambda b, g, i: (off_c + b, C_KV_HEADS + g)),
            pl.BlockSpec((c_len, gw), lambda b, g, i: (off_c + b, 2 * C_KV_HEADS + g)),
            pl.BlockSpec((t_len, gw), lambda b, g, i: (b, C_KV_HEADS + g)),
            pl.BlockSpec((t_len, gw), lambda b, g, i: (b, 2 * C_KV_HEADS + g)),
            pl.BlockSpec((None, 1, gw), lambda b, g, i: (g, 0, 0)),
        ],
        out_specs=pl.BlockSpec((tq, gw), lambda b, g, i: (b * n_q + i, g)),
        out_shape=jax.ShapeDtypeStruct((batch * t_len, C_HEADS * C_HEAD_DIM), BF16),
        compiler_params=_cparams(("parallel", "parallel", "arbitrary")),
        name="window_attention",
    )(p1, p1, p1, p1, p1, sink_exp)


def _out_proj_kernel(ya_ref, yb_ref, wa_ref, wb_ref, x_ref, g1_ref, g_ref, sc_ref, sh_ref, rw_ref,
                     xo_ref, h_ref, lg_ref):
    y = (jnp.dot(ya_ref[...], wa_ref[...], preferred_element_type=F32)
         + jnp.dot(yb_ref[...], wb_ref[...], preferred_element_type=F32))
    xn = x_ref[...] + g1_ref[...] * y
    xo_ref[...] = xn
    hn = xn * lax.rsqrt(jnp.mean(xn * xn, axis=-1, keepdims=True) + EPS) * g_ref[...]
    h2 = hn * (1.0 + sc_ref[...]) + sh_ref[...]
    h_ref[...] = h2
    lg_ref[...] = jnp.dot(h2, rw_ref[...], preferred_element_type=F32, precision=HIGHEST)


def out_proj(ya, yb_arr, yb_col, w_out, x_all, mods, g2n, router_w_pad, rows, mod_idx):
    d = x_all.shape[1]
    half = d // 2
    nw = router_w_pad.shape[1]
    row = lambda i: (i, 0)
    return pl.pallas_call(
        _out_proj_kernel,
        grid=(rows // ROW_TILE,),
        in_specs=[
            pl.BlockSpec((ROW_TILE, half), row),
            pl.BlockSpec((ROW_TILE, half), lambda i: (i, yb_col)),
            pl.BlockSpec((half, d), lambda i: (0, 0)),
            pl.BlockSpec((half, d), lambda i: (1, 0)),
            pl.BlockSpec((ROW_TILE, d), row),
            pl.BlockSpec((None, 1, d), lambda i: (mod_idx(i), 0, 2)),
            pl.BlockSpec((1, d), lambda i: (0, 0)),
            pl.BlockSpec((None, 1, d), lambda i: (mod_idx(i), 0, 4)),
            pl.BlockSpec((None, 1, d), lambda i: (mod_idx(i), 0, 3)),
            pl.BlockSpec((d, nw), lambda i: (0, 0)),
        ],
        out_specs=[pl.BlockSpec((ROW_TILE, d), row), pl.BlockSpec((ROW_TILE, d), row),
                   pl.BlockSpec((ROW_TILE, nw), row)],
        out_shape=[jax.ShapeDtypeStruct((rows, d), F32), jax.ShapeDtypeStruct((rows, d), F32),
                   jax.ShapeDtypeStruct((rows, nw), F32)],
        compiler_params=_cparams(("parallel",)),
        name="out_proj",
    )(ya, yb_arr, w_out, w_out, x_all, mods, g2n, mods, mods, router_w_pad)


def _row_copy(src_hbm, src_row, dst_ref, dst_row, sem):
    return pltpu.make_async_copy(src_hbm.at[pl.ds(src_row, 1)], dst_ref.at[pl.ds(dst_row, 1)], sem)


def _dispatch_kernel(dest_ref, h_hbm, zero_hbm, xs_hbm, sem):
    del zero_hbm
    i = pl.program_id(0)
    n = dest_ref.shape[-1]

    def start(r, carry):
        _row_copy(h_hbm, i * MOE_TOK + r // TOP_K, xs_hbm, dest_ref[0, 0, r], sem).start()
        return carry

    lax.fori_loop(0, n, start, 0)

    def wait(r, carry):
        _row_copy(h_hbm, 0, xs_hbm, 0, sem).wait()
        return carry

    lax.fori_loop(0, n, wait, 0)


def moe_dispatch(h2, dest, n_rows):
    rows, d = h2.shape
    n_t = rows // MOE_TOK
    dest3 = dest.reshape(n_t, 1, MOE_TOK * TOP_K)
    zeros = jnp.zeros((n_rows, d), F32)
    return pl.pallas_call(
        _dispatch_kernel,
        grid=(n_t,),
        in_specs=[
            pl.BlockSpec((1, 1, MOE_TOK * TOP_K), lambda i: (i, 0, 0), memory_space=pltpu.SMEM),
            pl.BlockSpec(memory_space=pl.ANY),
            pl.BlockSpec(memory_space=pl.ANY),
        ],
        out_specs=pl.BlockSpec(memory_space=pl.ANY),
        out_shape=jax.ShapeDtypeStruct((n_rows, d), F32),
        scratch_shapes=[pltpu.SemaphoreType.DMA],
        input_output_aliases={2: 0},
        compiler_params=_cparams(("arbitrary",)),
        name="moe_dispatch",
    )(dest3, h2, zeros)


def _expert_kernel(be_ref, nv_ref, x_ref, wg_ref, wu_ref, wd_ref, o_ref):
    del be_ref
    i = pl.program_id(0)

    @pl.when(i < nv_ref[0])
    def _():
        x = x_ref[...].astype(BF16)
        hid = (_silu(jnp.dot(x, wg_ref[...], preferred_element_type=F32))
               * jnp.dot(x, wu_ref[...], preferred_element_type=F32))
        o_ref[...] = jnp.dot(hid.astype(BF16), wd_ref[...], preferred_element_type=F32)

    @pl.when(i >= nv_ref[0])
    def _():
        o_ref[...] = jnp.zeros_like(o_ref)


def moe_experts(x_sorted, block_e, n_valid, w_gate, w_up, w_down):
    n_rows, d = x_sorted.shape
    de = w_gate.shape[2]
    n_blocks = n_rows // MOE_TILE
    grid_spec = pltpu.PrefetchScalarGridSpec(
        num_scalar_prefetch=2,
        grid=(n_blocks,),
        in_specs=[
            pl.BlockSpec((MOE_TILE, d), lambda i, be, nv: (i, 0)),
            pl.BlockSpec((None, d, de), lambda i, be, nv: (be[i], 0, 0)),
            pl.BlockSpec((None, d, de), lambda i, be, nv: (be[i], 0, 0)),
            pl.BlockSpec((None, de, d), lambda i, be, nv: (be[i], 0, 0)),
        ],
        out_specs=pl.BlockSpec((MOE_TILE, d), lambda i, be, nv: (i, 0)),
    )
    return pl.pallas_call(
        _expert_kernel,
        grid_spec=grid_spec,
        out_shape=jax.ShapeDtypeStruct((n_rows, d), F32),
        compiler_params=_cparams(("arbitrary",)),
        name="moe_experts",
    )(block_e, n_valid, x_sorted, w_gate, w_up, w_down)


def _combine_kernel(dest_ref, ys_hbm, w_ref, x_ref, g2_ref, gf_ref, o_ref, buf, sem, *, final_norm):
    n = dest_ref.shape[-1]

    def start(r, carry):
        _row_copy(ys_hbm, dest_ref[0, 0, r], buf.at[r % TOP_K], r // TOP_K, sem).start()
        return carry

    lax.fori_loop(0, n, start, 0)

    def wait(r, carry):
        _row_copy(ys_hbm, 0, buf.at[0], 0, sem).wait()
        return carry

    lax.fori_loop(0, n, wait, 0)
    w = w_ref[...]
    f = w[:, 0:1] * buf[0] + w[:, 1:2] * buf[1]
    xn = x_ref[...] + g2_ref[...] * f
    if final_norm:
        xn = xn * lax.rsqrt(jnp.mean(xn * xn, axis=-1, keepdims=True) + EPS) * gf_ref[...]
    o_ref[...] = xn


def moe_combine(y_sorted, dest, wts, x_new, mods, final_g, final_norm, mod_idx_tok):
    rows, d = x_new.shape
    n_t = rows // MOE_TOK
    dest3 = dest.reshape(n_t, 1, MOE_TOK * TOP_K)
    kern = functools.partial(_combine_kernel, final_norm=final_norm)
    return pl.pallas_call(
        kern,
        grid=(n_t,),
        in_specs=[
            pl.BlockSpec((1, 1, MOE_TOK * TOP_K), lambda i: (i, 0, 0), memory_space=pltpu.SMEM),
            pl.BlockSpec(memory_space=pl.ANY),
            pl.BlockSpec((MOE_TOK, TOP_K), lambda i: (i, 0)),
            pl.BlockSpec((MOE_TOK, d), lambda i: (i, 0)),
            pl.BlockSpec((None, 1, d), lambda i: (mod_idx_tok(i), 0, 5)),
            pl.BlockSpec((1, d), lambda i: (0, 0)),
        ],
        out_specs=pl.BlockSpec((MOE_TOK, d), lambda i: (i, 0)),
        out_shape=jax.ShapeDtypeStruct((rows, d), F32),
        scratch_shapes=[pltpu.VMEM((TOP_K, MOE_TOK, d), F32), pltpu.SemaphoreType.DMA],
        compiler_params=_cparams(("arbitrary",)),
        name="moe_combine",
    )(dest3, y_sorted, wts, x_new, mods, final_g)


def _route(logits, router_b):
    n_tok = logits.shape[0]
    aff = jax.nn.sigmoid(logits[:, :N_EXPERTS])
    sel = (aff + router_b.astype(F32)).reshape(n_tok, N_GROUPS, EXPERTS_PER_GROUP)
    group_score = jnp.sum(lax.top_k(sel, TOP_K)[0], axis=-1)
    grp = jnp.argmax(group_score, axis=-1)
    in_group = jnp.take_along_axis(sel, grp[:, None, None], axis=1)[:, 0]
    _, local = lax.top_k(in_group, TOP_K)
    expert = (grp[:, None] * EXPERTS_PER_GROUP + local).astype(jnp.int32)
    wts = jnp.take_along_axis(aff, expert, axis=1)
    wts = wts / jnp.sum(wts, axis=-1, keepdims=True)

    flat_e = expert.reshape(-1)
    onehot = (flat_e[:, None] == jnp.arange(N_EXPERTS, dtype=jnp.int32)[None, :]).astype(jnp.int32)
    csum = jnp.cumsum(onehot, axis=0)
    counts = csum[-1]
    pos = jnp.sum((csum - onehot) * onehot, axis=1)
    padded = (counts + MOE_TILE - 1) // MOE_TILE * MOE_TILE
    pad_end = jnp.cumsum(padded)
    pad_start = pad_end - padded
    dest = (jnp.sum(pad_start[None, :] * onehot, axis=1) + pos).astype(jnp.int32)
    n_blocks = -(-(n_tok * TOP_K) // MOE_TILE) + N_EXPERTS
    block_start = jnp.arange(n_blocks, dtype=jnp.int32) * MOE_TILE
    block_e = jnp.minimum(jnp.sum(pad_end[None, :] <= block_start[:, None], axis=1), N_EXPERTS - 1)
    n_valid = (pad_end[-1] // MOE_TILE).reshape(1)
    return wts, dest.reshape(n_tok, TOP_K), block_e.astype(jnp.int32), n_valid.astype(jnp.int32), n_blocks


def moe_layer(h2, logits, router_b, w_gate, w_up, w_down, x_new, mods, final_g, final_norm, mod_idx_tok):
    wts, dest, block_e, n_valid, n_blocks = _route(logits, router_b)
    x_sorted = moe_dispatch(h2, dest, n_blocks * MOE_TILE)
    y_sorted = moe_experts(x_sorted, block_e, n_valid, w_gate, w_up, w_down)
    return moe_combine(y_sorted, dest, wts, x_new, mods, final_g, final_norm, mod_idx_tok)


def _rope_tables(batch, t_len, c_len):
    n_rows = t_len // GRID_W
    row = jnp.repeat(jnp.arange(n_rows), GRID_W).astype(F32)
    col = jnp.tile(jnp.arange(GRID_W), n_rows).astype(F32)
    half = B_HALF // 2
    inv = 1.0 / (ROPE_BASE ** (jnp.arange(0, half, 2, dtype=F32) / half))
    ar, ac = row[:, None] * inv, col[:, None] * inv
    ang = jnp.concatenate([ar, ar, ac, ac], axis=-1)
    cos, sin = jnp.cos(ang), jnp.sin(ang)
    first = (jnp.arange(B_HALF) % 32) < 16
    sin_a = jnp.where(first, -sin, 0.0)
    sin_b = jnp.where(first, 0.0, sin)

    def full(tab, ctx_val):
        lat = jnp.tile(jnp.tile(tab, (1, LANES // B_HALF)), (batch, 1))
        ctx = jnp.full((batch * c_len, LANES), ctx_val, F32)
        return jnp.concatenate([lat, ctx], axis=0)

    return full(cos, 1.0), full(sin_a, 0.0), full(sin_b, 0.0)


def kernel(x, c, ctx, c_ctx, ada_w, ada_b, norm_mix_g, norm_ffn_g, even_w_in, even_w_out, hgrn_lb_logits,
           hgrn_onorm_g, diff_lambda, diff_subln_g, odd_w_qkv, odd_w_out, swa_sink, router_w, router_b,
           moe_w_gate, moe_w_up, moe_w_down, final_norm_g):
    batch, t_len, d = x.shape
    c_len = ctx.shape[1]
    n_lat = batch * t_len
    assert t_len % ROW_TILE == 0 and (batch * c_len) % ROW_TILE == 0 and c_len == HGRN_TILE
    assert batch < 8 and t_len % DIFF_TK == 0 and t_len % GRID_W == 0

    x_all = jnp.concatenate([x.reshape(n_lat, d), ctx.reshape(batch * c_len, d)], axis=0)
    rope = _rope_tables(batch, t_len, c_len)

    def mod_idx_for(tile):
        per_batch = t_len // tile
        return lambda i: jnp.minimum(i // per_batch, batch)

    mod_idx = mod_idx_for(ROW_TILE)
    mod_idx_tok = mod_idx_for(MOE_TOK)

    cvec = jnp.zeros((8, d), F32).at[:batch].set(c).at[batch].set(c_ctx)
    mods_all = ada_modulation(cvec, ada_w, ada_b)
    lower_bounds = jnp.cumsum(jax.nn.softmax(hgrn_lb_logits.astype(F32), axis=0), axis=0)
    router_w_pad = jnp.zeros((d, LANES), F32).at[:, :N_EXPERTS].set(router_w)

    mods = mods_all[0].reshape(8, 1, 6 * d)
    g_mix = norm_mix_g[0].reshape(1, d)
    w_in = even_w_in[0].astype(BF16)
    n_a = 5 * A_WIDTH
    p_a = norm_mod_matmul(x_all, g_mix, mods, 1, 0, w_in[:, :n_a], rope, (0, 0), F32, mod_idx)
    p_b = norm_mod_matmul(x_all, g_mix, mods, 1, 0, w_in[:, n_a:], rope, (0, 2 * B_WIDTH), BF16, mod_idx)
    o_f, o_b = hgrn_scan(p_a, lower_bounds[0], batch, t_len, c_len)
    ya = hgrn_finish(o_f, o_b, p_a, hgrn_onorm_g[0].reshape(1, A_DK))
    lambda_init = 0.8 - 0.6 * math.exp(-0.3 * 0)
    yb = diff_attention(p_b, diff_lambda[0], diff_subln_g[0].reshape(1, B_DV), lambda_init,
                        batch, t_len, c_len)
    na = x_all.shape[0]
    x_new, h2, logits = out_proj(ya, yb, 0, even_w_out[0].astype(BF16), x_all, mods,
                                 norm_ffn_g[0].reshape(1, d), router_w_pad, na, mod_idx)
    x_all = moe_layer(h2, logits, router_b, moe_w_gate[0].astype(BF16), moe_w_up[0].astype(BF16),
                      moe_w_down[0].astype(BF16), x_new, mods, final_norm_g.reshape(1, d), False,
                      mod_idx_tok)

    mods = mods_all[1].reshape(8, 1, 6 * d)
    w_qkv = odd_w_qkv[0]
    q_cols = C_HEADS * C_HEAD_DIM
    kv_cols = C_KV_HEADS * C_HEAD_DIM

    def rep_heads(wkv):
        wkv = wkv.reshape(d, C_KV_HEADS, 1, C_HEAD_DIM)
        return jnp.broadcast_to(wkv, (d, C_KV_HEADS, C_GROUP, C_HEAD_DIM)).reshape(d, q_cols)

    w1 = jnp.concatenate([w_qkv[:, :q_cols], rep_heads(w_qkv[:, q_cols:q_cols + kv_cols]),
                          rep_heads(w_qkv[:, q_cols + kv_cols:])], axis=1).astype(BF16)
    p1 = norm_mod_matmul(x_all, norm_mix_g[1].reshape(1, d), mods, 1, 0, w1, rope, (0, 2 * q_cols),
                         BF16, mod_idx)
    sink_exp = jnp.repeat(swa_sink[0].astype(F32), C_HEAD_DIM).reshape(C_KV_HEADS, 1, C_GROUP * C_HEAD_DIM)
    o1 = window_attention(p1, sink_exp, batch, t_len, c_len)
    x_new, h2, logits = out_proj(o1, o1, 1, odd_w_out[0].astype(BF16), x_all, mods,
                                 norm_ffn_g[1].reshape(1, d), router_w_pad, n_lat, mod_idx)
    out = moe_layer(h2, logits, router_b, moe_w_gate[1].astype(BF16), moe_w_up[1].astype(BF16),
                    moe_w_down[1].astype(BF16), x_new, mods, final_norm_g.reshape(1, d), True, mod_idx_tok)
    return out.reshape(batch, t_len, d)
```

```python
import functools
import math

import jax
import jax.numpy as jnp
from jax import lax
from jax.experimental import pallas as pl
from jax.experimental.pallas import tpu as pltpu

F32 = jnp.float32
BF16 = jnp.bfloat16
EPS = 1e-6
ROPE_BASE = 10000.0
GRID_W = 64

A_HEADS, A_DK = 4, 128
A_WIDTH = A_HEADS * A_DK
HGRN_CHUNK = 16
B_HEADS, B_HALF = 4, 64
B_DV = 2 * B_HALF
B_WIDTH = B_HEADS * B_DV
C_HEADS, C_KV_HEADS, C_HEAD_DIM, C_WINDOW = 16, 4, 64, 128
C_GROUP = C_HEADS // C_KV_HEADS
N_EXPERTS, N_GROUPS, TOP_K = 16, 4, 2
EXPERTS_PER_GROUP = N_EXPERTS // N_GROUPS

LANES = 128
ROW_TILE = 512
HGRN_TILE = 256
ATTN_TQ = 256
DIFF_TK = 512
DIFF_UNROLL = 4
MOE_TILE = 256
MOE_TOK = 256
VMEM_LIMIT = 56 * 1024 * 1024
HIGHEST = lax.Precision.HIGHEST
NEG_INF = float("-inf")


def _cparams(sem):
    return pltpu.CompilerParams(dimension_semantics=sem, vmem_limit_bytes=VMEM_LIMIT)


def _nt_dot(a, b):
    return lax.dot_general(a, b, (((1,), (1,)), ((), ())), preferred_element_type=F32)


def _silu(x):
    return x * jax.nn.sigmoid(x)


def _ada_kernel(c_ref, w_ref, b_ref, o_ref):
    s = _silu(c_ref[...])
    o_ref[...] = jnp.dot(s, w_ref[...], preferred_element_type=F32, precision=HIGHEST) + b_ref[...]


def ada_modulation(cvec, ada_w, ada_b):
    n_layers, d, n6 = ada_w.shape
    tn = 512
    return pl.pallas_call(
        _ada_kernel,
        grid=(n_layers, n6 // tn),
        in_specs=[
            pl.BlockSpec((8, d), lambda l, j: (0, 0)),
            pl.BlockSpec((None, d, tn), lambda l, j: (l, 0, j)),
            pl.BlockSpec((None, 1, tn), lambda l, j: (l, 0, j)),
        ],
        out_specs=pl.BlockSpec((None, 8, tn), lambda l, j: (l, 0, j)),
        out_shape=jax.ShapeDtypeStruct((n_layers, 8, n6), F32),
        compiler_params=_cparams(("parallel", "parallel")),
        name="ada_modulation",
    )(cvec, ada_w, ada_b.reshape(n_layers, 1, n6))


def _rope_slab(x, cos, sin_a, sin_b):
    return x * cos + pltpu.roll(x, LANES - 16, 1) * sin_a + pltpu.roll(x, 16, 1) * sin_b


def _nmm_kernel(x_ref, g_ref, sc_ref, sh_ref, w_ref, cos_ref, sa_ref, sb_ref, o_ref, h_scr,
                *, rope_lo, rope_hi):
    j = pl.program_id(1)

    @pl.when(j == 0)
    def _():
        x = x_ref[...]
        y = x * lax.rsqrt(jnp.mean(x * x, axis=-1, keepdims=True) + EPS) * g_ref[...]
        h_scr[...] = (y * (1.0 + sc_ref[...]) + sh_ref[...]).astype(BF16)

    acc = jnp.dot(h_scr[...], w_ref[...], preferred_element_type=F32)
    tn = acc.shape[1]
    if rope_hi <= rope_lo:
        o_ref[...] = acc.astype(o_ref.dtype)
    else:
        is_rope = jnp.logical_and(j >= rope_lo, j < rope_hi)

        @pl.when(is_rope)
        def _():
            cos, sa, sb = cos_ref[...], sa_ref[...], sb_ref[...]
            for c in range(tn // LANES):
                sl = slice(c * LANES, (c + 1) * LANES)
                o_ref[:, sl] = _rope_slab(acc[:, sl], cos, sa, sb).astype(o_ref.dtype)

        @pl.when(jnp.logical_not(is_rope))
        def _():
            o_ref[...] = acc.astype(o_ref.dtype)


def norm_mod_matmul(x_all, g, mods, sc_chunk, sh_chunk, w, rope, rope_cols, out_dtype, mod_idx, tn=512):
    na, d = x_all.shape
    n = w.shape[1]
    cos, sa, sb = rope
    rope_lo, rope_hi = rope_cols[0] // tn, rope_cols[1] // tn
    kern = functools.partial(_nmm_kernel, rope_lo=rope_lo, rope_hi=rope_hi)
    return pl.pallas_call(
        kern,
        grid=(na // ROW_TILE, n // tn),
        in_specs=[
            pl.BlockSpec((ROW_TILE, d), lambda i, j: (i, 0)),
            pl.BlockSpec((1, d), lambda i, j: (0, 0)),
            pl.BlockSpec((None, 1, d), lambda i, j: (mod_idx(i), 0, sc_chunk)),
            pl.BlockSpec((None, 1, d), lambda i, j: (mod_idx(i), 0, sh_chunk)),
            pl.BlockSpec((d, tn), lambda i, j: (0, j)),
            pl.BlockSpec((ROW_TILE, LANES), lambda i, j: (i, 0)),
            pl.BlockSpec((ROW_TILE, LANES), lambda i, j: (i, 0)),
            pl.BlockSpec((ROW_TILE, LANES), lambda i, j: (i, 0)),
        ],
        out_specs=pl.BlockSpec((ROW_TILE, tn), lambda i, j: (i, j)),
        out_shape=jax.ShapeDtypeStruct((na, n), out_dtype),
        scratch_shapes=[pltpu.VMEM((ROW_TILE, d), BF16)],
        compiler_params=_cparams(("parallel", "arbitrary")),
        name="norm_mod_matmul",
    )(x_all, g, mods, mods, w, cos, sa, sb)


def _hgrn_direction(q_raw, f_raw, v, lb_row, st, reverse):
    tt = q_raw.shape[0]
    n_chunks = tt // HGRN_CHUNK
    qs = _silu(q_raw) * (A_DK ** -0.5)
    f = lb_row + (1.0 - lb_row) * jax.nn.sigmoid(f_raw)
    kk = 1.0 - f
    lf = jnp.log(f)

    r = lax.broadcasted_iota(jnp.int32, (tt, tt), 0)
    c = lax.broadcasted_iota(jnp.int32, (tt, tt), 1)
    same = (r // HGRN_CHUNK) == (c // HGRN_CHUNK)
    tri = (c >= r) if reverse else (c <= r)
    m_cum = jnp.where(jnp.logical_and(same, tri), 1.0, 0.0).astype(BF16)
    m_tot = jnp.where(same, 1.0, 0.0).astype(BF16)
    hi = lf.astype(BF16)
    r1 = lf - hi.astype(F32)
    mid = r1.astype(BF16)
    lo = (r1 - mid.astype(F32)).astype(BF16)

    def split_dot(m):
        return (jnp.dot(m, hi, preferred_element_type=F32) + jnp.dot(m, mid, preferred_element_type=F32)
                + jnp.dot(m, lo, preferred_element_type=F32))

    cum = split_dot(m_cum)
    tot = split_dot(m_tot)
    q_in = qs * jnp.exp(cum)
    k_out = kk * jnp.exp(tot - cum)

    pos = lax.broadcasted_iota(jnp.int32, (tt, 1), 0) % HGRN_CHUNK
    o = jnp.zeros((tt, A_DK), F32)
    for dist in range(HGRN_CHUNK):
        if dist == 0:
            ks, cs, vs = kk, cum, v
            diff = jnp.zeros_like(cum)
        else:
            shift = (tt - dist) if reverse else dist
            ks = pltpu.roll(kk, shift, 0)
            cs = pltpu.roll(cum, shift, 0)
            vs = pltpu.roll(v, shift, 0)
            valid = (pos <= HGRN_CHUNK - 1 - dist) if reverse else (pos >= dist)
            diff = jnp.where(valid, cum - cs, NEG_INF)
        score = jnp.sum(qs * ks * jnp.exp(diff), axis=-1, keepdims=True)
        o = o + score * vs

    v_t = v.T.astype(BF16)
    chunk_id = lax.broadcasted_iota(jnp.int32, (tt, 1), 0) // HGRN_CHUNK
    k_out_b = k_out.astype(BF16)
    q_in_b = q_in.astype(BF16)
    outs = [None] * n_chunks
    order = range(n_chunks - 1, -1, -1) if reverse else range(n_chunks)
    for ci in order:
        lo_r = ci * HGRN_CHUNK
        outs[ci] = o[lo_r:lo_r + HGRN_CHUNK] + _nt_dot(q_in_b[lo_r:lo_r + HGRN_CHUNK], st.astype(BF16))
        k_sel = jnp.where(chunk_id == ci, k_out_b, jnp.zeros_like(k_out_b))
        upd = jnp.dot(v_t, k_sel, preferred_element_type=F32)
        st = st * jnp.exp(tot[lo_r:lo_r + 1]) + upd
    return jnp.concatenate(outs, axis=0), st


def _hgrn_kernel(qf_ref, ff_ref, vf_ref, qb_ref, fb_ref, vb_ref, lb_ref, of_ref, ob_ref, st_scr):
    @pl.when(pl.program_id(2) == 0)
    def _():
        st_scr[...] = jnp.zeros_like(st_scr)

    o_f, st_f = _hgrn_direction(qf_ref[...], ff_ref[...], vf_ref[...], lb_ref[0:1, :], st_scr[0], False)
    of_ref[...] = o_f
    st_scr[0] = st_f
    o_b, st_b = _hgrn_direction(qb_ref[...], fb_ref[...], vb_ref[...], lb_ref[1:2, :], st_scr[1], True)
    ob_ref[...] = o_b
    st_scr[1] = st_b


def hgrn_scan(p_a, lb, batch, t_len, c_len):
    na = p_a.shape[0]
    tt = HGRN_TILE
    n_c, n_t = c_len // tt, t_len // tt
    off_c = batch * n_t

    def rb_f(b, s):
        return jnp.where(s < n_c, off_c + b * n_c + s, b * n_t + (s - n_c))

    def rb_b(b, s):
        return jnp.where(s < n_c, off_c + b * n_c + (n_c - 1 - s), b * n_t + (n_t - 1 - (s - n_c)))

    def spec(rb, col0):
        return pl.BlockSpec((tt, A_DK), lambda b, h, s: (rb(b, s), col0 + h))

    out_sds = jax.ShapeDtypeStruct((na, A_WIDTH), F32)
    return pl.pallas_call(
        _hgrn_kernel,
        grid=(batch, A_HEADS, n_c + n_t),
        in_specs=[spec(rb_f, 0), spec(rb_f, A_HEADS), spec(rb_f, 3 * A_HEADS),
                  spec(rb_b, 0), spec(rb_b, 2 * A_HEADS), spec(rb_b, 3 * A_HEADS),
                  pl.BlockSpec((2, A_DK), lambda b, h, s: (0, h))],
        out_specs=[spec(rb_f, 0), spec(rb_b, 0)],
        out_shape=[out_sds, out_sds],
        scratch_shapes=[pltpu.VMEM((2, A_DK, A_DK), F32)],
        compiler_params=_cparams(("parallel", "parallel", "arbitrary")),
        name="hgrn_scan",
    )(p_a, p_a, p_a, p_a, p_a, p_a, lb)


def _hgrn_finish_kernel(of_ref, ob_ref, g_ref, gn_ref, o_ref):
    o = of_ref[...] + ob_ref[...]
    gate = _silu(g_ref[...])
    for h in range(A_HEADS):
        sl = slice(h * A_DK, (h + 1) * A_DK)
        oh = o[:, sl]
        y = oh * lax.rsqrt(jnp.mean(oh * oh, axis=-1, keepdims=True) + EPS) * gn_ref[...]
        o_ref[:, sl] = (y * gate[:, sl]).astype(o_ref.dtype)


def hgrn_finish(o_f, o_b, p_a, onorm_g):
    na = o_f.shape[0]
    blk = pl.BlockSpec((ROW_TILE, A_WIDTH), lambda i: (i, 0))
    return pl.pallas_call(
        _hgrn_finish_kernel,
        grid=(na // ROW_TILE,),
        in_specs=[blk, blk, pl.BlockSpec((ROW_TILE, A_WIDTH), lambda i: (i, 4)),
                  pl.BlockSpec((1, A_DK), lambda i: (0, 0))],
        out_specs=blk,
        out_shape=jax.ShapeDtypeStruct((na, A_WIDTH), BF16),
        compiler_params=_cparams(("parallel",)),
        name="hgrn_finish",
    )(o_f, o_b, p_a, onorm_g)


def _diff_attn_kernel(q_ref, kc_ref, vc_ref, k_ref, v_ref, lam_ref, gn_ref, o_ref,
                      m_scr, l_scr, acc_scr, *, n_q, n_k, unroll, lambda_init):
    i = pl.program_id(2)
    tq = q_ref.shape[0]
    lane = lax.broadcasted_iota(jnp.int32, (1, B_DV), 1)
    q = q_ref[...] * (B_HALF ** -0.5)
    zero = jnp.zeros_like(q)
    qm = (jnp.where(lane < B_HALF, q, zero), jnp.where(lane >= B_HALF, q, zero))

    m_scr[...] = jnp.full_like(m_scr, NEG_INF)
    l_scr[...] = jnp.zeros_like(l_scr)
    acc_scr[...] = jnp.zeros_like(acc_scr)

    def slabs(s):
        return [s[:, c * LANES:(c + 1) * LANES] for c in range(s.shape[1] // LANES)]

    def update(kvs):
        for m in range(2):
            ss = [_nt_dot(qm[m], kb) for kb, _ in kvs]
            part = functools.reduce(jnp.maximum, [sl for s in ss for sl in slabs(s)])
            m_old = m_scr[m]
            m_new = jnp.maximum(m_old, jnp.broadcast_to(jnp.max(part, axis=-1, keepdims=True), part.shape))
            alpha = jnp.exp(m_old - m_new)
            lsum = alpha * l_scr[m]
            acc = alpha * acc_scr[m]
            for s, (_, vb) in zip(ss, kvs):
                ps = [jnp.exp(sl - m_new) for sl in slabs(s)]
                lsum = lsum + functools.reduce(jnp.add, ps)
                p = jnp.concatenate([p_.astype(BF16) for p_ in ps], axis=1)
                acc = acc + jnp.dot(p, vb, preferred_element_type=F32)
            l_scr[m] = lsum
            acc_scr[m] = acc
            m_scr[m] = m_new

    update([(kc_ref[...], vc_ref[...])])

    def body(j, carry):
        kvs = []
        for u in range(unroll):
            start = pl.multiple_of((j * unroll + u) * DIFF_TK, DIFF_TK)
            kvs.append((k_ref[pl.ds(start, DIFF_TK), :], v_ref[pl.ds(start, DIFF_TK), :]))
        update(kvs)
        return carry

    lax.fori_loop(0, jnp.where(i < n_q, n_k, 0), body, 0)

    lp = lam_ref[...]
    lam = (jnp.exp(jnp.sum(lp[0:1] * lp[1:2], axis=-1, keepdims=True))
           - jnp.exp(jnp.sum(lp[2:3] * lp[3:4], axis=-1, keepdims=True)) + lambda_init)
    l0 = jnp.sum(l_scr[0], axis=-1, keepdims=True)
    l1 = jnp.sum(l_scr[1], axis=-1, keepdims=True)
    o = acc_scr[0] / l0 - lam * (acc_scr[1] / l1)
    y = o * lax.rsqrt(jnp.mean(o * o, axis=-1, keepdims=True) + EPS) * gn_ref[...]
    o_ref[...] = (y * (1.0 - lambda_init)).astype(o_ref.dtype)


def diff_attention(p_b, lam_p, subln_g, lambda_init, batch, t_len, c_len):
    na = p_b.shape[0]
    tq = ATTN_TQ
    assert c_len == tq
    n_q = t_len // tq
    off_c = batch * n_q

    def q_blk(b, i):
        return jnp.where(i < n_q, b * n_q + i, off_c + b)

    unroll = min(DIFF_UNROLL, t_len // DIFF_TK)
    assert t_len % (unroll * DIFF_TK) == 0
    kern = functools.partial(_diff_attn_kernel, n_q=n_q, n_k=t_len // (unroll * DIFF_TK), unroll=unroll,
                             lambda_init=lambda_init)
    return pl.pallas_call(
        kern,
        grid=(batch, B_HEADS, n_q + 1),
        in_specs=[
            pl.BlockSpec((tq, B_DV), lambda b, h, i: (q_blk(b, i), h)),
            pl.BlockSpec((c_len, B_DV), lambda b, h, i: (off_c + b, B_HEADS + h)),
            pl.BlockSpec((c_len, B_DV), lambda b, h, i: (off_c + b, 2 * B_HEADS + h)),
            pl.BlockSpec((t_len, B_DV), lambda b, h, i: (b, B_HEADS + h)),
            pl.BlockSpec((t_len, B_DV), lambda b, h, i: (b, 2 * B_HEADS + h)),
            pl.BlockSpec((4, B_HALF), lambda b, h, i: (0, 0)),
            pl.BlockSpec((1, B_DV), lambda b, h, i: (0, 0)),
        ],
        out_specs=pl.BlockSpec((tq, B_DV), lambda b, h, i: (q_blk(b, i), h)),
        out_shape=jax.ShapeDtypeStruct((na, B_WIDTH), BF16),
        scratch_shapes=[pltpu.VMEM((2, tq, LANES), F32), pltpu.VMEM((2, tq, LANES), F32),
                        pltpu.VMEM((2, tq, B_DV), F32)],
        compiler_params=_cparams(("parallel", "parallel", "arbitrary")),
        name="diff_attention",
    )(p_b, p_b, p_b, p_b, p_b, lam_p, subln_g)


def _swa_kernel(q_ref, kc_ref, vc_ref, k_ref, v_ref, sink_ref, o_ref, *, t_len):
    i = pl.program_id(2)
    tq, gw = q_ref.shape
    win = tq + 2 * C_WINDOW
    a = i * tq
    start = pl.multiple_of(jnp.clip(a - C_WINDOW, 0, t_len - win), C_WINDOW)
    kw = k_ref[pl.ds(start, win), :]
    vw = v_ref[pl.ds(start, win), :]
    kc, vc = kc_ref[...], vc_ref[...]
    qpos = a + lax.broadcasted_iota(jnp.int32, (tq, 1), 0)
    kpos = start + lax.broadcasted_iota(jnp.int32, (1, win), 1)
    valid = jnp.abs(qpos - kpos) <= C_WINDOW
    lane = lax.broadcasted_iota(jnp.int32, (1, gw), 1)
    q = q_ref[...] * (C_HEAD_DIM ** -0.5)
    sink_row = sink_ref[...]
    out = jnp.zeros((tq, gw), F32)
    for j in range(C_GROUP):
        in_head = (lane // C_HEAD_DIM) == j
        qj = jnp.where(in_head, q, jnp.zeros_like(q))
        sink = jnp.sum(jnp.where(lane == j * C_HEAD_DIM, sink_row, 0.0), axis=-1, keepdims=True)
        s_c = _nt_dot(qj, kc)
        s_w = jnp.where(valid, _nt_dot(qj, kw), NEG_INF)
        m = jnp.maximum(jnp.maximum(jnp.max(s_c, axis=-1, keepdims=True),
                                    jnp.max(s_w, axis=-1, keepdims=True)), sink)
        p_c = jnp.exp(s_c - m)
        p_w = jnp.exp(s_w - m)
        den = (jnp.exp(sink - m) + jnp.sum(p_c, axis=-1, keepdims=True)
               + jnp.sum(p_w, axis=-1, keepdims=True))
        o = (jnp.dot(p_c.astype(BF16), vc, preferred_element_type=F32)
             + jnp.dot(p_w.astype(BF16), vw, preferred_element_type=F32)) / den
        out = out + jnp.where(in_head, o, 0.0)
    o_ref[...] = out.astype(o_ref.dtype)


def window_attention(p1, sink_exp, batch, t_len, c_len):
    tq = ATTN_TQ
    gw = C_GROUP * C_HEAD_DIM
    n_q = t_len // tq
    off_c = batch * (t_len // c_len)
    kern = functools.partial(_swa_kernel, t_len=t_len)
    return pl.pallas_call(
        kern,
        grid=(batch, C_KV_HEADS, n_q),
        in_specs=[
            pl.BlockSpec((tq, gw), lambda b, g, i: (b * n_q + i, g)),
            pl.BlockSpec((c_len, gw), lambda b, g, i: (off_c + b, C_KV_HEADS + g)),
            pl.BlockSpec((c_len, gw), lambda b, g, i: (off_c + b, 2 * C_KV_HEADS + g)),
            pl.BlockSpec((t_len, gw), lambda b, g, i: (b, C_KV_HEADS + g)),
            pl.BlockSpec((t_len, gw), lambda b, g, i: (b, 2 * C_KV_HEADS + g)),
            pl.BlockSpec((None, 1, gw), lambda b, g, i: (g, 0, 0)),
        ],
        out_specs=pl.BlockSpec((tq, gw), lambda b, g, i: (b * n_q + i, g)),
        out_shape=jax.ShapeDtypeStruct((batch * t_len, C_HEADS * C_HEAD_DIM), BF16),
        compiler_params=_cparams(("parallel", "parallel", "arbitrary")),
        name="window_attention",
    )(p1, p1, p1, p1, p1, sink_exp)


def _to_row_tiles(ref, val):
    for s in range(val.shape[1] // LANES):
        ref[:, s, :] = val[:, s * LANES:(s + 1) * LANES]


def _from_row_tiles(ref):
    return jnp.concatenate([ref[:, s, :] for s in range(ref.shape[1])], axis=1)


def _out_proj_kernel(ya_ref, yb_ref, wa_ref, wb_ref, x_ref, g1_ref, g_ref, sc_ref, sh_ref, rw_ref,
                     xo_ref, h_ref, lg_ref):
    y = (jnp.dot(ya_ref[...], wa_ref[...], preferred_element_type=F32)
         + jnp.dot(yb_ref[...], wb_ref[...], preferred_element_type=F32))
    xn = x_ref[...] + g1_ref[...] * y
    xo_ref[...] = xn
    hn = xn * lax.rsqrt(jnp.mean(xn * xn, axis=-1, keepdims=True) + EPS) * g_ref[...]
    h2 = hn * (1.0 + sc_ref[...]) + sh_ref[...]
    _to_row_tiles(h_ref, h2)
    lg_ref[...] = lax.dot_general(rw_ref[...], h2, (((1,), (1,)), ((), ())),
                                  preferred_element_type=F32, precision=HIGHEST)


def out_proj(ya, yb_arr, yb_col, w_out, x_all, mods, g2n, router_wt, rows, mod_idx):
    d = x_all.shape[1]
    half = d // 2
    ne = router_wt.shape[0]
    row = lambda i: (i, 0)
    return pl.pallas_call(
        _out_proj_kernel,
        grid=(rows // ROW_TILE,),
        in_specs=[
            pl.BlockSpec((ROW_TILE, half), row),
            pl.BlockSpec((ROW_TILE, half), lambda i: (i, yb_col)),
            pl.BlockSpec((half, d), lambda i: (0, 0)),
            pl.BlockSpec((half, d), lambda i: (1, 0)),
            pl.BlockSpec((ROW_TILE, d), row),
            pl.BlockSpec((None, 1, d), lambda i: (mod_idx(i), 0, 2)),
            pl.BlockSpec((1, d), lambda i: (0, 0)),
            pl.BlockSpec((None, 1, d), lambda i: (mod_idx(i), 0, 4)),
            pl.BlockSpec((None, 1, d), lambda i: (mod_idx(i), 0, 3)),
            pl.BlockSpec((ne, d), lambda i: (0, 0)),
        ],
        out_specs=[pl.BlockSpec((ROW_TILE, d), row),
                   pl.BlockSpec((ROW_TILE, d // LANES, LANES), lambda i: (i, 0, 0)),
                   pl.BlockSpec((ne, ROW_TILE), lambda i: (0, i))],
        out_shape=[jax.ShapeDtypeStruct((rows, d), F32),
                   jax.ShapeDtypeStruct((rows, d // LANES, LANES), F32),
                   jax.ShapeDtypeStruct((ne, rows), F32)],
        compiler_params=_cparams(("parallel",)),
        name="out_proj",
    )(ya, yb_arr, w_out, w_out, x_all, mods, g2n, mods, mods, router_wt)


def _router_kernel(lg_ref, rb_ref, ids_ref, w_ref, cnt_ref, carry):
    @pl.when(pl.program_id(0) == 0)
    def _():
        carry[...] = jnp.zeros_like(carry)

    tr = lg_ref.shape[1]
    aff = jax.nn.sigmoid(lg_ref[...])
    sel = aff + rb_ref[...]
    rows = [sel[e:e + 1] for e in range(N_EXPERTS)]

    def beats(a, b, a_first):
        return jnp.logical_or(a > b, jnp.logical_and(a == b, a_first))

    def rank_among(vals, j):
        r = jnp.zeros(vals[0].shape, jnp.int32)
        for i2 in range(len(vals)):
            if i2 != j:
                r = r + beats(vals[i2], vals[j], i2 < j).astype(jnp.int32)
        return r

    rank, gscore = [], []
    for g in range(N_GROUPS):
        grp_rows = rows[g * EXPERTS_PER_GROUP:(g + 1) * EXPERTS_PER_GROUP]
        grp_rank = [rank_among(grp_rows, j) for j in range(EXPERTS_PER_GROUP)]
        rank += grp_rank
        gscore.append(sum(jnp.where(grp_rank[j] < TOP_K, grp_rows[j], 0.0) for j in range(EXPERTS_PER_GROUP)))
    chosen = [rank_among(gscore, g) == 0 for g in range(N_GROUPS)]
    onehot = []
    for k in range(TOP_K):
        hot_rows = [jnp.logical_and(chosen[e // EXPERTS_PER_GROUP], rank[e] == k).astype(F32)
                    for e in range(N_EXPERTS)]
        onehot.append(jnp.concatenate(hot_rows, axis=0))

    e_idx = lax.broadcasted_iota(jnp.int32, (N_EXPERTS, 1), 0).astype(F32)
    picked = [jnp.sum(onehot[k] * aff, axis=0, keepdims=True) for k in range(TOP_K)]
    denom = picked[0] + picked[1]
    before = (lax.broadcasted_iota(jnp.int32, (tr, tr), 0)
              < lax.broadcasted_iota(jnp.int32, (tr, tr), 1)).astype(BF16)
    both = onehot[0] + onehot[1]
    seen = carry[:, 0:1] + jnp.dot(both.astype(BF16), before, preferred_element_type=F32)
    id_rows = [jnp.sum(onehot[k] * e_idx, axis=0, keepdims=True) for k in range(TOP_K)]
    id_rows += [jnp.sum(onehot[k] * seen, axis=0, keepdims=True) for k in range(TOP_K)]
    pad = jnp.zeros((8 - 2 * TOP_K, tr), F32)
    ids_ref[...] = jnp.concatenate(id_rows + [pad], axis=0).astype(jnp.int32)
    w_ref[...] = jnp.concatenate([picked[0] / denom, picked[1] / denom,
                                  jnp.zeros((8 - TOP_K, tr), F32)], axis=0)
    carry[...] = carry[...] + jnp.sum(both, axis=1, keepdims=True)
    cnt_ref[...] = carry[...]


def moe_router(logits_t, router_b):
    ne, rows = logits_t.shape
    tr = ROW_TILE
    return pl.pallas_call(
        _router_kernel,
        grid=(rows // tr,),
        in_specs=[pl.BlockSpec((ne, tr), lambda i: (0, i)), pl.BlockSpec((ne, 1), lambda i: (0, 0))],
        out_specs=[pl.BlockSpec((8, tr), lambda i: (0, i)), pl.BlockSpec((8, tr), lambda i: (0, i)),
                   pl.BlockSpec((ne, LANES), lambda i: (0, 0))],
        out_shape=[jax.ShapeDtypeStruct((8, rows), jnp.int32), jax.ShapeDtypeStruct((8, rows), F32),
                   jax.ShapeDtypeStruct((ne, LANES), F32)],
        scratch_shapes=[pltpu.VMEM((ne, LANES), F32)],
        compiler_params=_cparams(("arbitrary",)),
        name="moe_router",
    )(logits_t, router_b.astype(F32).reshape(ne, 1))


def _tile_copy(src, dst, sem):
    return pltpu.make_async_copy(src, dst, sem)


def _dispatch_kernel(dest_ref, h_ref, zero_hbm, xs_hbm, sem):
    del zero_hbm
    n = dest_ref.shape[-1]

    def start(r, carry):
        _tile_copy(h_ref.at[r // TOP_K], xs_hbm.at[dest_ref[0, 0, r]], sem).start()
        return carry

    lax.fori_loop(0, n, start, 0)

    def wait(r, carry):
        _tile_copy(h_ref.at[0], xs_hbm.at[0], sem).wait()
        return carry

    lax.fori_loop(0, n, wait, 0)


def moe_dispatch(h2, dest, n_rows):
    rows, ns, nl = h2.shape
    n_t = rows // MOE_TOK
    dest3 = dest.reshape(n_t, 1, MOE_TOK * TOP_K)
    zeros = jnp.zeros((n_rows, ns, nl), F32)
    return pl.pallas_call(
        _dispatch_kernel,
        grid=(n_t,),
        in_specs=[
            pl.BlockSpec((1, 1, MOE_TOK * TOP_K), lambda i: (i, 0, 0), memory_space=pltpu.SMEM),
            pl.BlockSpec((MOE_TOK, ns, nl), lambda i: (i, 0, 0)),
            pl.BlockSpec(memory_space=pl.ANY),
        ],
        out_specs=pl.BlockSpec(memory_space=pl.ANY),
        out_shape=jax.ShapeDtypeStruct((n_rows, ns, nl), F32),
        scratch_shapes=[pltpu.SemaphoreType.DMA],
        input_output_aliases={2: 0},
        compiler_params=_cparams(("arbitrary",)),
        name="moe_dispatch",
    )(dest3, h2, zeros)


def _expert_kernel(be_ref, nv_ref, x_ref, wg_ref, wu_ref, wd_ref, o_ref):
    del be_ref
    i = pl.program_id(0)

    @pl.when(i < nv_ref[0])
    def _():
        x = _from_row_tiles(x_ref).astype(BF16)
        hid = (_silu(jnp.dot(x, wg_ref[...], preferred_element_type=F32))
               * jnp.dot(x, wu_ref[...], preferred_element_type=F32))
        _to_row_tiles(o_ref, jnp.dot(hid.astype(BF16), wd_ref[...], preferred_element_type=F32))

    @pl.when(i >= nv_ref[0])
    def _():
        o_ref[...] = jnp.zeros_like(o_ref)


def moe_experts(x_sorted, block_e, n_valid, w_gate, w_up, w_down):
    n_rows, ns, nl = x_sorted.shape
    d, de = w_gate.shape[1], w_gate.shape[2]
    n_blocks = n_rows // MOE_TILE
    grid_spec = pltpu.PrefetchScalarGridSpec(
        num_scalar_prefetch=2,
        grid=(n_blocks,),
        in_specs=[
            pl.BlockSpec((MOE_TILE, ns, nl), lambda i, be, nv: (i, 0, 0)),
            pl.BlockSpec((None, d, de), lambda i, be, nv: (be[i], 0, 0)),
            pl.BlockSpec((None, d, de), lambda i, be, nv: (be[i], 0, 0)),
            pl.BlockSpec((None, de, d), lambda i, be, nv: (be[i], 0, 0)),
        ],
        out_specs=pl.BlockSpec((MOE_TILE, ns, nl), lambda i, be, nv: (i, 0, 0)),
    )
    return pl.pallas_call(
        _expert_kernel,
        grid_spec=grid_spec,
        out_shape=jax.ShapeDtypeStruct((n_rows, ns, nl), F32),
        compiler_params=_cparams(("arbitrary",)),
        name="moe_experts",
    )(block_e, n_valid, x_sorted, w_gate, w_up, w_down)


def _combine_kernel(dest_ref, ys_hbm, w_ref, x_ref, g2_ref, gf_ref, o_ref, buf, sem, *, final_norm):
    n = dest_ref.shape[-1]

    def start(r, carry):
        _tile_copy(ys_hbm.at[dest_ref[0, 0, r]], buf.at[r % TOP_K, r // TOP_K], sem).start()
        return carry

    lax.fori_loop(0, n, start, 0)

    def wait(r, carry):
        _tile_copy(ys_hbm.at[0], buf.at[0, 0], sem).wait()
        return carry

    lax.fori_loop(0, n, wait, 0)
    w = w_ref[...]
    f = w[:, 0:1] * _from_row_tiles(buf.at[0]) + w[:, 1:2] * _from_row_tiles(buf.at[1])
    xn = x_ref[...] + g2_ref[...] * f
    if final_norm:
        xn = xn * lax.rsqrt(jnp.mean(xn * xn, axis=-1, keepdims=True) + EPS) * gf_ref[...]
    o_ref[...] = xn


def moe_combine(y_sorted, dest, wts, x_new, mods, final_g, final_norm, mod_idx_tok):
    rows, d = x_new.shape
    n_t = rows // MOE_TOK
    dest3 = dest.reshape(n_t, 1, MOE_TOK * TOP_K)
    kern = functools.partial(_combine_kernel, final_norm=final_norm)
    return pl.pallas_call(
        kern,
        grid=(n_t,),
        in_specs=[
            pl.BlockSpec((1, 1, MOE_TOK * TOP_K), lambda i: (i, 0, 0), memory_space=pltpu.SMEM),
            pl.BlockSpec(memory_space=pl.ANY),
            pl.BlockSpec((MOE_TOK, TOP_K), lambda i: (i, 0)),
            pl.BlockSpec((MOE_TOK, d), lambda i: (i, 0)),
            pl.BlockSpec((None, 1, d), lambda i: (mod_idx_tok(i), 0, 5)),
            pl.BlockSpec((1, d), lambda i: (0, 0)),
        ],
        out_specs=pl.BlockSpec((MOE_TOK, d), lambda i: (i, 0)),
        out_shape=jax.ShapeDtypeStruct((rows, d), F32),
        scratch_shapes=[pltpu.VMEM((TOP_K, MOE_TOK, d // LANES, LANES), F32), pltpu.SemaphoreType.DMA],
        compiler_params=_cparams(("arbitrary",)),
        name="moe_combine",
    )(dest3, y_sorted, wts, x_new, mods, final_g)


def _row_plan(ids, counts, n_tok):
    counts = counts[:, 0].astype(jnp.int32)
    padded = (counts + MOE_TILE - 1) // MOE_TILE * MOE_TILE
    pad_end = jnp.cumsum(padded)
    pad_start = pad_end - padded
    experts = jnp.arange(N_EXPERTS, dtype=jnp.int32)[:, None]
    dest = [jnp.sum(jnp.where(ids[k][None, :] == experts, pad_start[:, None], 0), axis=0) + ids[TOP_K + k]
            for k in range(TOP_K)]
    dest = jnp.stack(dest, axis=1).astype(jnp.int32)
    n_blocks = -(-(n_tok * TOP_K) // MOE_TILE) + N_EXPERTS
    block_start = jnp.arange(n_blocks, dtype=jnp.int32) * MOE_TILE
    block_e = jnp.minimum(jnp.sum(pad_end[None, :] <= block_start[:, None], axis=1), N_EXPERTS - 1)
    n_valid = (pad_end[-1] // MOE_TILE).reshape(1)
    return dest, block_e.astype(jnp.int32), n_valid.astype(jnp.int32), n_blocks


def moe_layer(h2, logits_t, router_b, w_gate, w_up, w_down, x_new, mods, final_g, final_norm, mod_idx_tok):
    n_tok = h2.shape[0]
    ids, wts8, counts = moe_router(logits_t, router_b)
    dest, block_e, n_valid, n_blocks = _row_plan(ids, counts, n_tok)
    wts = wts8[:TOP_K].T
    x_sorted = moe_dispatch(h2, dest, n_blocks * MOE_TILE)
    y_sorted = moe_experts(x_sorted, block_e, n_valid, w_gate, w_up, w_down)
    return moe_combine(y_sorted, dest, wts, x_new, mods, final_g, final_norm, mod_idx_tok)


def _rope_tables(batch, t_len, c_len):
    n_rows = t_len // GRID_W
    row = jnp.repeat(jnp.arange(n_rows), GRID_W).astype(F32)
    col = jnp.tile(jnp.arange(GRID_W), n_rows).astype(F32)
    half = B_HALF // 2
    inv = 1.0 / (ROPE_BASE ** (jnp.arange(0, half, 2, dtype=F32) / half))
    ar, ac = row[:, None] * inv, col[:, None] * inv
    ang = jnp.concatenate([ar, ar, ac, ac], axis=-1)
    cos, sin = jnp.cos(ang), jnp.sin(ang)
    first = (jnp.arange(B_HALF) % 32) < 16
    sin_a = jnp.where(first, -sin, 0.0)
    sin_b = jnp.where(first, 0.0, sin)

    def full(tab, ctx_val):
        lat = jnp.tile(jnp.tile(tab, (1, LANES // B_HALF)), (batch, 1))
        ctx = jnp.full((batch * c_len, LANES), ctx_val, F32)
        return jnp.concatenate([lat, ctx], axis=0)

    return full(cos, 1.0), full(sin_a, 0.0), full(sin_b, 0.0)


def kernel(x, c, ctx, c_ctx, ada_w, ada_b, norm_mix_g, norm_ffn_g, even_w_in, even_w_out, hgrn_lb_logits,
           hgrn_onorm_g, diff_lambda, diff_subln_g, odd_w_qkv, odd_w_out, swa_sink, router_w, router_b,
           moe_w_gate, moe_w_up, moe_w_down, final_norm_g):
    batch, t_len, d = x.shape
    c_len = ctx.shape[1]
    n_lat = batch * t_len
    assert t_len % ROW_TILE == 0 and (batch * c_len) % ROW_TILE == 0 and c_len == HGRN_TILE
    assert batch < 8 and t_len % DIFF_TK == 0 and t_len % GRID_W == 0

    x_all = jnp.concatenate([x.reshape(n_lat, d), ctx.reshape(batch * c_len, d)], axis=0)
    rope = _rope_tables(batch, t_len, c_len)

    def mod_idx_for(tile):
        per_batch = t_len // tile
        return lambda i: jnp.minimum(i // per_batch, batch)

    mod_idx = mod_idx_for(ROW_TILE)
    mod_idx_tok = mod_idx_for(MOE_TOK)

    cvec = jnp.zeros((8, d), F32).at[:batch].set(c).at[batch].set(c_ctx)
    mods_all = ada_modulation(cvec, ada_w, ada_b)
    lower_bounds = jnp.cumsum(jax.nn.softmax(hgrn_lb_logits.astype(F32), axis=0), axis=0)
    router_wt = router_w.astype(F32).T

    mods = mods_all[0].reshape(8, 1, 6 * d)
    g_mix = norm_mix_g[0].reshape(1, d)
    w_in = even_w_in[0].astype(BF16)
    n_a = 5 * A_WIDTH
    p_a = norm_mod_matmul(x_all, g_mix, mods, 1, 0, w_in[:, :n_a], rope, (0, 0), F32, mod_idx)
    p_b = norm_mod_matmul(x_all, g_mix, mods, 1, 0, w_in[:, n_a:], rope, (0, 2 * B_WIDTH), BF16, mod_idx)
    o_f, o_b = hgrn_scan(p_a, lower_bounds[0], batch, t_len, c_len)
    ya = hgrn_finish(o_f, o_b, p_a, hgrn_onorm_g[0].reshape(1, A_DK))
    lambda_init = 0.8 - 0.6 * math.exp(-0.3 * 0)
    yb = diff_attention(p_b, diff_lambda[0], diff_subln_g[0].reshape(1, B_DV), lambda_init,
                        batch, t_len, c_len)
    na = x_all.shape[0]
    x_new, h2, logits = out_proj(ya, yb, 0, even_w_out[0].astype(BF16), x_all, mods,
                                 norm_ffn_g[0].reshape(1, d), router_wt, na, mod_idx)
    x_all = moe_layer(h2, logits, router_b, moe_w_gate[0].astype(BF16), moe_w_up[0].astype(BF16),
                      moe_w_down[0].astype(BF16), x_new, mods, final_norm_g.reshape(1, d), False,
                      mod_idx_tok)

    mods = mods_all[1].reshape(8, 1, 6 * d)
    w_qkv = odd_w_qkv[0]
    q_cols = C_HEADS * C_HEAD_DIM
    kv_cols = C_KV_HEADS * C_HEAD_DIM

    def rep_heads(wkv):
        wkv = wkv.reshape(d, C_KV_HEADS, 1, C_HEAD_DIM)
        return jnp.broadcast_to(wkv, (d, C_KV_HEADS, C_GROUP, C_HEAD_DIM)).reshape(d, q_cols)

    w1 = jnp.concatenate([w_qkv[:, :q_cols], rep_heads(w_qkv[:, q_cols:q_cols + kv_cols]),
                          rep_heads(w_qkv[:, q_cols + kv_cols:])], axis=1).astype(BF16)
    p1 = norm_mod_matmul(x_all, norm_mix_g[1].reshape(1, d), mods, 1, 0, w1, rope, (0, 2 * q_cols),
                         BF16, mod_idx)
    sink_exp = jnp.repeat(swa_sink[0].astype(F32), C_HEAD_DIM).reshape(C_KV_HEADS, 1, C_GROUP * C_HEAD_DIM)
    o1 = window_attention(p1, sink_exp, batch, t_len, c_len)
    x_new, h2, logits = out_proj(o1, o1, 1, odd_w_out[0].astype(BF16), x_all, mods,
                                 norm_ffn_g[1].reshape(1, d), router_wt, n_lat, mod_idx)
    out = moe_layer(h2, logits, router_b, moe_w_gate[1].astype(BF16), moe_w_up[1].astype(BF16),
                    moe_w_down[1].astype(BF16), x_new, mods, final_norm_g.reshape(1, d), True, mod_idx_tok)
    return out.reshape(batch, t_len, d)
```

```python
import functools
import math

import jax
import jax.numpy as jnp
from jax import lax
from jax.experimental import pallas as pl
from jax.experimental.pallas import tpu as pltpu

F32 = jnp.float32
BF16 = jnp.bfloat16
EPS = 1e-6
ROPE_BASE = 10000.0
GRID_W = 64

A_HEADS, A_DK = 4, 128
A_WIDTH = A_HEADS * A_DK
HGRN_CHUNK = 16
B_HEADS, B_HALF = 4, 64
B_DV = 2 * B_HALF
B_WIDTH = B_HEADS * B_DV
C_HEADS, C_KV_HEADS, C_HEAD_DIM, C_WINDOW = 16, 4, 64, 128
C_GROUP = C_HEADS // C_KV_HEADS
N_EXPERTS, N_GROUPS, TOP_K = 16, 4, 2
EXPERTS_PER_GROUP = N_EXPERTS // N_GROUPS

LANES = 128
ROW_TILE = 512
HGRN_TILE = 256
ATTN_TQ = 256
DIFF_TK = 512
DIFF_UNROLL = 4
MOE_TILE = 256
MOE_TOK = 256
DMA_UNROLL = 8
VMEM_LIMIT = 56 * 1024 * 1024
HIGHEST = lax.Precision.HIGHEST
NEG_INF = float("-inf")


def _cparams(sem):
    return pltpu.CompilerParams(dimension_semantics=sem, vmem_limit_bytes=VMEM_LIMIT)


def _nt_dot(a, b):
    return lax.dot_general(a, b, (((1,), (1,)), ((), ())), preferred_element_type=F32)


def _sigmoid(x):
    return 0.5 * jnp.tanh(0.5 * x) + 0.5


def _silu(x):
    return x * _sigmoid(x)


def _ada_kernel(c_ref, w_ref, b_ref, o_ref):
    s = _silu(c_ref[...])
    o_ref[...] = jnp.dot(s, w_ref[...], preferred_element_type=F32, precision=HIGHEST) + b_ref[...]


def ada_modulation(cvec, ada_w, ada_b):
    n_layers, d, n6 = ada_w.shape
    tn = 512
    return pl.pallas_call(
        _ada_kernel,
        grid=(n_layers, n6 // tn),
        in_specs=[
            pl.BlockSpec((8, d), lambda l, j: (0, 0)),
            pl.BlockSpec((None, d, tn), lambda l, j: (l, 0, j)),
            pl.BlockSpec((None, 1, tn), lambda l, j: (l, 0, j)),
        ],
        out_specs=pl.BlockSpec((None, 8, tn), lambda l, j: (l, 0, j)),
        out_shape=jax.ShapeDtypeStruct((n_layers, 8, n6), F32),
        compiler_params=_cparams(("parallel", "parallel")),
        name="ada_modulation",
    )(cvec, ada_w, ada_b.reshape(n_layers, 1, n6))


def _rope_slab(x, cos, sin_a, sin_b):
    return x * cos + pltpu.roll(x, LANES - 16, 1) * sin_a + pltpu.roll(x, 16, 1) * sin_b


def _nmm_kernel(x_ref, g_ref, sc_ref, sh_ref, w_ref, cos_ref, sa_ref, sb_ref, o_ref, *, rope_cols, tn):
    x = x_ref[...]
    y = x * lax.rsqrt(jnp.mean(x * x, axis=-1, keepdims=True) + EPS) * g_ref[...]
    h = (y * (1.0 + sc_ref[...]) + sh_ref[...]).astype(BF16)
    n = w_ref.shape[1]
    for c0 in range(0, n, tn):
        acc = jnp.dot(h, w_ref[:, c0:c0 + tn], preferred_element_type=F32)
        if rope_cols[0] <= c0 < rope_cols[1]:
            cos, sa, sb = cos_ref[...], sa_ref[...], sb_ref[...]
            for c in range(tn // LANES):
                sl = slice(c * LANES, (c + 1) * LANES)
                o_ref[:, c0 + c * LANES:c0 + (c + 1) * LANES] = (
                    _rope_slab(acc[:, sl], cos, sa, sb).astype(o_ref.dtype))
        else:
            o_ref[:, c0:c0 + tn] = acc.astype(o_ref.dtype)


def norm_mod_matmul(x_all, g, mods, sc_chunk, sh_chunk, w, rope, rope_cols, out_dtype, mod_idx, tn=512):
    na, d = x_all.shape
    n = w.shape[1]
    cos, sa, sb = rope
    assert n % tn == 0 and rope_cols[0] % tn == 0 and rope_cols[1] % tn == 0
    kern = functools.partial(_nmm_kernel, rope_cols=rope_cols, tn=tn)
    row = lambda i: (i, 0)
    return pl.pallas_call(
        kern,
        grid=(na // ROW_TILE,),
        in_specs=[
            pl.BlockSpec((ROW_TILE, d), row),
            pl.BlockSpec((1, d), lambda i: (0, 0)),
            pl.BlockSpec((None, 1, d), lambda i: (mod_idx(i), 0, sc_chunk)),
            pl.BlockSpec((None, 1, d), lambda i: (mod_idx(i), 0, sh_chunk)),
            pl.BlockSpec((d, n), lambda i: (0, 0)),
            pl.BlockSpec((ROW_TILE, LANES), row),
            pl.BlockSpec((ROW_TILE, LANES), row),
            pl.BlockSpec((ROW_TILE, LANES), row),
        ],
        out_specs=pl.BlockSpec((ROW_TILE, n), row),
        out_shape=jax.ShapeDtypeStruct((na, n), out_dtype),
        compiler_params=_cparams(("parallel",)),
        name="norm_mod_matmul",
    )(x_all, g, mods, mods, w, cos, sa, sb)


def _hgrn_direction(q_raw, f_raw, v, lb_row, st, reverse):
    tt = q_raw.shape[0]
    n_chunks = tt // HGRN_CHUNK
    qs = _silu(q_raw) * (A_DK ** -0.5)
    f = lb_row + (1.0 - lb_row) * _sigmoid(f_raw)
    kk = 1.0 - f
    lf = jnp.log(f)

    r = lax.broadcasted_iota(jnp.int32, (tt, tt), 0)
    c = lax.broadcasted_iota(jnp.int32, (tt, tt), 1)
    same = (r // HGRN_CHUNK) == (c // HGRN_CHUNK)
    tri = (c >= r) if reverse else (c <= r)
    m_cum = jnp.where(jnp.logical_and(same, tri), 1.0, 0.0).astype(BF16)
    m_tot = jnp.where(same, 1.0, 0.0).astype(BF16)
    hi = lf.astype(BF16)
    r1 = lf - hi.astype(F32)
    mid = r1.astype(BF16)
    lo = (r1 - mid.astype(F32)).astype(BF16)

    def split_dot(m):
        return (jnp.dot(m, hi, preferred_element_type=F32) + jnp.dot(m, mid, preferred_element_type=F32)
                + jnp.dot(m, lo, preferred_element_type=F32))

    cum = split_dot(m_cum)
    tot = split_dot(m_tot)
    q_in = qs * jnp.exp(cum)
    k_out = kk * jnp.exp(tot - cum)

    pos = lax.broadcasted_iota(jnp.int32, (tt, 1), 0) % HGRN_CHUNK
    o = jnp.zeros((tt, A_DK), F32)
    for dist in range(HGRN_CHUNK):
        if dist == 0:
            ks, cs, vs = kk, cum, v
            diff = jnp.zeros_like(cum)
        else:
            shift = (tt - dist) if reverse else dist
            ks = pltpu.roll(kk, shift, 0)
            cs = pltpu.roll(cum, shift, 0)
            vs = pltpu.roll(v, shift, 0)
            valid = (pos <= HGRN_CHUNK - 1 - dist) if reverse else (pos >= dist)
            diff = jnp.where(valid, cum - cs, NEG_INF)
        score = jnp.sum(qs * ks * jnp.exp(diff), axis=-1, keepdims=True)
        o = o + score * vs

    v_t = v.T.astype(BF16)
    chunk_id = lax.broadcasted_iota(jnp.int32, (tt, 1), 0) // HGRN_CHUNK
    k_out_b = k_out.astype(BF16)
    q_in_b = q_in.astype(BF16)
    outs = [None] * n_chunks
    order = range(n_chunks - 1, -1, -1) if reverse else range(n_chunks)
    for ci in order:
        lo_r = ci * HGRN_CHUNK
        outs[ci] = o[lo_r:lo_r + HGRN_CHUNK] + _nt_dot(q_in_b[lo_r:lo_r + HGRN_CHUNK], st.astype(BF16))
        k_sel = jnp.where(chunk_id == ci, k_out_b, jnp.zeros_like(k_out_b))
        upd = jnp.dot(v_t, k_sel, preferred_element_type=F32)
        st = st * jnp.exp(tot[lo_r:lo_r + 1]) + upd
    return jnp.concatenate(outs, axis=0), st


def _hgrn_kernel(qf_ref, ff_ref, vf_ref, qb_ref, fb_ref, vb_ref, lb_ref, of_ref, ob_ref, st_scr):
    @pl.when(pl.program_id(2) == 0)
    def _():
        st_scr[...] = jnp.zeros_like(st_scr)

    o_f, st_f = _hgrn_direction(qf_ref[...], ff_ref[...], vf_ref[...], lb_ref[0:1, :], st_scr[0], False)
    of_ref[...] = o_f
    st_scr[0] = st_f
    o_b, st_b = _hgrn_direction(qb_ref[...], fb_ref[...], vb_ref[...], lb_ref[1:2, :], st_scr[1], True)
    ob_ref[...] = o_b
    st_scr[1] = st_b


def hgrn_scan(p_a, lb, batch, t_len, c_len):
    na = p_a.shape[0]
    tt = HGRN_TILE
    n_c, n_t = c_len // tt, t_len // tt
    off_c = batch * n_t

    def rb_f(b, s):
        return jnp.where(s < n_c, off_c + b * n_c + s, b * n_t + (s - n_c))

    def rb_b(b, s):
        return jnp.where(s < n_c, off_c + b * n_c + (n_c - 1 - s), b * n_t + (n_t - 1 - (s - n_c)))

    def spec(rb, col0):
        return pl.BlockSpec((tt, A_DK), lambda b, h, s: (rb(b, s), col0 + h))

    out_sds = jax.ShapeDtypeStruct((na, A_WIDTH), F32)
    return pl.pallas_call(
        _hgrn_kernel,
        grid=(batch, A_HEADS, n_c + n_t),
        in_specs=[spec(rb_f, 0), spec(rb_f, A_HEADS), spec(rb_f, 3 * A_HEADS),
                  spec(rb_b, 0), spec(rb_b, 2 * A_HEADS), spec(rb_b, 3 * A_HEADS),
                  pl.BlockSpec((2, A_DK), lambda b, h, s: (0, h))],
        out_specs=[spec(rb_f, 0), spec(rb_b, 0)],
        out_shape=[out_sds, out_sds],
        scratch_shapes=[pltpu.VMEM((2, A_DK, A_DK), F32)],
        compiler_params=_cparams(("parallel", "parallel", "arbitrary")),
        name="hgrn_scan",
    )(p_a, p_a, p_a, p_a, p_a, p_a, lb)


def _hgrn_finish_kernel(of_ref, ob_ref, g_ref, gn_ref, o_ref):
    o = of_ref[...] + ob_ref[...]
    gate = _silu(g_ref[...])
    for h in range(A_HEADS):
        sl = slice(h * A_DK, (h + 1) * A_DK)
        oh = o[:, sl]
        y = oh * lax.rsqrt(jnp.mean(oh * oh, axis=-1, keepdims=True) + EPS) * gn_ref[...]
        o_ref[:, sl] = (y * gate[:, sl]).astype(o_ref.dtype)


def hgrn_finish(o_f, o_b, p_a, onorm_g):
    na = o_f.shape[0]
    blk = pl.BlockSpec((ROW_TILE, A_WIDTH), lambda i: (i, 0))
    return pl.pallas_call(
        _hgrn_finish_kernel,
        grid=(na // ROW_TILE,),
        in_specs=[blk, blk, pl.BlockSpec((ROW_TILE, A_WIDTH), lambda i: (i, 4)),
                  pl.BlockSpec((1, A_DK), lambda i: (0, 0))],
        out_specs=blk,
        out_shape=jax.ShapeDtypeStruct((na, A_WIDTH), BF16),
        compiler_params=_cparams(("parallel",)),
        name="hgrn_finish",
    )(o_f, o_b, p_a, onorm_g)


def _diff_attn_kernel(q_ref, kc_ref, vc_ref, k_ref, v_ref, lam_ref, gn_ref, o_ref,
                      m_scr, l_scr, acc_scr, *, n_q, n_k, unroll, lambda_init):
    i = pl.program_id(2)
    tq = q_ref.shape[0]
    lane = lax.broadcasted_iota(jnp.int32, (1, B_DV), 1)
    q = q_ref[...] * (B_HALF ** -0.5)
    zero = jnp.zeros_like(q)
    qm = (jnp.where(lane < B_HALF, q, zero), jnp.where(lane >= B_HALF, q, zero))

    m_scr[...] = jnp.full_like(m_scr, NEG_INF)
    l_scr[...] = jnp.zeros_like(l_scr)
    acc_scr[...] = jnp.zeros_like(acc_scr)

    def slabs(s):
        return [s[:, c * LANES:(c + 1) * LANES] for c in range(s.shape[1] // LANES)]

    def update(kvs):
        for m in range(2):
            ss = [_nt_dot(qm[m], kb) for kb, _ in kvs]
            part = functools.reduce(jnp.maximum, [sl for s in ss for sl in slabs(s)])
            m_old = m_scr[m]
            m_new = jnp.maximum(m_old, jnp.broadcast_to(jnp.max(part, axis=-1, keepdims=True), part.shape))
            alpha = jnp.exp(m_old - m_new)
            lsum = alpha * l_scr[m]
            acc = alpha * acc_scr[m]
            for s, (_, vb) in zip(ss, kvs):
                ps = [jnp.exp(sl - m_new) for sl in slabs(s)]
                lsum = lsum + functools.reduce(jnp.add, ps)
                p = jnp.concatenate([p_.astype(BF16) for p_ in ps], axis=1)
                acc = acc + jnp.dot(p, vb, preferred_element_type=F32)
            l_scr[m] = lsum
            acc_scr[m] = acc
            m_scr[m] = m_new

    update([(kc_ref[...], vc_ref[...])])

    def body(j, carry):
        kvs = []
        for u in range(unroll):
            start = pl.multiple_of((j * unroll + u) * DIFF_TK, DIFF_TK)
            kvs.append((k_ref[pl.ds(start, DIFF_TK), :], v_ref[pl.ds(start, DIFF_TK), :]))
        update(kvs)
        return carry

    lax.fori_loop(0, jnp.where(i < n_q, n_k, 0), body, 0)

    lp = lam_ref[...]
    lam = (jnp.exp(jnp.sum(lp[0:1] * lp[1:2], axis=-1, keepdims=True))
           - jnp.exp(jnp.sum(lp[2:3] * lp[3:4], axis=-1, keepdims=True)) + lambda_init)
    l0 = jnp.sum(l_scr[0], axis=-1, keepdims=True)
    l1 = jnp.sum(l_scr[1], axis=-1, keepdims=True)
    o = acc_scr[0] / l0 - lam * (acc_scr[1] / l1)
    y = o * lax.rsqrt(jnp.mean(o * o, axis=-1, keepdims=True) + EPS) * gn_ref[...]
    o_ref[...] = (y * (1.0 - lambda_init)).astype(o_ref.dtype)


def diff_attention(p_b, lam_p, subln_g, lambda_init, batch, t_len, c_len):
    na = p_b.shape[0]
    tq = ATTN_TQ
    assert c_len == tq
    n_q = t_len // tq
    off_c = batch * n_q

    def q_blk(b, i):
        return jnp.where(i < n_q, b * n_q + i, off_c + b)

    unroll = min(DIFF_UNROLL, t_len // DIFF_TK)
    assert t_len % (unroll * DIFF_TK) == 0
    kern = functools.partial(_diff_attn_kernel, n_q=n_q, n_k=t_len // (unroll * DIFF_TK), unroll=unroll,
                             lambda_init=lambda_init)
    return pl.pallas_call(
        kern,
        grid=(batch, B_HEADS, n_q + 1),
        in_specs=[
            pl.BlockSpec((tq, B_DV), lambda b, h, i: (q_blk(b, i), h)),
            pl.BlockSpec((c_len, B_DV), lambda b, h, i: (off_c + b, B_HEADS + h)),
            pl.BlockSpec((c_len, B_DV), lambda b, h, i: (off_c + b, 2 * B_HEADS + h)),
            pl.BlockSpec((t_len, B_DV), lambda b, h, i: (b, B_HEADS + h)),
            pl.BlockSpec((t_len, B_DV), lambda b, h, i: (b, 2 * B_HEADS + h)),
            pl.BlockSpec((4, B_HALF), lambda b, h, i: (0, 0)),
            pl.BlockSpec((1, B_DV), lambda b, h, i: (0, 0)),
        ],
        out_specs=pl.BlockSpec((tq, B_DV), lambda b, h, i: (q_blk(b, i), h)),
        out_shape=jax.ShapeDtypeStruct((na, B_WIDTH), BF16),
        scratch_shapes=[pltpu.VMEM((2, tq, LANES), F32), pltpu.VMEM((2, tq, LANES), F32),
                        pltpu.VMEM((2, tq, B_DV), F32)],
        compiler_params=_cparams(("parallel", "parallel", "arbitrary")),
        name="diff_attention",
    )(p_b, p_b, p_b, p_b, p_b, lam_p, subln_g)


def _swa_kernel(q_ref, kc_ref, vc_ref, k_ref, v_ref, sink_ref, o_ref, *, t_len):
    i = pl.program_id(2)
    tq, gw = q_ref.shape
    win = tq + 2 * C_WINDOW
    a = i * tq
    start = pl.multiple_of(jnp.clip(a - C_WINDOW, 0, t_len - win), C_WINDOW)
    kw = k_ref[pl.ds(start, win), :]
    vw = v_ref[pl.ds(start, win), :]
    kc, vc = kc_ref[...], vc_ref[...]
    qpos = a + lax.broadcasted_iota(jnp.int32, (tq, 1), 0)
    kpos = start + lax.broadcasted_iota(jnp.int32, (1, win), 1)
    valid = jnp.abs(qpos - kpos) <= C_WINDOW
    lane = lax.broadcasted_iota(jnp.int32, (1, gw), 1)
    q = q_ref[...] * (C_HEAD_DIM ** -0.5)
    sink_row = sink_ref[...]
    out = jnp.zeros((tq, gw), F32)
    for j in range(C_GROUP):
        in_head = (lane // C_HEAD_DIM) == j
        qj = jnp.where(in_head, q, jnp.zeros_like(q))
        sink = jnp.sum(jnp.where(lane == j * C_HEAD_DIM, sink_row, 0.0), axis=-1, keepdims=True)
        s_c = _nt_dot(qj, kc)
        s_w = jnp.where(valid, _nt_dot(qj, kw), NEG_INF)
        m = jnp.maximum(jnp.maximum(jnp.max(s_c, axis=-1, keepdims=True),
                                    jnp.max(s_w, axis=-1, keepdims=True)), sink)
        p_c = jnp.exp(s_c - m)
        p_w = jnp.exp(s_w - m)
        den = (jnp.exp(sink - m) + jnp.sum(p_c, axis=-1, keepdims=True)
               + jnp.sum(p_w, axis=-1, keepdims=True))
        o = (jnp.dot(p_c.astype(BF16), vc, preferred_element_type=F32)
             + jnp.dot(p_w.astype(BF16), vw, preferred_element_type=F32)) / den
        out = out + jnp.where(in_head, o, 0.0)
    o_ref[...] = out.astype(o_ref.dtype)


def window_attention(p1, sink_exp, batch, t_len, c_len):
    tq = ATTN_TQ
    gw = C_GROUP * C_HEAD_DIM
    n_q = t_len // tq
    off_c = batch * (t_len // c_len)
    kern = functools.partial(_swa_kernel, t_len=t_len)
    return pl.pallas_call(
        kern,
        grid=(batch, C_KV_HEADS, n_q),
        in_specs=[
            pl.BlockSpec((tq, gw), lambda b, g, i: (b * n_q + i, g)),
            pl.BlockSpec((c_len, gw), lambda b, g, i: (off_c + b, C_KV_HEADS + g)),
            pl.BlockSpec((c_len, gw), lambda b, g, i: (off_c + b, 2 * C_KV_HEADS + g)),
            pl.BlockSpec((t_len, gw), lambda b, g, i: (b, C_KV_HEADS + g)),
            pl.BlockSpec((t_len, gw), lambda b, g, i: (b, 2 * C_KV_HEADS + g)),
            pl.BlockSpec((None, 1, gw), lambda b, g, i: (g, 0, 0)),
        ],
        out_specs=pl.BlockSpec((tq, gw), lambda b, g, i: (b * n_q + i, g)),
        out_shape=jax.ShapeDtypeStruct((batch * t_len, C_HEADS * C_HEAD_DIM), BF16),
        compiler_params=_cparams(("parallel", "parallel", "arbitrary")),
        name="window_attention",
    )(p1, p1, p1, p1, p1, sink_exp)


def _to_row_tiles(ref, val):
    for s in range(val.shape[1] // LANES):
        ref[:, s, :] = val[:, s * LANES:(s + 1) * LANES]


def _from_row_tiles(ref):
    return jnp.concatenate([ref[:, s, :] for s in range(ref.shape[1])], axis=1)


def _out_proj_kernel(ya_ref, yb_ref, wa_ref, wb_ref, x_ref, g1_ref, g_ref, sc_ref, sh_ref, rw_ref,
                     xo_ref, h_ref, lg_ref):
    y = (jnp.dot(ya_ref[...], wa_ref[...], preferred_element_type=F32)
         + jnp.dot(yb_ref[...], wb_ref[...], preferred_element_type=F32))
    xn = x_ref[...] + g1_ref[...] * y
    xo_ref[...] = xn
    hn = xn * lax.rsqrt(jnp.mean(xn * xn, axis=-1, keepdims=True) + EPS) * g_ref[...]
    h2 = hn * (1.0 + sc_ref[...]) + sh_ref[...]
    _to_row_tiles(h_ref, h2)
    lg_ref[...] = lax.dot_general(rw_ref[...], h2, (((1,), (1,)), ((), ())),
                                  preferred_element_type=F32, precision=HIGHEST)


def out_proj(ya, yb_arr, yb_col, w_out, x_all, mods, g2n, router_wt, rows, mod_idx):
    d = x_all.shape[1]
    half = d // 2
    ne = router_wt.shape[0]
    row = lambda i: (i, 0)
    return pl.pallas_call(
        _out_proj_kernel,
        grid=(rows // ROW_TILE,),
        in_specs=[
            pl.BlockSpec((ROW_TILE, half), row),
            pl.BlockSpec((ROW_TILE, half), lambda i: (i, yb_col)),
            pl.BlockSpec((half, d), lambda i: (0, 0)),
            pl.BlockSpec((half, d), lambda i: (1, 0)),
            pl.BlockSpec((ROW_TILE, d), row),
            pl.BlockSpec((None, 1, d), lambda i: (mod_idx(i), 0, 2)),
            pl.BlockSpec((1, d), lambda i: (0, 0)),
            pl.BlockSpec((None, 1, d), lambda i: (mod_idx(i), 0, 4)),
            pl.BlockSpec((None, 1, d), lambda i: (mod_idx(i), 0, 3)),
            pl.BlockSpec((ne, d), lambda i: (0, 0)),
        ],
        out_specs=[pl.BlockSpec((ROW_TILE, d), row),
                   pl.BlockSpec((ROW_TILE, d // LANES, LANES), lambda i: (i, 0, 0)),
                   pl.BlockSpec((ne, ROW_TILE), lambda i: (0, i))],
        out_shape=[jax.ShapeDtypeStruct((rows, d), F32),
                   jax.ShapeDtypeStruct((rows, d // LANES, LANES), F32),
                   jax.ShapeDtypeStruct((ne, rows), F32)],
        compiler_params=_cparams(("parallel",)),
        name="out_proj",
    )(ya, yb_arr, w_out, w_out, x_all, mods, g2n, mods, mods, router_wt)


def _router_kernel(lg_ref, rb_ref, ids_ref, w_ref, cnt_ref, carry):
    @pl.when(pl.program_id(0) == 0)
    def _():
        carry[...] = jnp.zeros_like(carry)

    tr = lg_ref.shape[1]
    aff = jax.nn.sigmoid(lg_ref[...])
    sel = aff + rb_ref[...]
    rows = [sel[e:e + 1] for e in range(N_EXPERTS)]

    def beats(a, b, a_first):
        return jnp.logical_or(a > b, jnp.logical_and(a == b, a_first))

    def rank_among(vals, j):
        r = jnp.zeros(vals[0].shape, jnp.int32)
        for i2 in range(len(vals)):
            if i2 != j:
                r = r + beats(vals[i2], vals[j], i2 < j).astype(jnp.int32)
        return r

    rank, gscore = [], []
    for g in range(N_GROUPS):
        grp_rows = rows[g * EXPERTS_PER_GROUP:(g + 1) * EXPERTS_PER_GROUP]
        grp_rank = [rank_among(grp_rows, j) for j in range(EXPERTS_PER_GROUP)]
        rank += grp_rank
        gscore.append(sum(jnp.where(grp_rank[j] < TOP_K, grp_rows[j], 0.0) for j in range(EXPERTS_PER_GROUP)))
    chosen = [rank_among(gscore, g) == 0 for g in range(N_GROUPS)]
    onehot = []
    for k in range(TOP_K):
        hot_rows = [jnp.logical_and(chosen[e // EXPERTS_PER_GROUP], rank[e] == k).astype(F32)
                    for e in range(N_EXPERTS)]
        onehot.append(jnp.concatenate(hot_rows, axis=0))

    e_idx = lax.broadcasted_iota(jnp.int32, (N_EXPERTS, 1), 0).astype(F32)
    picked = [jnp.sum(onehot[k] * aff, axis=0, keepdims=True) for k in range(TOP_K)]
    denom = picked[0] + picked[1]
    before = (lax.broadcasted_iota(jnp.int32, (tr, tr), 0)
              < lax.broadcasted_iota(jnp.int32, (tr, tr), 1)).astype(BF16)
    both = onehot[0] + onehot[1]
    seen = carry[:, 0:1] + jnp.dot(both.astype(BF16), before, preferred_element_type=F32)
    id_rows = [jnp.sum(onehot[k] * e_idx, axis=0, keepdims=True) for k in range(TOP_K)]
    id_rows += [jnp.sum(onehot[k] * seen, axis=0, keepdims=True) for k in range(TOP_K)]
    pad = jnp.zeros((8 - 2 * TOP_K, tr), F32)
    ids_ref[...] = jnp.concatenate(id_rows + [pad], axis=0).astype(jnp.int32)
    w_ref[...] = jnp.concatenate([picked[0] / denom, picked[1] / denom,
                                  jnp.zeros((8 - TOP_K, tr), F32)], axis=0)
    carry[...] = carry[...] + jnp.sum(both, axis=1, keepdims=True)
    cnt_ref[...] = carry[...]


def moe_router(logits_t, router_b):
    ne, rows = logits_t.shape
    tr = ROW_TILE
    return pl.pallas_call(
        _router_kernel,
        grid=(rows // tr,),
        in_specs=[pl.BlockSpec((ne, tr), lambda i: (0, i)), pl.BlockSpec((ne, 1), lambda i: (0, 0))],
        out_specs=[pl.BlockSpec((8, tr), lambda i: (0, i)), pl.BlockSpec((8, tr), lambda i: (0, i)),
                   pl.BlockSpec((ne, LANES), lambda i: (0, 0))],
        out_shape=[jax.ShapeDtypeStruct((8, rows), jnp.int32), jax.ShapeDtypeStruct((8, rows), F32),
                   jax.ShapeDtypeStruct((ne, LANES), F32)],
        scratch_shapes=[pltpu.VMEM((ne, LANES), F32)],
        compiler_params=_cparams(("arbitrary",)),
        name="moe_router",
    )(logits_t, router_b.astype(F32).reshape(ne, 1))


def _tile_copy(src, dst, sem):
    return pltpu.make_async_copy(src, dst, sem)


def _dispatch_kernel(dest_ref, h_ref, zero_hbm, xs_hbm, sem):
    del zero_hbm
    n_tok = h_ref.shape[0]

    def start(t, carry):
        for k in range(TOP_K):
            _tile_copy(h_ref.at[t], xs_hbm.at[dest_ref[0, 0, TOP_K * t + k]], sem).start()
        return carry

    lax.fori_loop(0, n_tok, start, 0, unroll=DMA_UNROLL)
    for k in range(TOP_K):
        _tile_copy(h_ref, xs_hbm.at[pl.ds(0, n_tok)], sem).wait()


def moe_dispatch(h2, dest, n_rows):
    rows, ns, nl = h2.shape
    n_t = rows // MOE_TOK
    dest3 = dest.reshape(n_t, 1, MOE_TOK * TOP_K)
    zeros = jnp.zeros((n_rows, ns, nl), F32)
    return pl.pallas_call(
        _dispatch_kernel,
        grid=(n_t,),
        in_specs=[
            pl.BlockSpec((1, 1, MOE_TOK * TOP_K), lambda i: (i, 0, 0), memory_space=pltpu.SMEM),
            pl.BlockSpec((MOE_TOK, ns, nl), lambda i: (i, 0, 0)),
            pl.BlockSpec(memory_space=pl.ANY),
        ],
        out_specs=pl.BlockSpec(memory_space=pl.ANY),
        out_shape=jax.ShapeDtypeStruct((n_rows, ns, nl), F32),
        scratch_shapes=[pltpu.SemaphoreType.DMA],
        input_output_aliases={2: 0},
        compiler_params=_cparams(("arbitrary",)),
        name="moe_dispatch",
    )(dest3, h2, zeros)


def _expert_kernel(be_ref, nv_ref, x_ref, wg_ref, wu_ref, wd_ref, o_ref, wg_s, wu_s, wd_s):
    i = pl.program_id(0)
    new_expert = jnp.logical_or(i == 0, be_ref[i] != be_ref[jnp.maximum(i - 1, 0)])

    @pl.when(new_expert)
    def _():
        wg_s[...] = wg_ref[...].astype(BF16)
        wu_s[...] = wu_ref[...].astype(BF16)
        wd_s[...] = wd_ref[...].astype(BF16)

    @pl.when(i < nv_ref[0])
    def _():
        x = _from_row_tiles(x_ref).astype(BF16)
        hid = (_silu(jnp.dot(x, wg_s[...], preferred_element_type=F32))
               * jnp.dot(x, wu_s[...], preferred_element_type=F32))
        _to_row_tiles(o_ref, jnp.dot(hid.astype(BF16), wd_s[...], preferred_element_type=F32))

    @pl.when(i >= nv_ref[0])
    def _():
        o_ref[...] = jnp.zeros_like(o_ref)


def moe_experts(x_sorted, block_e, n_valid, layer, w_gate, w_up, w_down):
    n_rows, ns, nl = x_sorted.shape
    d, de = w_gate.shape[2], w_gate.shape[3]
    n_blocks = n_rows // MOE_TILE
    grid_spec = pltpu.PrefetchScalarGridSpec(
        num_scalar_prefetch=2,
        grid=(n_blocks,),
        in_specs=[
            pl.BlockSpec((MOE_TILE, ns, nl), lambda i, be, nv: (i, 0, 0)),
            pl.BlockSpec((None, None, d, de), lambda i, be, nv: (layer, be[i], 0, 0)),
            pl.BlockSpec((None, None, d, de), lambda i, be, nv: (layer, be[i], 0, 0)),
            pl.BlockSpec((None, None, de, d), lambda i, be, nv: (layer, be[i], 0, 0)),
        ],
        out_specs=pl.BlockSpec((MOE_TILE, ns, nl), lambda i, be, nv: (i, 0, 0)),
        scratch_shapes=[pltpu.VMEM((d, de), BF16), pltpu.VMEM((d, de), BF16), pltpu.VMEM((de, d), BF16)],
    )
    return pl.pallas_call(
        _expert_kernel,
        grid_spec=grid_spec,
        out_shape=jax.ShapeDtypeStruct((n_rows, ns, nl), F32),
        compiler_params=_cparams(("arbitrary",)),
        name="moe_experts",
    )(block_e, n_valid, x_sorted, w_gate, w_up, w_down)


def _combine_kernel(dest_ref, ys_hbm, w_ref, x_ref, g2_ref, gf_ref, o_ref, buf, sem, *, final_norm):
    n_tok = buf.shape[1]

    def start(t, carry):
        for k in range(TOP_K):
            _tile_copy(ys_hbm.at[dest_ref[0, 0, TOP_K * t + k]], buf.at[k, t], sem).start()
        return carry

    lax.fori_loop(0, n_tok, start, 0, unroll=DMA_UNROLL)
    for k in range(TOP_K):
        _tile_copy(ys_hbm.at[pl.ds(0, n_tok)], buf.at[k], sem).wait()
    w = w_ref[...]
    f = w[:, 0:1] * _from_row_tiles(buf.at[0]) + w[:, 1:2] * _from_row_tiles(buf.at[1])
    xn = x_ref[...] + g2_ref[...] * f
    if final_norm:
        xn = xn * lax.rsqrt(jnp.mean(xn * xn, axis=-1, keepdims=True) + EPS) * gf_ref[...]
    o_ref[...] = xn


def moe_combine(y_sorted, dest, wts, x_new, mods, final_g, final_norm, mod_idx_tok):
    rows, d = x_new.shape
    n_t = rows // MOE_TOK
    dest3 = dest.reshape(n_t, 1, MOE_TOK * TOP_K)
    kern = functools.partial(_combine_kernel, final_norm=final_norm)
    return pl.pallas_call(
        kern,
        grid=(n_t,),
        in_specs=[
            pl.BlockSpec((1, 1, MOE_TOK * TOP_K), lambda i: (i, 0, 0), memory_space=pltpu.SMEM),
            pl.BlockSpec(memory_space=pl.ANY),
            pl.BlockSpec((MOE_TOK, TOP_K), lambda i: (i, 0)),
            pl.BlockSpec((MOE_TOK, d), lambda i: (i, 0)),
            pl.BlockSpec((None, 1, d), lambda i: (mod_idx_tok(i), 0, 5)),
            pl.BlockSpec((1, d), lambda i: (0, 0)),
        ],
        out_specs=pl.BlockSpec((MOE_TOK, d), lambda i: (i, 0)),
        out_shape=jax.ShapeDtypeStruct((rows, d), F32),
        scratch_shapes=[pltpu.VMEM((TOP_K, MOE_TOK, d // LANES, LANES), F32), pltpu.SemaphoreType.DMA],
        compiler_params=_cparams(("arbitrary",)),
        name="moe_combine",
    )(dest3, y_sorted, wts, x_new, mods, final_g)


def _row_plan(ids, counts, n_tok):
    counts = counts[:, 0].astype(jnp.int32)
    padded = (counts + MOE_TILE - 1) // MOE_TILE * MOE_TILE
    pad_end = jnp.cumsum(padded)
    pad_start = pad_end - padded
    experts = jnp.arange(N_EXPERTS, dtype=jnp.int32)[:, None]
    dest = [jnp.sum(jnp.where(ids[k][None, :] == experts, pad_start[:, None], 0), axis=0) + ids[TOP_K + k]
            for k in range(TOP_K)]
    dest = jnp.stack(dest, axis=1).astype(jnp.int32)
    n_blocks = -(-(n_tok * TOP_K) // MOE_TILE) + N_EXPERTS
    block_start = jnp.arange(n_blocks, dtype=jnp.int32) * MOE_TILE
    block_e = jnp.minimum(jnp.sum(pad_end[None, :] <= block_start[:, None], axis=1), N_EXPERTS - 1)
    n_valid = (pad_end[-1] // MOE_TILE).reshape(1)
    return dest, block_e.astype(jnp.int32), n_valid.astype(jnp.int32), n_blocks


def moe_layer(h2, logits_t, router_b, layer, w_gate, w_up, w_down, x_new, mods, final_g, final_norm,
              mod_idx_tok):
    n_tok = h2.shape[0]
    ids, wts8, counts = moe_router(logits_t, router_b)
    dest, block_e, n_valid, n_blocks = _row_plan(ids, counts, n_tok)
    wts = wts8[:TOP_K].T
    x_sorted = moe_dispatch(h2, dest, n_blocks * MOE_TILE)
    y_sorted = moe_experts(x_sorted, block_e, n_valid, layer, w_gate, w_up, w_down)
    return moe_combine(y_sorted, dest, wts, x_new, mods, final_g, final_norm, mod_idx_tok)


def _rope_tables(batch, t_len, c_len):
    n_rows = t_len // GRID_W
    row = jnp.repeat(jnp.arange(n_rows), GRID_W).astype(F32)
    col = jnp.tile(jnp.arange(GRID_W), n_rows).astype(F32)
    half = B_HALF // 2
    inv = 1.0 / (ROPE_BASE ** (jnp.arange(0, half, 2, dtype=F32) / half))
    ar, ac = row[:, None] * inv, col[:, None] * inv
    ang = jnp.concatenate([ar, ar, ac, ac], axis=-1)
    cos, sin = jnp.cos(ang), jnp.sin(ang)
    first = (jnp.arange(B_HALF) % 32) < 16
    sin_a = jnp.where(first, -sin, 0.0)
    sin_b = jnp.where(first, 0.0, sin)

    def full(tab, ctx_val):
        lat = jnp.tile(jnp.tile(tab, (1, LANES // B_HALF)), (batch, 1))
        ctx = jnp.full((batch * c_len, LANES), ctx_val, F32)
        return jnp.concatenate([lat, ctx], axis=0)

    return full(cos, 1.0), full(sin_a, 0.0), full(sin_b, 0.0)


def kernel(x, c, ctx, c_ctx, ada_w, ada_b, norm_mix_g, norm_ffn_g, even_w_in, even_w_out, hgrn_lb_logits,
           hgrn_onorm_g, diff_lambda, diff_subln_g, odd_w_qkv, odd_w_out, swa_sink, router_w, router_b,
           moe_w_gate, moe_w_up, moe_w_down, final_norm_g):
    batch, t_len, d = x.shape
    c_len = ctx.shape[1]
    n_lat = batch * t_len
    assert t_len % ROW_TILE == 0 and (batch * c_len) % ROW_TILE == 0 and c_len == HGRN_TILE
    assert batch < 8 and t_len % DIFF_TK == 0 and t_len % GRID_W == 0

    x_all = jnp.concatenate([x.reshape(n_lat, d), ctx.reshape(batch * c_len, d)], axis=0)
    rope = _rope_tables(batch, t_len, c_len)

    def mod_idx_for(tile):
        per_batch = t_len // tile
        return lambda i: jnp.minimum(i // per_batch, batch)

    mod_idx = mod_idx_for(ROW_TILE)
    mod_idx_tok = mod_idx_for(MOE_TOK)

    cvec = jnp.zeros((8, d), F32).at[:batch].set(c).at[batch].set(c_ctx)
    mods_all = ada_modulation(cvec, ada_w, ada_b)
    lower_bounds = jnp.cumsum(jax.nn.softmax(hgrn_lb_logits.astype(F32), axis=0), axis=0)
    router_wt = router_w.astype(F32).T

    mods = mods_all[0].reshape(8, 1, 6 * d)
    g_mix = norm_mix_g[0].reshape(1, d)
    w_in = even_w_in[0].astype(BF16)
    n_a = 5 * A_WIDTH
    p_a = norm_mod_matmul(x_all, g_mix, mods, 1, 0, w_in[:, :n_a], rope, (0, 0), F32, mod_idx)
    p_b = norm_mod_matmul(x_all, g_mix, mods, 1, 0, w_in[:, n_a:], rope, (0, 2 * B_WIDTH), BF16, mod_idx)
    o_f, o_b = hgrn_scan(p_a, lower_bounds[0], batch, t_len, c_len)
    ya = hgrn_finish(o_f, o_b, p_a, hgrn_onorm_g[0].reshape(1, A_DK))
    lambda_init = 0.8 - 0.6 * math.exp(-0.3 * 0)
    yb = diff_attention(p_b, diff_lambda[0], diff_subln_g[0].reshape(1, B_DV), lambda_init,
                        batch, t_len, c_len)
    na = x_all.shape[0]
    x_new, h2, logits = out_proj(ya, yb, 0, even_w_out[0].astype(BF16), x_all, mods,
                                 norm_ffn_g[0].reshape(1, d), router_wt, na, mod_idx)
    x_all = moe_layer(h2, logits, router_b, 0, moe_w_gate, moe_w_up, moe_w_down, x_new, mods,
                      final_norm_g.reshape(1, d), False, mod_idx_tok)

    mods = mods_all[1].reshape(8, 1, 6 * d)
    w_qkv = odd_w_qkv[0]
    q_cols = C_HEADS * C_HEAD_DIM
    kv_cols = C_KV_HEADS * C_HEAD_DIM

    def rep_heads(wkv):
        wkv = wkv.reshape(d, C_KV_HEADS, 1, C_HEAD_DIM)
        return jnp.broadcast_to(wkv, (d, C_KV_HEADS, C_GROUP, C_HEAD_DIM)).reshape(d, q_cols)

    w1 = jnp.concatenate([w_qkv[:, :q_cols], rep_heads(w_qkv[:, q_cols:q_cols + kv_cols]),
                          rep_heads(w_qkv[:, q_cols + kv_cols:])], axis=1).astype(BF16)
    p1 = norm_mod_matmul(x_all, norm_mix_g[1].reshape(1, d), mods, 1, 0, w1, rope, (0, 2 * q_cols),
                         BF16, mod_idx)
    sink_exp = jnp.repeat(swa_sink[0].astype(F32), C_HEAD_DIM).reshape(C_KV_HEADS, 1, C_GROUP * C_HEAD_DIM)
    o1 = window_attention(p1, sink_exp, batch, t_len, c_len)
    x_new, h2, logits = out_proj(o1, o1, 1, odd_w_out[0].astype(BF16), x_all, mods,
                                 norm_ffn_g[1].reshape(1, d), router_wt, n_lat, mod_idx)
    out = moe_layer(h2, logits, router_b, 1, moe_w_gate, moe_w_up, moe_w_down, x_new, mods,
                    final_norm_g.reshape(1, d), True, mod_idx_tok)
    return out.reshape(batch, t_len, d)
```

```python
import functools
import math

import jax
import jax.numpy as jnp
from jax import lax
from jax.experimental import pallas as pl
from jax.experimental.pallas import tpu as pltpu

F32 = jnp.float32
BF16 = jnp.bfloat16
EPS = 1e-6
ROPE_BASE = 10000.0
GRID_W = 64

A_HEADS, A_DK = 4, 128
A_WIDTH = A_HEADS * A_DK
HGRN_CHUNK = 16
B_HEADS, B_HALF = 4, 64
B_DV = 2 * B_HALF
B_WIDTH = B_HEADS * B_DV
C_HEADS, C_KV_HEADS, C_HEAD_DIM, C_WINDOW = 16, 4, 64, 128
C_GROUP = C_HEADS // C_KV_HEADS
N_EXPERTS, N_GROUPS, TOP_K = 16, 4, 2
EXPERTS_PER_GROUP = N_EXPERTS // N_GROUPS

LANES = 128
ROW_TILE = 512
HGRN_TILE = 256
HGRN_HEADS_PER_STEP = 1
ATTN_TQ = 256
DIFF_TK = 512
DIFF_UNROLL = 4
DIFF_TQ = 512
MOE_TILE = 256
MOE_TOK = 256
DMA_UNROLL = 8
VMEM_LIMIT = 56 * 1024 * 1024
HIGHEST = lax.Precision.HIGHEST
NEG_INF = float("-inf")


def _cparams(sem):
    return pltpu.CompilerParams(dimension_semantics=sem, vmem_limit_bytes=VMEM_LIMIT)


def _nt_dot(a, b):
    return lax.dot_general(a, b, (((1,), (1,)), ((), ())), preferred_element_type=F32)


def _sigmoid(x):
    return 0.5 * jnp.tanh(0.5 * x) + 0.5


def _silu(x):
    return x * _sigmoid(x)


def _ada_kernel(c_ref, w_ref, b_ref, o_ref):
    s = _silu(c_ref[...])
    o_ref[...] = jnp.dot(s, w_ref[...], preferred_element_type=F32, precision=HIGHEST) + b_ref[...]


def ada_modulation(cvec, ada_w, ada_b):
    n_layers, d, n6 = ada_w.shape
    tn = 512
    return pl.pallas_call(
        _ada_kernel,
        grid=(n_layers, n6 // tn),
        in_specs=[
            pl.BlockSpec((8, d), lambda l, j: (0, 0)),
            pl.BlockSpec((None, d, tn), lambda l, j: (l, 0, j)),
            pl.BlockSpec((None, 1, tn), lambda l, j: (l, 0, j)),
        ],
        out_specs=pl.BlockSpec((None, 8, tn), lambda l, j: (l, 0, j)),
        out_shape=jax.ShapeDtypeStruct((n_layers, 8, n6), F32),
        compiler_params=_cparams(("parallel", "parallel")),
        name="ada_modulation",
    )(cvec, ada_w, ada_b.reshape(n_layers, 1, n6))


def _rope_slab(x, cos, sin_a, sin_b):
    return x * cos + pltpu.roll(x, LANES - 16, 1) * sin_a + pltpu.roll(x, 16, 1) * sin_b


def _nmm_kernel(x_ref, g_ref, sc_ref, sh_ref, w_ref, cos_ref, sa_ref, sb_ref, *rest, rope_cols, tn):
    x = x_ref[...]
    y = x * lax.rsqrt(jnp.mean(x * x, axis=-1, keepdims=True) + EPS) * g_ref[...]
    h = (y * (1.0 + sc_ref[...]) + sh_ref[...]).astype(BF16)
    if len(rest) == 3:
        wt_ref, o_ref, ot_ref = rest
        ot_ref[...] = _nt_dot(wt_ref[...], h).astype(ot_ref.dtype)
    else:
        (o_ref,) = rest
    n = w_ref.shape[1]
    for c0 in range(0, n, tn):
        acc = jnp.dot(h, w_ref[:, c0:c0 + tn], preferred_element_type=F32)
        if rope_cols[0] <= c0 < rope_cols[1]:
            cos, sa, sb = cos_ref[...], sa_ref[...], sb_ref[...]
            for c in range(tn // LANES):
                sl = slice(c * LANES, (c + 1) * LANES)
                o_ref[:, c0 + c * LANES:c0 + (c + 1) * LANES] = (
                    _rope_slab(acc[:, sl], cos, sa, sb).astype(o_ref.dtype))
        else:
            o_ref[:, c0:c0 + tn] = acc.astype(o_ref.dtype)


def norm_mod_matmul(x_all, g, mods, sc_chunk, sh_chunk, w, rope, rope_cols, out_dtype, mod_idx, tn=512,
                    wt=None):
    na, d = x_all.shape
    n = w.shape[1]
    cos, sa, sb = rope
    assert n % tn == 0 and rope_cols[0] % tn == 0 and rope_cols[1] % tn == 0
    kern = functools.partial(_nmm_kernel, rope_cols=rope_cols, tn=tn)
    row = lambda i: (i, 0)
    in_specs = [
        pl.BlockSpec((ROW_TILE, d), row),
        pl.BlockSpec((1, d), lambda i: (0, 0)),
        pl.BlockSpec((None, 1, d), lambda i: (mod_idx(i), 0, sc_chunk)),
        pl.BlockSpec((None, 1, d), lambda i: (mod_idx(i), 0, sh_chunk)),
        pl.BlockSpec((d, n), lambda i: (0, 0)),
        pl.BlockSpec((ROW_TILE, LANES), row),
        pl.BlockSpec((ROW_TILE, LANES), row),
        pl.BlockSpec((ROW_TILE, LANES), row),
    ]
    out_specs = [pl.BlockSpec((ROW_TILE, n), row)]
    out_shape = [jax.ShapeDtypeStruct((na, n), out_dtype)]
    args = [x_all, g, mods, mods, w, cos, sa, sb]
    if wt is not None:
        nt = wt.shape[0]
        in_specs.append(pl.BlockSpec((nt, d), lambda i: (0, 0)))
        out_specs.append(pl.BlockSpec((nt, ROW_TILE), lambda i: (0, i)))
        out_shape.append(jax.ShapeDtypeStruct((nt, na), out_dtype))
        args.append(wt)
    outs = pl.pallas_call(
        kern,
        grid=(na // ROW_TILE,),
        in_specs=in_specs,
        out_specs=out_specs,
        out_shape=out_shape,
        compiler_params=_cparams(("parallel",)),
        name="norm_mod_matmul",
    )(*args)
    return outs[0] if wt is None else outs


def _hgrn_direction(q_raw, f_raw, v, lb_row, st, reverse):
    tt = q_raw.shape[0]
    n_chunks = tt // HGRN_CHUNK
    qs = _silu(q_raw) * (A_DK ** -0.5)
    f = lb_row + (1.0 - lb_row) * _sigmoid(f_raw)
    kk = 1.0 - f
    lf = jnp.log(f)

    r = lax.broadcasted_iota(jnp.int32, (tt, tt), 0)
    c = lax.broadcasted_iota(jnp.int32, (tt, tt), 1)
    same = (r // HGRN_CHUNK) == (c // HGRN_CHUNK)
    tri = (c >= r) if reverse else (c <= r)
    m_cum = jnp.where(jnp.logical_and(same, tri), 1.0, 0.0).astype(BF16)
    m_tot = jnp.where(same, 1.0, 0.0).astype(BF16)
    hi = lf.astype(BF16)
    r1 = lf - hi.astype(F32)
    mid = r1.astype(BF16)
    lo = (r1 - mid.astype(F32)).astype(BF16)

    def split_dot(m):
        return (jnp.dot(m, hi, preferred_element_type=F32) + jnp.dot(m, mid, preferred_element_type=F32)
                + jnp.dot(m, lo, preferred_element_type=F32))

    cum = split_dot(m_cum)
    tot = split_dot(m_tot)
    q_in = qs * jnp.exp(cum)
    k_out = kk * jnp.exp(tot - cum)

    pos = lax.broadcasted_iota(jnp.int32, (tt, 1), 0) % HGRN_CHUNK
    o = jnp.zeros((tt, A_DK), F32)
    for dist in range(HGRN_CHUNK):
        if dist == 0:
            ks, cs, vs = kk, cum, v
            diff = jnp.zeros_like(cum)
        else:
            shift = (tt - dist) if reverse else dist
            ks = pltpu.roll(kk, shift, 0)
            cs = pltpu.roll(cum, shift, 0)
            vs = pltpu.roll(v, shift, 0)
            valid = (pos <= HGRN_CHUNK - 1 - dist) if reverse else (pos >= dist)
            diff = jnp.where(valid, cum - cs, NEG_INF)
        score = jnp.sum(qs * ks * jnp.exp(diff), axis=-1, keepdims=True)
        o = o + score * vs

    v_t = v.T.astype(BF16)
    chunk_id = lax.broadcasted_iota(jnp.int32, (tt, 1), 0) // HGRN_CHUNK
    k_out_b = k_out.astype(BF16)
    q_in_b = q_in.astype(BF16)
    outs = [None] * n_chunks
    order = range(n_chunks - 1, -1, -1) if reverse else range(n_chunks)
    for ci in order:
        lo_r = ci * HGRN_CHUNK
        outs[ci] = o[lo_r:lo_r + HGRN_CHUNK] + _nt_dot(q_in_b[lo_r:lo_r + HGRN_CHUNK], st.astype(BF16))
        k_sel = jnp.where(chunk_id == ci, k_out_b, jnp.zeros_like(k_out_b))
        upd = jnp.dot(v_t, k_sel, preferred_element_type=F32)
        st = st * jnp.exp(tot[lo_r:lo_r + 1]) + upd
    return jnp.concatenate(outs, axis=0), st


def _hgrn_kernel(qf_ref, ff_ref, vf_ref, qb_ref, fb_ref, vb_ref, lb_ref, of_ref, ob_ref, st_scr):
    @pl.when(pl.program_id(2) == 0)
    def _():
        st_scr[...] = jnp.zeros_like(st_scr)

    for hh in range(HGRN_HEADS_PER_STEP):
        sl = slice(hh * A_DK, (hh + 1) * A_DK)
        o_f, st_f = _hgrn_direction(qf_ref[:, sl], ff_ref[:, sl], vf_ref[:, sl], lb_ref[0:1, sl],
                                    st_scr[2 * hh], False)
        of_ref[:, sl] = o_f
        st_scr[2 * hh] = st_f
        o_b, st_b = _hgrn_direction(qb_ref[:, sl], fb_ref[:, sl], vb_ref[:, sl], lb_ref[1:2, sl],
                                    st_scr[2 * hh + 1], True)
        ob_ref[:, sl] = o_b
        st_scr[2 * hh + 1] = st_b


def hgrn_scan(p_a, lb, batch, t_len, c_len):
    na = p_a.shape[0]
    tt = HGRN_TILE
    n_c, n_t = c_len // tt, t_len // tt
    off_c = batch * n_t

    def rb_f(b, s):
        return jnp.where(s < n_c, off_c + b * n_c + s, b * n_t + (s - n_c))

    def rb_b(b, s):
        return jnp.where(s < n_c, off_c + b * n_c + (n_c - 1 - s), b * n_t + (n_t - 1 - (s - n_c)))

    hps = HGRN_HEADS_PER_STEP
    n_hb = A_HEADS // hps

    def spec(rb, group):
        return pl.BlockSpec((tt, hps * A_DK), lambda b, h, s: (rb(b, s), group * n_hb + h))

    out_sds = jax.ShapeDtypeStruct((na, A_WIDTH), F32)
    return pl.pallas_call(
        _hgrn_kernel,
        grid=(batch, n_hb, n_c + n_t),
        in_specs=[spec(rb_f, 0), spec(rb_f, 1), spec(rb_f, 3),
                  spec(rb_b, 0), spec(rb_b, 2), spec(rb_b, 3),
                  pl.BlockSpec((2, hps * A_DK), lambda b, h, s: (0, h))],
        out_specs=[spec(rb_f, 0), spec(rb_b, 0)],
        out_shape=[out_sds, out_sds],
        scratch_shapes=[pltpu.VMEM((2 * hps, A_DK, A_DK), F32)],
        compiler_params=_cparams(("parallel", "parallel", "arbitrary")),
        name="hgrn_scan",
    )(p_a, p_a, p_a, p_a, p_a, p_a, lb)


def _hgrn_finish_kernel(of_ref, ob_ref, g_ref, gn_ref, o_ref):
    o = of_ref[...] + ob_ref[...]
    gate = _silu(g_ref[...])
    for h in range(A_HEADS):
        sl = slice(h * A_DK, (h + 1) * A_DK)
        oh = o[:, sl]
        y = oh * lax.rsqrt(jnp.mean(oh * oh, axis=-1, keepdims=True) + EPS) * gn_ref[...]
        o_ref[:, sl] = (y * gate[:, sl]).astype(o_ref.dtype)


def hgrn_finish(o_f, o_b, p_a, onorm_g):
    na = o_f.shape[0]
    blk = pl.BlockSpec((ROW_TILE, A_WIDTH), lambda i: (i, 0))
    return pl.pallas_call(
        _hgrn_finish_kernel,
        grid=(na // ROW_TILE,),
        in_specs=[blk, blk, pl.BlockSpec((ROW_TILE, A_WIDTH), lambda i: (i, 4)),
                  pl.BlockSpec((1, A_DK), lambda i: (0, 0))],
        out_specs=blk,
        out_shape=jax.ShapeDtypeStruct((na, A_WIDTH), BF16),
        compiler_params=_cparams(("parallel",)),
        name="hgrn_finish",
    )(o_f, o_b, p_a, onorm_g)


def _diff_attn_kernel(q_ref, kc_ref, vtc_ref, k_ref, vt_ref, lam_ref, gn_ref, o_ref,
                      m_scr, l_scr, acc_scr, *, n_k, unroll, lambda_init):
    q_t =(q_ref[...].astype(F32) * (B_HALF ** -0.5)).T.astype(BF16)
    tq = q_t.shape[1]
    sub = lax.broadcasted_iota(jnp.int32, (B_DV, 1), 0)
    zero = jnp.zeros_like(q_t)
    qm = (jnp.where(sub < B_HALF, q_t, zero), jnp.where(sub >= B_HALF, q_t, zero))

    m_scr[...] = jnp.full_like(m_scr, NEG_INF)
    l_scr[...] = jnp.zeros_like(l_scr)
    acc_scr[...] = jnp.zeros_like(acc_scr)

    def fold8(s, op):
        return op(s.reshape(s.shape[0] // 8, 8, tq), axis=0)

    def update(kvs):
        for m in range(2):
            ss = [jnp.dot(kb, qm[m], preferred_element_type=F32) for kb, _ in kvs]
            part = functools.reduce(jnp.maximum, [fold8(s, jnp.max) for s in ss])
            m_old = m_scr[m, 0:1, :]
            m_new = jnp.maximum(m_old, jnp.max(part, axis=0, keepdims=True))
            alpha = jnp.exp(m_old - m_new)
            lsum = alpha * l_scr[m]
            acc = alpha * acc_scr[m]
            for s, (_, vtb) in zip(ss, kvs):
                p = jnp.exp(s - m_new)
                lsum = lsum + fold8(p, jnp.sum)
                acc = acc + jnp.dot(vtb, p.astype(BF16), preferred_element_type=F32)
            l_scr[m] = lsum
            acc_scr[m] = acc
            m_scr[m] = jnp.broadcast_to(m_new, (8, tq))

    update([(kc_ref[...], vtc_ref[...])])
    for j in range(n_k):
        kvs = []
        for u in range(unroll):
            start = (j * unroll + u) * DIFF_TK
            kvs.append((k_ref[start:start + DIFF_TK, :], vt_ref[:, start:start + DIFF_TK]))
        update(kvs)

    lp = lam_ref[...]
    lam = (jnp.exp(jnp.sum(lp[0:1] * lp[1:2], axis=-1, keepdims=True))
           - jnp.exp(jnp.sum(lp[2:3] * lp[3:4], axis=-1, keepdims=True)) + lambda_init)
    l0 = jnp.sum(l_scr[0], axis=0, keepdims=True)
    l1 = jnp.sum(l_scr[1], axis=0, keepdims=True)
    o = acc_scr[0] / l0 - lam * (acc_scr[1] / l1)
    y = o * lax.rsqrt(jnp.mean(o * o, axis=0, keepdims=True) + EPS) * gn_ref[...]
    o_ref[...] = (y * (1.0 - lambda_init)).T.astype(o_ref.dtype)


def diff_attention(p_qk, v_t, lam_p, subln_g, lambda_init, batch, t_len, c_len):
    off_c = batch * (t_len // c_len)
    unroll = min(DIFF_UNROLL, t_len // DIFF_TK)
    assert t_len % (unroll * DIFF_TK) == 0 and t_len % DIFF_TQ == 0

    def call(tq, n_q, q_off, n_k):
        kern = functools.partial(_diff_attn_kernel, n_k=n_k, unroll=unroll, lambda_init=lambda_init)
        in_specs = [
            pl.BlockSpec((tq, B_DV), lambda b, h, i: (q_off + b * n_q + i, h)),
            pl.BlockSpec((c_len, B_DV), lambda b, h, i: (off_c + b, B_HEADS + h)),
            pl.BlockSpec((B_DV, c_len), lambda b, h, i: (h, off_c + b)),
            pl.BlockSpec((t_len, B_DV), lambda b, h, i: (b, B_HEADS + h)),
            pl.BlockSpec((B_DV, t_len), lambda b, h, i: (h, b)),
            pl.BlockSpec((4, B_HALF), lambda b, h, i: (0, 0)),
            pl.BlockSpec((B_DV, 1), lambda b, h, i: (0, 0)),
        ]
        args = [p_qk, p_qk, v_t, p_qk, v_t, lam_p, subln_g.reshape(B_DV, 1)]
        return pl.pallas_call(
            kern,
            grid=(batch, B_HEADS, n_q),
            in_specs=in_specs,
            out_specs=pl.BlockSpec((tq, B_DV), lambda b, h, i: (b * n_q + i, h)),
            out_shape=jax.ShapeDtypeStruct((batch * n_q * tq, B_WIDTH), BF16),
            scratch_shapes=[pltpu.VMEM((2, 8, tq), F32), pltpu.VMEM((2, 8, tq), F32),
                            pltpu.VMEM((2, B_DV, tq), F32)],
            compiler_params=_cparams(("parallel", "parallel", "arbitrary")),
            name="diff_attention",
        )(*args)

    y_lat = call(DIFF_TQ, t_len // DIFF_TQ, 0, t_len // (unroll * DIFF_TK))
    y_ctx = call(c_len, 1, off_c, 0)
    return y_lat, y_ctx


def _swa_kernel(q_ref, kc_ref, vc_ref, k_ref, v_ref, sink_ref, o_ref, *, t_len):
    i = pl.program_id(2)
    tq, gw = q_ref.shape
    win = tq + 2 * C_WINDOW
    a = i * tq
    start = pl.multiple_of(jnp.clip(a - C_WINDOW, 0, t_len - win), C_WINDOW)
    kw = k_ref[pl.ds(start, win), :]
    vw = v_ref[pl.ds(start, win), :]
    kc, vc = kc_ref[...], vc_ref[...]
    qpos = a + lax.broadcasted_iota(jnp.int32, (tq, 1), 0)
    kpos = start + lax.broadcasted_iota(jnp.int32, (1, win), 1)
    valid = jnp.abs(qpos - kpos) <= C_WINDOW
    lane = lax.broadcasted_iota(jnp.int32, (1, gw), 1)
    q = q_ref[...] * (C_HEAD_DIM ** -0.5)
    sink_row = sink_ref[...]
    out = jnp.zeros((tq, gw), F32)
    for j in range(C_GROUP):
        in_head = (lane // C_HEAD_DIM) == j
        qj = jnp.where(in_head, q, jnp.zeros_like(q))
        sink = jnp.sum(jnp.where(lane == j * C_HEAD_DIM, sink_row, 0.0), axis=-1, keepdims=True)
        s_c = _nt_dot(qj, kc)
        s_w = jnp.where(valid, _nt_dot(qj, kw), NEG_INF)
        m = jnp.maximum(jnp.maximum(jnp.max(s_c, axis=-1, keepdims=True),
                                    jnp.max(s_w, axis=-1, keepdims=True)), sink)
        p_c = jnp.exp(s_c - m)
        p_w = jnp.exp(s_w - m)
        den = (jnp.exp(sink - m) + jnp.sum(p_c, axis=-1, keepdims=True)
               + jnp.sum(p_w, axis=-1, keepdims=True))
        o = (jnp.dot(p_c.astype(BF16), vc, preferred_element_type=F32)
             + jnp.dot(p_w.astype(BF16), vw, preferred_element_type=F32)) / den
        out = out + jnp.where(in_head, o, 0.0)
    o_ref[...] = out.astype(o_ref.dtype)


def window_attention(p1, sink_exp, batch, t_len, c_len):
    tq = ATTN_TQ
    gw = C_GROUP * C_HEAD_DIM
    n_q = t_len // tq
    off_c = batch * (t_len // c_len)
    kern = functools.partial(_swa_kernel, t_len=t_len)
    return pl.pallas_call(
        kern,
        grid=(batch, C_KV_HEADS, n_q),
        in_specs=[
            pl.BlockSpec((tq, gw), lambda b, g, i: (b * n_q + i, g)),
            pl.BlockSpec((c_len, gw), lambda b, g, i: (off_c + b, C_KV_HEADS + g)),
            pl.BlockSpec((c_len, gw), lambda b, g, i: (off_c + b, 2 * C_KV_HEADS + g)),
            pl.BlockSpec((t_len, gw), lambda b, g, i: (b, C_KV_HEADS + g)),
            pl.BlockSpec((t_len, gw), lambda b, g, i: (b, 2 * C_KV_HEADS + g)),
            pl.BlockSpec((None, 1, gw), lambda b, g, i: (g, 0, 0)),
        ],
        out_specs=pl.BlockSpec((tq, gw), lambda b, g, i: (b * n_q + i, g)),
        out_shape=jax.ShapeDtypeStruct((batch * t_len, C_HEADS * C_HEAD_DIM), BF16),
        compiler_params=_cparams(("parallel", "parallel", "arbitrary")),
        name="window_attention",
    )(p1, p1, p1, p1, p1, sink_exp)


def _to_row_tiles(ref, val):
    for s in range(val.shape[1] // LANES):
        ref[:, s, :] = val[:, s * LANES:(s + 1) * LANES]


def _from_row_tiles(ref):
    return jnp.concatenate([ref[:, s, :] for s in range(ref.shape[1])], axis=1)


def _out_proj_kernel(ya_ref, yb_ref, wa_ref, wb_ref, x_ref, g1_ref, g_ref, sc_ref, sh_ref, rw_ref,
                     *rest, n_main):
    if len(rest) == 4:
        yt_ref, xo_ref, h_ref, lg_ref = rest
        yb = jnp.where(pl.program_id(0) < n_main, yb_ref[...], yt_ref[...])
    else:
        xo_ref, h_ref, lg_ref = rest
        yb = yb_ref[...]
    y = (jnp.dot(ya_ref[...], wa_ref[...], preferred_element_type=F32)
         + jnp.dot(yb, wb_ref[...], preferred_element_type=F32))
    xn = x_ref[...] + g1_ref[...] * y
    xo_ref[...] = xn
    hn = xn * lax.rsqrt(jnp.mean(xn * xn, axis=-1, keepdims=True) + EPS) * g_ref[...]
    h2 = hn * (1.0 + sc_ref[...]) + sh_ref[...]
    _to_row_tiles(h_ref, h2)
    lg_ref[...] = lax.dot_general(rw_ref[...], h2, (((1,), (1,)), ((), ())),
                                  preferred_element_type=F32, precision=HIGHEST)


def out_proj(ya, yb_arr, yb_col, w_out, x_all, mods, g2n, router_wt, rows, mod_idx, yb_tail=None):
    d = x_all.shape[1]
    half = d // 2
    ne = router_wt.shape[0]
    row = lambda i: (i, 0)
    n_main = yb_arr.shape[0] // ROW_TILE
    in_specs = [
        pl.BlockSpec((ROW_TILE, half), row),
        pl.BlockSpec((ROW_TILE, half), lambda i: (jnp.minimum(i, n_main - 1), yb_col)),
        pl.BlockSpec((half, d), lambda i: (0, 0)),
        pl.BlockSpec((half, d), lambda i: (1, 0)),
        pl.BlockSpec((ROW_TILE, d), row),
        pl.BlockSpec((None, 1, d), lambda i: (mod_idx(i), 0, 2)),
        pl.BlockSpec((1, d), lambda i: (0, 0)),
        pl.BlockSpec((None, 1, d), lambda i: (mod_idx(i), 0, 4)),
        pl.BlockSpec((None, 1, d), lambda i: (mod_idx(i), 0, 3)),
        pl.BlockSpec((ne, d), lambda i: (0, 0)),
    ]
    args = [ya, yb_arr, w_out, w_out, x_all, mods, g2n, mods, mods, router_wt]
    if yb_tail is not None:
        in_specs.append(pl.BlockSpec((ROW_TILE, half), lambda i: (jnp.maximum(i - n_main, 0), 0)))
        args.append(yb_tail)
    return pl.pallas_call(
        functools.partial(_out_proj_kernel, n_main=n_main),
        grid=(rows // ROW_TILE,),
        in_specs=in_specs,
        out_specs=[pl.BlockSpec((ROW_TILE, d), row),
                   pl.BlockSpec((ROW_TILE, d // LANES, LANES), lambda i: (i, 0, 0)),
                   pl.BlockSpec((ne, ROW_TILE), lambda i: (0, i))],
        out_shape=[jax.ShapeDtypeStruct((rows, d), F32),
                   jax.ShapeDtypeStruct((rows, d // LANES, LANES), F32),
                   jax.ShapeDtypeStruct((ne, rows), F32)],
        compiler_params=_cparams(("parallel",)),
        name="out_proj",
    )(*args)


def _router_kernel(lg_ref, rb_ref, ids_ref, w_ref, cnt_ref, carry):
    @pl.when(pl.program_id(0) == 0)
    def _():
        carry[...] = jnp.zeros_like(carry)

    tr = lg_ref.shape[1]
    aff = jax.nn.sigmoid(lg_ref[...])
    sel = aff + rb_ref[...]
    rows = [sel[e:e + 1] for e in range(N_EXPERTS)]

    def beats(a, b, a_first):
        return jnp.logical_or(a > b, jnp.logical_and(a == b, a_first))

    def rank_among(vals, j):
        r = jnp.zeros(vals[0].shape, jnp.int32)
        for i2 in range(len(vals)):
            if i2 != j:
                r = r + beats(vals[i2], vals[j], i2 < j).astype(jnp.int32)
        return r

    rank, gscore = [], []
    for g in range(N_GROUPS):
        grp_rows = rows[g * EXPERTS_PER_GROUP:(g + 1) * EXPERTS_PER_GROUP]
        grp_rank = [rank_among(grp_rows, j) for j in range(EXPERTS_PER_GROUP)]
        rank += grp_rank
        gscore.append(sum(jnp.where(grp_rank[j] < TOP_K, grp_rows[j], 0.0) for j in range(EXPERTS_PER_GROUP)))
    chosen = [rank_among(gscore, g) == 0 for g in range(N_GROUPS)]
    onehot = []
    for k in range(TOP_K):
        hot_rows = [jnp.logical_and(chosen[e // EXPERTS_PER_GROUP], rank[e] == k).astype(F32)
                    for e in range(N_EXPERTS)]
        onehot.append(jnp.concatenate(hot_rows, axis=0))

    e_idx = lax.broadcasted_iota(jnp.int32, (N_EXPERTS, 1), 0).astype(F32)
    picked = [jnp.sum(onehot[k] * aff, axis=0, keepdims=True) for k in range(TOP_K)]
    denom = picked[0] + picked[1]
    before = (lax.broadcasted_iota(jnp.int32, (tr, tr), 0)
              < lax.broadcasted_iota(jnp.int32, (tr, tr), 1)).astype(BF16)
    both = onehot[0] + onehot[1]
    seen = carry[:, 0:1] + jnp.dot(both.astype(BF16), before, preferred_element_type=F32)
    id_rows = [jnp.sum(onehot[k] * e_idx, axis=0, keepdims=True) for k in range(TOP_K)]
    id_rows += [jnp.sum(onehot[k] * seen, axis=0, keepdims=True) for k in range(TOP_K)]
    pad = jnp.zeros((8 - 2 * TOP_K, tr), F32)
    ids_ref[...] = jnp.concatenate(id_rows + [pad], axis=0).astype(jnp.int32)
    w_ref[...] = jnp.concatenate([picked[0] / denom, picked[1] / denom,
                                  jnp.zeros((8 - TOP_K, tr), F32)], axis=0)
    carry[...] = carry[...] + jnp.sum(both, axis=1, keepdims=True)
    cnt_ref[...] = carry[...]


def moe_router(logits_t, router_b):
    ne, rows = logits_t.shape
    tr = ROW_TILE
    return pl.pallas_call(
        _router_kernel,
        grid=(rows // tr,),
        in_specs=[pl.BlockSpec((ne, tr), lambda i: (0, i)), pl.BlockSpec((ne, 1), lambda i: (0, 0))],
        out_specs=[pl.BlockSpec((8, tr), lambda i: (0, i)), pl.BlockSpec((8, tr), lambda i: (0, i)),
                   pl.BlockSpec((ne, LANES), lambda i: (0, 0))],
        out_shape=[jax.ShapeDtypeStruct((8, rows), jnp.int32), jax.ShapeDtypeStruct((8, rows), F32),
                   jax.ShapeDtypeStruct((ne, LANES), F32)],
        scratch_shapes=[pltpu.VMEM((ne, LANES), F32)],
        compiler_params=_cparams(("arbitrary",)),
        name="moe_router",
    )(logits_t, router_b.astype(F32).reshape(ne, 1))


def _tile_copy(src, dst, sem):
    return pltpu.make_async_copy(src, dst, sem)


def _dispatch_kernel(dest_ref, h_ref, zero_hbm, xs_hbm, sem):
    del zero_hbm
    n_tok = h_ref.shape[0]

    def start(t, carry):
        for k in range(TOP_K):
            _tile_copy(h_ref.at[t], xs_hbm.at[dest_ref[0, 0, TOP_K * t + k]], sem).start()
        return carry

    lax.fori_loop(0, n_tok, start, 0, unroll=DMA_UNROLL)
    for k in range(TOP_K):
        _tile_copy(h_ref, xs_hbm.at[pl.ds(0, n_tok)], sem).wait()


def moe_dispatch(h2, dest, n_rows):
    rows, ns, nl = h2.shape
    n_t = rows // MOE_TOK
    dest3 = dest.reshape(n_t, 1, MOE_TOK * TOP_K)
    zeros = jnp.zeros((n_rows, ns, nl), F32)
    return pl.pallas_call(
        _dispatch_kernel,
        grid=(n_t,),
        in_specs=[
            pl.BlockSpec((1, 1, MOE_TOK * TOP_K), lambda i: (i, 0, 0), memory_space=pltpu.SMEM),
            pl.BlockSpec((MOE_TOK, ns, nl), lambda i: (i, 0, 0)),
            pl.BlockSpec(memory_space=pl.ANY),
        ],
        out_specs=pl.BlockSpec(memory_space=pl.ANY),
        out_shape=jax.ShapeDtypeStruct((n_rows, ns, nl), F32),
        scratch_shapes=[pltpu.SemaphoreType.DMA],
        input_output_aliases={2: 0},
        compiler_params=_cparams(("arbitrary",)),
        name="moe_dispatch",
    )(dest3, h2, zeros)


def _expert_kernel(be_ref, nv_ref, x_ref, wg_ref, wu_ref, wd_ref, o_ref, wg_s, wu_s, wd_s):
    i = pl.program_id(0)
    new_expert = jnp.logical_or(i == 0, be_ref[i] != be_ref[jnp.maximum(i - 1, 0)])

    @pl.when(new_expert)
    def _():
        wg_s[...] = wg_ref[...].astype(BF16)
        wu_s[...] = wu_ref[...].astype(BF16)
        wd_s[...] = wd_ref[...].astype(BF16)

    @pl.when(i < nv_ref[0])
    def _():
        x = _from_row_tiles(x_ref).astype(BF16)
        hid = (_silu(jnp.dot(x, wg_s[...], preferred_element_type=F32))
               * jnp.dot(x, wu_s[...], preferred_element_type=F32))
        _to_row_tiles(o_ref, jnp.dot(hid.astype(BF16), wd_s[...], preferred_element_type=F32))

    @pl.when(i >= nv_ref[0])
    def _():
        o_ref[...] = jnp.zeros_like(o_ref)


def moe_experts(x_sorted, block_e, n_valid, layer, w_gate, w_up, w_down):
    n_rows, ns, nl = x_sorted.shape
    d, de = w_gate.shape[2], w_gate.shape[3]
    n_blocks = n_rows // MOE_TILE
    grid_spec = pltpu.PrefetchScalarGridSpec(
        num_scalar_prefetch=2,
        grid=(n_blocks,),
        in_specs=[
            pl.BlockSpec((MOE_TILE, ns, nl), lambda i, be, nv: (i, 0, 0)),
            pl.BlockSpec((None, None, d, de), lambda i, be, nv: (layer, be[i], 0, 0)),
            pl.BlockSpec((None, None, d, de), lambda i, be, nv: (layer, be[i], 0, 0)),
            pl.BlockSpec((None, None, de, d), lambda i, be, nv: (layer, be[i], 0, 0)),
        ],
        out_specs=pl.BlockSpec((MOE_TILE, ns, nl), lambda i, be, nv: (i, 0, 0)),
        scratch_shapes=[pltpu.VMEM((d, de), BF16), pltpu.VMEM((d, de), BF16), pltpu.VMEM((de, d), BF16)],
    )
    return pl.pallas_call(
        _expert_kernel,
        grid_spec=grid_spec,
        out_shape=jax.ShapeDtypeStruct((n_rows, ns, nl), F32),
        compiler_params=_cparams(("arbitrary",)),
        name="moe_experts",
    )(block_e, n_valid, x_sorted, w_gate, w_up, w_down)


def _combine_kernel(dest_ref, ys_hbm, w_ref, x_ref, g2_ref, gf_ref, o_ref, buf, sem, *, final_norm):
    n_tok = buf.shape[1]

    def start(t, carry):
        for k in range(TOP_K):
            _tile_copy(ys_hbm.at[dest_ref[0, 0, TOP_K * t + k]], buf.at[k, t], sem).start()
        return carry

    lax.fori_loop(0, n_tok, start, 0, unroll=DMA_UNROLL)
    for k in range(TOP_K):
        _tile_copy(ys_hbm.at[pl.ds(0, n_tok)], buf.at[k], sem).wait()
    w = w_ref[...]
    f = w[:, 0:1] * _from_row_tiles(buf.at[0]) + w[:, 1:2] * _from_row_tiles(buf.at[1])
    xn = x_ref[...] + g2_ref[...] * f
    if final_norm:
        xn = xn * lax.rsqrt(jnp.mean(xn * xn, axis=-1, keepdims=True) + EPS) * gf_ref[...]
    o_ref[...] = xn


def moe_combine(y_sorted, dest, wts, x_new, mods, final_g, final_norm, mod_idx_tok):
    rows, d = x_new.shape
    n_t = rows // MOE_TOK
    dest3 = dest.reshape(n_t, 1, MOE_TOK * TOP_K)
    kern = functools.partial(_combine_kernel, final_norm=final_norm)
    return pl.pallas_call(
        kern,
        grid=(n_t,),
        in_specs=[
            pl.BlockSpec((1, 1, MOE_TOK * TOP_K), lambda i: (i, 0, 0), memory_space=pltpu.SMEM),
            pl.BlockSpec(memory_space=pl.ANY),
            pl.BlockSpec((MOE_TOK, TOP_K), lambda i: (i, 0)),
            pl.BlockSpec((MOE_TOK, d), lambda i: (i, 0)),
            pl.BlockSpec((None, 1, d), lambda i: (mod_idx_tok(i), 0, 5)),
            pl.BlockSpec((1, d), lambda i: (0, 0)),
        ],
        out_specs=pl.BlockSpec((MOE_TOK, d), lambda i: (i, 0)),
        out_shape=jax.ShapeDtypeStruct((rows, d), F32),
        scratch_shapes=[pltpu.VMEM((TOP_K, MOE_TOK, d // LANES, LANES), F32), pltpu.SemaphoreType.DMA],
        compiler_params=_cparams(("arbitrary",)),
        name="moe_combine",
    )(dest3, y_sorted, wts, x_new, mods, final_g)


def _row_plan(ids, counts, n_tok):
    counts = counts[:, 0].astype(jnp.int32)
    padded = (counts + MOE_TILE - 1) // MOE_TILE * MOE_TILE
    pad_end = jnp.cumsum(padded)
    pad_start = pad_end - padded
    experts = jnp.arange(N_EXPERTS, dtype=jnp.int32)[:, None]
    dest = [jnp.sum(jnp.where(ids[k][None, :] == experts, pad_start[:, None], 0), axis=0) + ids[TOP_K + k]
            for k in range(TOP_K)]
    dest = jnp.stack(dest, axis=1).astype(jnp.int32)
    n_blocks = -(-(n_tok * TOP_K) // MOE_TILE) + N_EXPERTS
    block_start = jnp.arange(n_blocks, dtype=jnp.int32) * MOE_TILE
    block_e = jnp.minimum(jnp.sum(pad_end[None, :] <= block_start[:, None], axis=1), N_EXPERTS - 1)
    n_valid = (pad_end[-1] // MOE_TILE).reshape(1)
    return dest, block_e.astype(jnp.int32), n_valid.astype(jnp.int32), n_blocks


def moe_layer(h2, logits_t, router_b, layer, w_gate, w_up, w_down, x_new, mods, final_g, final_norm,
              mod_idx_tok):
    n_tok = h2.shape[0]
    ids, wts8, counts = moe_router(logits_t, router_b)
    dest, block_e, n_valid, n_blocks = _row_plan(ids, counts, n_tok)
    wts = wts8[:TOP_K].T
    x_sorted = moe_dispatch(h2, dest, n_blocks * MOE_TILE)
    y_sorted = moe_experts(x_sorted, block_e, n_valid, layer, w_gate, w_up, w_down)
    return moe_combine(y_sorted, dest, wts, x_new, mods, final_g, final_norm, mod_idx_tok)


def _rope_tables(batch, t_len, c_len):
    n_rows = t_len // GRID_W
    row = jnp.repeat(jnp.arange(n_rows), GRID_W).astype(F32)
    col = jnp.tile(jnp.arange(GRID_W), n_rows).astype(F32)
    half = B_HALF // 2
    inv = 1.0 / (ROPE_BASE ** (jnp.arange(0, half, 2, dtype=F32) / half))
    ar, ac = row[:, None] * inv, col[:, None] * inv
    ang = jnp.concatenate([ar, ar, ac, ac], axis=-1)
    cos, sin = jnp.cos(ang), jnp.sin(ang)
    first = (jnp.arange(B_HALF) % 32) < 16
    sin_a = jnp.where(first, -sin, 0.0)
    sin_b = jnp.where(first, 0.0, sin)

    def full(tab, ctx_val):
        lat = jnp.tile(jnp.tile(tab, (1, LANES // B_HALF)), (batch, 1))
        ctx = jnp.full((batch * c_len, LANES), ctx_val, F32)
        return jnp.concatenate([lat, ctx], axis=0)

    return full(cos, 1.0), full(sin_a, 0.0), full(sin_b, 0.0)


def kernel(x, c, ctx, c_ctx, ada_w, ada_b, norm_mix_g, norm_ffn_g, even_w_in, even_w_out, hgrn_lb_logits,
           hgrn_onorm_g, diff_lambda, diff_subln_g, odd_w_qkv, odd_w_out, swa_sink, router_w, router_b,
           moe_w_gate, moe_w_up, moe_w_down, final_norm_g):
    batch, t_len, d = x.shape
    c_len = ctx.shape[1]
    n_lat = batch * t_len
    assert t_len % ROW_TILE == 0 and (batch * c_len) % ROW_TILE == 0 and c_len == HGRN_TILE
    assert batch < 8 and t_len % DIFF_TK == 0 and t_len % GRID_W == 0

    x_all = jnp.concatenate([x.reshape(n_lat, d), ctx.reshape(batch * c_len, d)], axis=0)
    rope = _rope_tables(batch, t_len, c_len)

    def mod_idx_for(tile):
        per_batch = t_len // tile
        return lambda i: jnp.minimum(i // per_batch, batch)

    mod_idx = mod_idx_for(ROW_TILE)
    mod_idx_tok = mod_idx_for(MOE_TOK)

    cvec = jnp.zeros((8, d), F32).at[:batch].set(c).at[batch].set(c_ctx)
    mods_all = ada_modulation(cvec, ada_w, ada_b)
    lower_bounds = jnp.cumsum(jax.nn.softmax(hgrn_lb_logits.astype(F32), axis=0), axis=0)
    router_wt = router_w.astype(F32).T

    mods = mods_all[0].reshape(8, 1, 6 * d)
    g_mix = norm_mix_g[0].reshape(1, d)
    w_in = even_w_in[0].astype(BF16)
    n_a = 5 * A_WIDTH
    p_a = norm_mod_matmul(x_all, g_mix, mods, 1, 0, w_in[:, :n_a], rope, (0, 0), F32, mod_idx)
    n_qk = n_a + 2 * B_WIDTH
    p_qk, v_t = norm_mod_matmul(x_all, g_mix, mods, 1, 0, w_in[:, n_a:n_qk], rope, (0, 2 * B_WIDTH), BF16,
                                mod_idx, wt=w_in[:, n_qk:].T)
    o_f, o_b = hgrn_scan(p_a, lower_bounds[0], batch, t_len, c_len)
    ya = hgrn_finish(o_f, o_b, p_a, hgrn_onorm_g[0].reshape(1, A_DK))
    lambda_init = 0.8 - 0.6 * math.exp(-0.3 * 0)
    yb, yb_ctx = diff_attention(p_qk, v_t, diff_lambda[0], diff_subln_g[0], lambda_init, batch, t_len, c_len)
    na = x_all.shape[0]
    x_new, h2, logits = out_proj(ya, yb, 0, even_w_out[0].astype(BF16), x_all, mods,
                                 norm_ffn_g[0].reshape(1, d), router_wt, na, mod_idx, yb_tail=yb_ctx)
    x_all = moe_layer(h2, logits, router_b, 0, moe_w_gate, moe_w_up, moe_w_down, x_new, mods,
                      final_norm_g.reshape(1, d), False, mod_idx_tok)

    mods = mods_all[1].reshape(8, 1, 6 * d)
    w_qkv = odd_w_qkv[0]
    q_cols = C_HEADS * C_HEAD_DIM
    kv_cols = C_KV_HEADS * C_HEAD_DIM

    def rep_heads(wkv):
        wkv = wkv.reshape(d, C_KV_HEADS, 1, C_HEAD_DIM)
        return jnp.broadcast_to(wkv, (d, C_KV_HEADS, C_GROUP, C_HEAD_DIM)).reshape(d, q_cols)

    w1 = jnp.concatenate([w_qkv[:, :q_cols], rep_heads(w_qkv[:, q_cols:q_cols + kv_cols]),
                          rep_heads(w_qkv[:, q_cols + kv_cols:])], axis=1).astype(BF16)
    p1 = norm_mod_matmul(x_all, norm_mix_g[1].reshape(1, d), mods, 1, 0, w1, rope, (0, 2 * q_cols),
                         BF16, mod_idx)
    sink_exp = jnp.repeat(swa_sink[0].astype(F32), C_HEAD_DIM).reshape(C_KV_HEADS, 1, C_GROUP * C_HEAD_DIM)
    o1 = window_attention(p1, sink_exp, batch, t_len, c_len)
    x_new, h2, logits = out_proj(o1, o1, 1, odd_w_out[0].astype(BF16), x_all, mods,
                                 norm_ffn_g[1].reshape(1, d), router_wt, n_lat, mod_idx)
    out = moe_layer(h2, logits, router_b, 1, moe_w_gate, moe_w_up, moe_w_down, x_new, mods,
                    final_norm_g.reshape(1, d), True, mod_idx_tok)
    return out.reshape(batch, t_len, d)
```

```python
import functools
import math

import jax
import jax.numpy as jnp
from jax import lax
from jax.experimental import pallas as pl
from jax.experimental.pallas import tpu as pltpu

F32 = jnp.float32
BF16 = jnp.bfloat16
EPS = 1e-6
ROPE_BASE = 10000.0
GRID_W = 64

A_HEADS, A_DK = 4, 128
A_WIDTH = A_HEADS * A_DK
HGRN_CHUNK = 16
B_HEADS, B_HALF = 4, 64
B_DV = 2 * B_HALF
B_WIDTH = B_HEADS * B_DV
C_HEADS, C_KV_HEADS, C_HEAD_DIM, C_WINDOW = 16, 4, 64, 128
C_GROUP = C_HEADS // C_KV_HEADS
N_EXPERTS, N_GROUPS, TOP_K = 16, 4, 2
EXPERTS_PER_GROUP = N_EXPERTS // N_GROUPS

LANES = 128
ROW_TILE = 512
HGRN_TILE = 256
HGRN_HEADS_PER_STEP = 1
ATTN_TQ = 256
DIFF_TK = 512
DIFF_UNROLL = 4
DIFF_TQ = 512
MOE_TILE = 256
MOE_TOK = 256
DMA_UNROLL = 8
VMEM_LIMIT = 56 * 1024 * 1024
HIGHEST = lax.Precision.HIGHEST
NEG_INF = float("-inf")


def _cparams(sem):
    return pltpu.CompilerParams(dimension_semantics=sem, vmem_limit_bytes=VMEM_LIMIT)


def _nt_dot(a, b):
    return lax.dot_general(a, b, (((1,), (1,)), ((), ())), preferred_element_type=F32)


def _sigmoid(x):
    return 0.5 * jnp.tanh(0.5 * x) + 0.5


def _silu(x):
    return x * _sigmoid(x)


def _ada_kernel(c_ref, w_ref, b_ref, o_ref):
    s = _silu(c_ref[...])
    o_ref[...] = jnp.dot(s, w_ref[...], preferred_element_type=F32, precision=HIGHEST) + b_ref[...]


def ada_modulation(cvec, ada_w, ada_b):
    n_layers, d, n6 = ada_w.shape
    tn = 512
    return pl.pallas_call(
        _ada_kernel,
        grid=(n_layers, n6 // tn),
        in_specs=[
            pl.BlockSpec((8, d), lambda l, j: (0, 0)),
            pl.BlockSpec((None, d, tn), lambda l, j: (l, 0, j)),
            pl.BlockSpec((None, 1, tn), lambda l, j: (l, 0, j)),
        ],
        out_specs=pl.BlockSpec((None, 8, tn), lambda l, j: (l, 0, j)),
        out_shape=jax.ShapeDtypeStruct((n_layers, 8, n6), F32),
        compiler_params=_cparams(("parallel", "parallel")),
        name="ada_modulation",
    )(cvec, ada_w, ada_b.reshape(n_layers, 1, n6))


def _rope_slab(x, cos, sin_a, sin_b):
    return x * cos + pltpu.roll(x, LANES - 16, 1) * sin_a + pltpu.roll(x, 16, 1) * sin_b


def _nmm_kernel(x_ref, g_ref, sc_ref, sh_ref, w_ref, cos_ref, sa_ref, sb_ref, *rest, rope_cols, tn):
    x = x_ref[...]
    y = x * lax.rsqrt(jnp.mean(x * x, axis=-1, keepdims=True) + EPS) * g_ref[...]
    h = (y * (1.0 + sc_ref[...]) + sh_ref[...]).astype(BF16)
    if len(rest) == 3:
        wt_ref, o_ref, ot_ref = rest
        ot_ref[...] = _nt_dot(wt_ref[...], h).astype(ot_ref.dtype)
    else:
        (o_ref,) = rest
    n = w_ref.shape[1]
    for c0 in range(0, n, tn):
        acc = jnp.dot(h, w_ref[:, c0:c0 + tn], preferred_element_type=F32)
        if rope_cols[0] <= c0 < rope_cols[1]:
            cos, sa, sb = cos_ref[...], sa_ref[...], sb_ref[...]
            for c in range(tn // LANES):
                sl = slice(c * LANES, (c + 1) * LANES)
                o_ref[:, c0 + c * LANES:c0 + (c + 1) * LANES] = (
                    _rope_slab(acc[:, sl], cos, sa, sb).astype(o_ref.dtype))
        else:
            o_ref[:, c0:c0 + tn] = acc.astype(o_ref.dtype)


def norm_mod_matmul(x_all, g, mods, sc_chunk, sh_chunk, w, rope, rope_cols, out_dtype, mod_idx, tn=512,
                    wt=None):
    na, d = x_all.shape
    n = w.shape[1]
    cos, sa, sb = rope
    assert n % tn == 0 and rope_cols[0] % tn == 0 and rope_cols[1] % tn == 0
    kern = functools.partial(_nmm_kernel, rope_cols=rope_cols, tn=tn)
    row = lambda i: (i, 0)
    in_specs = [
        pl.BlockSpec((ROW_TILE, d), row),
        pl.BlockSpec((1, d), lambda i: (0, 0)),
        pl.BlockSpec((None, 1, d), lambda i: (mod_idx(i), 0, sc_chunk)),
        pl.BlockSpec((None, 1, d), lambda i: (mod_idx(i), 0, sh_chunk)),
        pl.BlockSpec((d, n), lambda i: (0, 0)),
        pl.BlockSpec((ROW_TILE, LANES), row),
        pl.BlockSpec((ROW_TILE, LANES), row),
        pl.BlockSpec((ROW_TILE, LANES), row),
    ]
    out_specs = [pl.BlockSpec((ROW_TILE, n), row)]
    out_shape = [jax.ShapeDtypeStruct((na, n), out_dtype)]
    args = [x_all, g, mods, mods, w, cos, sa, sb]
    if wt is not None:
        nt = wt.shape[0]
        in_specs.append(pl.BlockSpec((nt, d), lambda i: (0, 0)))
        out_specs.append(pl.BlockSpec((nt, ROW_TILE), lambda i: (0, i)))
        out_shape.append(jax.ShapeDtypeStruct((nt, na), out_dtype))
        args.append(wt)
    outs = pl.pallas_call(
        kern,
        grid=(na // ROW_TILE,),
        in_specs=in_specs,
        out_specs=out_specs,
        out_shape=out_shape,
        compiler_params=_cparams(("parallel",)),
        name="norm_mod_matmul",
    )(*args)
    return outs[0] if wt is None else outs


def _hgrn_direction(q_raw, f_raw, v, lb_row, st, reverse):
    tt = q_raw.shape[0]
    n_chunks = tt // HGRN_CHUNK
    qs = _silu(q_raw) * (A_DK ** -0.5)
    f = lb_row + (1.0 - lb_row) * _sigmoid(f_raw)
    kk = 1.0 - f
    lf = jnp.log(f)

    r = lax.broadcasted_iota(jnp.int32, (tt, tt), 0)
    c = lax.broadcasted_iota(jnp.int32, (tt, tt), 1)
    same = (r // HGRN_CHUNK) == (c // HGRN_CHUNK)
    tri = (c >= r) if reverse else (c <= r)
    m_cum = jnp.where(jnp.logical_and(same, tri), 1.0, 0.0).astype(BF16)
    m_tot = jnp.where(same, 1.0, 0.0).astype(BF16)
    hi = lf.astype(BF16)
    r1 = lf - hi.astype(F32)
    mid = r1.astype(BF16)
    lo = (r1 - mid.astype(F32)).astype(BF16)

    def split_dot(m):
        return (jnp.dot(m, hi, preferred_element_type=F32) + jnp.dot(m, mid, preferred_element_type=F32)
                + jnp.dot(m, lo, preferred_element_type=F32))

    cum = split_dot(m_cum)
    tot = split_dot(m_tot)
    q_in = qs * jnp.exp(cum)
    k_out = kk * jnp.exp(tot - cum)

    pos = lax.broadcasted_iota(jnp.int32, (tt, 1), 0) % HGRN_CHUNK
    o = jnp.zeros((tt, A_DK), F32)
    for dist in range(HGRN_CHUNK):
        if dist == 0:
            ks, cs, vs = kk, cum, v
            diff = jnp.zeros_like(cum)
        else:
            shift = (tt - dist) if reverse else dist
            ks = pltpu.roll(kk, shift, 0)
            cs = pltpu.roll(cum, shift, 0)
            vs = pltpu.roll(v, shift, 0)
            valid = (pos <= HGRN_CHUNK - 1 - dist) if reverse else (pos >= dist)
            diff = jnp.where(valid, cum - cs, NEG_INF)
        score = jnp.sum(qs * ks * jnp.exp(diff), axis=-1, keepdims=True)
        o = o + score * vs

    v_t = v.T.astype(BF16)
    q_in_t = q_in.T.astype(BF16)
    k_out_b = k_out.astype(BF16)
    lane_chunk = lax.broadcasted_iota(jnp.int32, (1, tt), 1) // HGRN_CHUNK
    updates = [jnp.dot(v_t * (lane_chunk == ci).astype(BF16), k_out_b, preferred_element_type=F32)
               for ci in range(n_chunks)]
    o_t = jnp.zeros((A_DK, tt), F32)
    order = range(n_chunks - 1, -1, -1) if reverse else range(n_chunks)
    for ci in order:
        lo_r = ci * HGRN_CHUNK
        from_state = jnp.dot(st.astype(BF16), q_in_t, preferred_element_type=F32)
        o_t = o_t + jnp.where(lane_chunk == ci, from_state, 0.0)
        st = st * jnp.exp(tot[lo_r:lo_r + 1]) + updates[ci]
    return o + o_t.T, st


def _hgrn_kernel(qf_ref, ff_ref, vf_ref, qb_ref, fb_ref, vb_ref, lb_ref, of_ref, ob_ref, st_scr):
    @pl.when(pl.program_id(2) == 0)
    def _():
        st_scr[...] = jnp.zeros_like(st_scr)

    for hh in range(HGRN_HEADS_PER_STEP):
        sl = slice(hh * A_DK, (hh + 1) * A_DK)
        o_f, st_f = _hgrn_direction(qf_ref[:, sl], ff_ref[:, sl], vf_ref[:, sl], lb_ref[0:1, sl],
                                    st_scr[2 * hh], False)
        of_ref[:, sl] = o_f
        st_scr[2 * hh] = st_f
        o_b, st_b = _hgrn_direction(qb_ref[:, sl], fb_ref[:, sl], vb_ref[:, sl], lb_ref[1:2, sl],
                                    st_scr[2 * hh + 1], True)
        ob_ref[:, sl] = o_b
        st_scr[2 * hh + 1] = st_b


def hgrn_scan(p_a, lb, batch, t_len, c_len):
    na = p_a.shape[0]
    tt = HGRN_TILE
    n_c, n_t = c_len // tt, t_len // tt
    off_c = batch * n_t

    def rb_f(b, s):
        return jnp.where(s < n_c, off_c + b * n_c + s, b * n_t + (s - n_c))

    def rb_b(b, s):
        return jnp.where(s < n_c, off_c + b * n_c + (n_c - 1 - s), b * n_t + (n_t - 1 - (s - n_c)))

    hps = HGRN_HEADS_PER_STEP
    n_hb = A_HEADS // hps

    def spec(rb, group):
        return pl.BlockSpec((tt, hps * A_DK), lambda b, h, s: (rb(b, s), group * n_hb + h))

    out_sds = jax.ShapeDtypeStruct((na, A_WIDTH), F32)
    return pl.pallas_call(
        _hgrn_kernel,
        grid=(batch, n_hb, n_c + n_t),
        in_specs=[spec(rb_f, 0), spec(rb_f, 1), spec(rb_f, 3),
                  spec(rb_b, 0), spec(rb_b, 2), spec(rb_b, 3),
                  pl.BlockSpec((2, hps * A_DK), lambda b, h, s: (0, h))],
        out_specs=[spec(rb_f, 0), spec(rb_b, 0)],
        out_shape=[out_sds, out_sds],
        scratch_shapes=[pltpu.VMEM((2 * hps, A_DK, A_DK), F32)],
        compiler_params=_cparams(("parallel", "parallel", "arbitrary")),
        name="hgrn_scan",
    )(p_a, p_a, p_a, p_a, p_a, p_a, lb)


def _hgrn_finish_kernel(of_ref, ob_ref, g_ref, gn_ref, o_ref):
    o = of_ref[...] + ob_ref[...]
    gate = _silu(g_ref[...])
    for h in range(A_HEADS):
        sl = slice(h * A_DK, (h + 1) * A_DK)
        oh = o[:, sl]
        y = oh * lax.rsqrt(jnp.mean(oh * oh, axis=-1, keepdims=True) + EPS) * gn_ref[...]
        o_ref[:, sl] = (y * gate[:, sl]).astype(o_ref.dtype)


def hgrn_finish(o_f, o_b, p_a, onorm_g):
    na = o_f.shape[0]
    blk = pl.BlockSpec((ROW_TILE, A_WIDTH), lambda i: (i, 0))
    return pl.pallas_call(
        _hgrn_finish_kernel,
        grid=(na // ROW_TILE,),
        in_specs=[blk, blk, pl.BlockSpec((ROW_TILE, A_WIDTH), lambda i: (i, 4)),
                  pl.BlockSpec((1, A_DK), lambda i: (0, 0))],
        out_specs=blk,
        out_shape=jax.ShapeDtypeStruct((na, A_WIDTH), BF16),
        compiler_params=_cparams(("parallel",)),
        name="hgrn_finish",
    )(o_f, o_b, p_a, onorm_g)


def _diff_attn_kernel(q_ref, kc_ref, vtc_ref, k_ref, vt_ref, lam_ref, gn_ref, o_ref,
                      m_scr, l_scr, acc_scr, *, n_k, unroll, lambda_init):
    q_t =(q_ref[...].astype(F32) * (B_HALF ** -0.5)).T.astype(BF16)
    tq = q_t.shape[1]
    sub = lax.broadcasted_iota(jnp.int32, (B_DV, 1), 0)
    zero = jnp.zeros_like(q_t)
    qm = (jnp.where(sub < B_HALF, q_t, zero), jnp.where(sub >= B_HALF, q_t, zero))

    m_scr[...] = jnp.full_like(m_scr, NEG_INF)
    l_scr[...] = jnp.zeros_like(l_scr)
    acc_scr[...] = jnp.zeros_like(acc_scr)

    def fold8(s, op):
        return op(s.reshape(s.shape[0] // 8, 8, tq), axis=0)

    def update(kvs):
        for m in range(2):
            ss = [jnp.dot(kb, qm[m], preferred_element_type=F32) for kb, _ in kvs]
            part = functools.reduce(jnp.maximum, [fold8(s, jnp.max) for s in ss])
            m_old = m_scr[m, 0:1, :]
            m_new = jnp.maximum(m_old, jnp.max(part, axis=0, keepdims=True))
            alpha = jnp.exp(m_old - m_new)
            lsum = alpha * l_scr[m]
            acc = alpha * acc_scr[m]
            for s, (_, vtb) in zip(ss, kvs):
                p = jnp.exp(s - m_new)
                lsum = lsum + fold8(p, jnp.sum)
                acc = acc + jnp.dot(vtb, p.astype(BF16), preferred_element_type=F32)
            l_scr[m] = lsum
            acc_scr[m] = acc
            m_scr[m] = jnp.broadcast_to(m_new, (8, tq))

    update([(kc_ref[...], vtc_ref[...])])
    for j in range(n_k):
        kvs = []
        for u in range(unroll):
            start = (j * unroll + u) * DIFF_TK
            kvs.append((k_ref[start:start + DIFF_TK, :], vt_ref[:, start:start + DIFF_TK]))
        update(kvs)

    lp = lam_ref[...]
    lam = (jnp.exp(jnp.sum(lp[0:1] * lp[1:2], axis=-1, keepdims=True))
           - jnp.exp(jnp.sum(lp[2:3] * lp[3:4], axis=-1, keepdims=True)) + lambda_init)
    l0 = jnp.sum(l_scr[0], axis=0, keepdims=True)
    l1 = jnp.sum(l_scr[1], axis=0, keepdims=True)
    o = acc_scr[0] / l0 - lam * (acc_scr[1] / l1)
    y = o * lax.rsqrt(jnp.mean(o * o, axis=0, keepdims=True) + EPS) * gn_ref[...]
    o_ref[...] = (y * (1.0 - lambda_init)).T.astype(o_ref.dtype)


def diff_attention(p_qk, v_t, lam_p, subln_g, lambda_init, batch, t_len, c_len):
    off_c = batch * (t_len // c_len)
    unroll = min(DIFF_UNROLL, t_len // DIFF_TK)
    assert t_len % (unroll * DIFF_TK) == 0 and t_len % DIFF_TQ == 0

    def call(tq, n_q, q_off, n_k):
        kern = functools.partial(_diff_attn_kernel, n_k=n_k, unroll=unroll, lambda_init=lambda_init)
        in_specs = [
            pl.BlockSpec((tq, B_DV), lambda b, h, i: (q_off + b * n_q + i, h)),
            pl.BlockSpec((c_len, B_DV), lambda b, h, i: (off_c + b, B_HEADS + h)),
            pl.BlockSpec((B_DV, c_len), lambda b, h, i: (h, off_c + b)),
            pl.BlockSpec((t_len, B_DV), lambda b, h, i: (b, B_HEADS + h)),
            pl.BlockSpec((B_DV, t_len), lambda b, h, i: (h, b)),
            pl.BlockSpec((4, B_HALF), lambda b, h, i: (0, 0)),
            pl.BlockSpec((B_DV, 1), lambda b, h, i: (0, 0)),
        ]
        args = [p_qk, p_qk, v_t, p_qk, v_t, lam_p, subln_g.reshape(B_DV, 1)]
        return pl.pallas_call(
            kern,
            grid=(batch, B_HEADS, n_q),
            in_specs=in_specs,
            out_specs=pl.BlockSpec((tq, B_DV), lambda b, h, i: (b * n_q + i, h)),
            out_shape=jax.ShapeDtypeStruct((batch * n_q * tq, B_WIDTH), BF16),
            scratch_shapes=[pltpu.VMEM((2, 8, tq), F32), pltpu.VMEM((2, 8, tq), F32),
                            pltpu.VMEM((2, B_DV, tq), F32)],
            compiler_params=_cparams(("parallel", "parallel", "arbitrary")),
            name="diff_attention",
        )(*args)

    y_lat = call(DIFF_TQ, t_len // DIFF_TQ, 0, t_len // (unroll * DIFF_TK))
    y_ctx = call(c_len, 1, off_c, 0)
    return y_lat, y_ctx


def _swa_kernel(q_ref, kc_ref, vc_ref, k_ref, v_ref, sink_ref, o_ref, *, t_len):
    i = pl.program_id(2)
    tq, gw = q_ref.shape
    win = tq + 2 * C_WINDOW
    a = i * tq
    start = pl.multiple_of(jnp.clip(a - C_WINDOW, 0, t_len - win), C_WINDOW)
    kw = k_ref[pl.ds(start, win), :]
    vw = v_ref[pl.ds(start, win), :]
    kc, vc = kc_ref[...], vc_ref[...]
    qpos = a + lax.broadcasted_iota(jnp.int32, (tq, 1), 0)
    kpos = start + lax.broadcasted_iota(jnp.int32, (1, win), 1)
    valid = jnp.abs(qpos - kpos) <= C_WINDOW
    lane = lax.broadcasted_iota(jnp.int32, (1, gw), 1)
    q = q_ref[...] * (C_HEAD_DIM ** -0.5)
    sink_row = sink_ref[...]
    out = jnp.zeros((tq, gw), F32)
    for j in range(C_GROUP):
        in_head = (lane // C_HEAD_DIM) == j
        qj = jnp.where(in_head, q, jnp.zeros_like(q))
        sink = jnp.sum(jnp.where(lane == j * C_HEAD_DIM, sink_row, 0.0), axis=-1, keepdims=True)
        s_c = _nt_dot(qj, kc)
        s_w = jnp.where(valid, _nt_dot(qj, kw), NEG_INF)
        m = jnp.maximum(jnp.maximum(jnp.max(s_c, axis=-1, keepdims=True),
                                    jnp.max(s_w, axis=-1, keepdims=True)), sink)
        p_c = jnp.exp(s_c - m)
        p_w = jnp.exp(s_w - m)
        den = (jnp.exp(sink - m) + jnp.sum(p_c, axis=-1, keepdims=True)
               + jnp.sum(p_w, axis=-1, keepdims=True))
        o = (jnp.dot(p_c.astype(BF16), vc, preferred_element_type=F32)
             + jnp.dot(p_w.astype(BF16), vw, preferred_element_type=F32)) / den
        out = out + jnp.where(in_head, o, 0.0)
    o_ref[...] = out.astype(o_ref.dtype)


def window_attention(p1, sink_exp, batch, t_len, c_len):
    tq = ATTN_TQ
    gw = C_GROUP * C_HEAD_DIM
    n_q = t_len // tq
    off_c = batch * (t_len // c_len)
    kern = functools.partial(_swa_kernel, t_len=t_len)
    return pl.pallas_call(
        kern,
        grid=(batch, C_KV_HEADS, n_q),
        in_specs=[
            pl.BlockSpec((tq, gw), lambda b, g, i: (b * n_q + i, g)),
            pl.BlockSpec((c_len, gw), lambda b, g, i: (off_c + b, C_KV_HEADS + g)),
            pl.BlockSpec((c_len, gw), lambda b, g, i: (off_c + b, 2 * C_KV_HEADS + g)),
            pl.BlockSpec((t_len, gw), lambda b, g, i: (b, C_KV_HEADS + g)),
            pl.BlockSpec((t_len, gw), lambda b, g, i: (b, 2 * C_KV_HEADS + g)),
            pl.BlockSpec((None, 1, gw), lambda b, g, i: (g, 0, 0)),
        ],
        out_specs=pl.BlockSpec((tq, gw), lambda b, g, i: (b * n_q + i, g)),
        out_shape=jax.ShapeDtypeStruct((batch * t_len, C_HEADS * C_HEAD_DIM), BF16),
        compiler_params=_cparams(("parallel", "parallel", "arbitrary")),
        name="window_attention",
    )(p1, p1, p1, p1, p1, sink_exp)


def _to_row_tiles(ref, val):
    for s in range(val.shape[1] // LANES):
        ref[:, s, :] = val[:, s * LANES:(s + 1) * LANES]


def _from_row_tiles(ref):
    return jnp.concatenate([ref[:, s, :] for s in range(ref.shape[1])], axis=1)


def _out_proj_kernel(ya_ref, yb_ref, wa_ref, wb_ref, x_ref, g1_ref, g_ref, sc_ref, sh_ref, rw_ref,
                     *rest, n_main):
    if len(rest) == 4:
        yt_ref, xo_ref, h_ref, lg_ref = rest
        yb = jnp.where(pl.program_id(0) < n_main, yb_ref[...], yt_ref[...])
    else:
        xo_ref, h_ref, lg_ref = rest
        yb = yb_ref[...]
    y = (jnp.dot(ya_ref[...], wa_ref[...], preferred_element_type=F32)
         + jnp.dot(yb, wb_ref[...], preferred_element_type=F32))
    xn = x_ref[...] + g1_ref[...] * y
    xo_ref[...] = xn
    hn = xn * lax.rsqrt(jnp.mean(xn * xn, axis=-1, keepdims=True) + EPS) * g_ref[...]
    h2 = hn * (1.0 + sc_ref[...]) + sh_ref[...]
    _to_row_tiles(h_ref, h2)
    lg_ref[...] = lax.dot_general(rw_ref[...], h2, (((1,), (1,)), ((), ())),
                                  preferred_element_type=F32, precision=HIGHEST)


def out_proj(ya, yb_arr, yb_col, w_out, x_all, mods, g2n, router_wt, rows, mod_idx, yb_tail=None):
    d = x_all.shape[1]
    half = d // 2
    ne = router_wt.shape[0]
    row = lambda i: (i, 0)
    n_main = yb_arr.shape[0] // ROW_TILE
    in_specs = [
        pl.BlockSpec((ROW_TILE, half), row),
        pl.BlockSpec((ROW_TILE, half), lambda i: (jnp.minimum(i, n_main - 1), yb_col)),
        pl.BlockSpec((half, d), lambda i: (0, 0)),
        pl.BlockSpec((half, d), lambda i: (1, 0)),
        pl.BlockSpec((ROW_TILE, d), row),
        pl.BlockSpec((None, 1, d), lambda i: (mod_idx(i), 0, 2)),
        pl.BlockSpec((1, d), lambda i: (0, 0)),
        pl.BlockSpec((None, 1, d), lambda i: (mod_idx(i), 0, 4)),
        pl.BlockSpec((None, 1, d), lambda i: (mod_idx(i), 0, 3)),
        pl.BlockSpec((ne, d), lambda i: (0, 0)),
    ]
    args = [ya, yb_arr, w_out, w_out, x_all, mods, g2n, mods, mods, router_wt]
    if yb_tail is not None:
        in_specs.append(pl.BlockSpec((ROW_TILE, half), lambda i: (jnp.maximum(i - n_main, 0), 0)))
        args.append(yb_tail)
    return pl.pallas_call(
        functools.partial(_out_proj_kernel, n_main=n_main),
        grid=(rows // ROW_TILE,),
        in_specs=in_specs,
        out_specs=[pl.BlockSpec((ROW_TILE, d), row),
                   pl.BlockSpec((ROW_TILE, d // LANES, LANES), lambda i: (i, 0, 0)),
                   pl.BlockSpec((ne, ROW_TILE), lambda i: (0, i))],
        out_shape=[jax.ShapeDtypeStruct((rows, d), F32),
                   jax.ShapeDtypeStruct((rows, d // LANES, LANES), F32),
                   jax.ShapeDtypeStruct((ne, rows), F32)],
        compiler_params=_cparams(("parallel",)),
        name="out_proj",
    )(*args)


def _router_kernel(lg_ref, rb_ref, ids_ref, w_ref, cnt_ref, carry):
    @pl.when(pl.program_id(0) == 0)
    def _():
        carry[...] = jnp.zeros_like(carry)

    tr = lg_ref.shape[1]
    aff = jax.nn.sigmoid(lg_ref[...])
    sel = aff + rb_ref[...]
    rows = [sel[e:e + 1] for e in range(N_EXPERTS)]

    def beats(a, b, a_first):
        return jnp.logical_or(a > b, jnp.logical_and(a == b, a_first))

    def rank_among(vals, j):
        r = jnp.zeros(vals[0].shape, jnp.int32)
        for i2 in range(len(vals)):
            if i2 != j:
                r = r + beats(vals[i2], vals[j], i2 < j).astype(jnp.int32)
        return r

    rank, gscore = [], []
    for g in range(N_GROUPS):
        grp_rows = rows[g * EXPERTS_PER_GROUP:(g + 1) * EXPERTS_PER_GROUP]
        grp_rank = [rank_among(grp_rows, j) for j in range(EXPERTS_PER_GROUP)]
        rank += grp_rank
        gscore.append(sum(jnp.where(grp_rank[j] < TOP_K, grp_rows[j], 0.0) for j in range(EXPERTS_PER_GROUP)))
    chosen = [rank_among(gscore, g) == 0 for g in range(N_GROUPS)]
    onehot = []
    for k in range(TOP_K):
        hot_rows = [jnp.logical_and(chosen[e // EXPERTS_PER_GROUP], rank[e] == k).astype(F32)
                    for e in range(N_EXPERTS)]
        onehot.append(jnp.concatenate(hot_rows, axis=0))

    e_idx = lax.broadcasted_iota(jnp.int32, (N_EXPERTS, 1), 0).astype(F32)
    picked = [jnp.sum(onehot[k] * aff, axis=0, keepdims=True) for k in range(TOP_K)]
    denom = picked[0] + picked[1]
    before = (lax.broadcasted_iota(jnp.int32, (tr, tr), 0)
              < lax.broadcasted_iota(jnp.int32, (tr, tr), 1)).astype(BF16)
    both = onehot[0] + onehot[1]
    seen = carry[:, 0:1] + jnp.dot(both.astype(BF16), before, preferred_element_type=F32)
    id_rows = [jnp.sum(onehot[k] * e_idx, axis=0, keepdims=True) for k in range(TOP_K)]
    id_rows += [jnp.sum(onehot[k] * seen, axis=0, keepdims=True) for k in range(TOP_K)]
    pad = jnp.zeros((8 - 2 * TOP_K, tr), F32)
    ids_ref[...] = jnp.concatenate(id_rows + [pad], axis=0).astype(jnp.int32)
    w_ref[...] = jnp.concatenate([picked[0] / denom, picked[1] / denom,
                                  jnp.zeros((8 - TOP_K, tr), F32)], axis=0)
    carry[...] = carry[...] + jnp.sum(both, axis=1, keepdims=True)
    cnt_ref[...] = carry[...]


def moe_router(logits_t, router_b):
    ne, rows = logits_t.shape
    tr = ROW_TILE
    return pl.pallas_call(
        _router_kernel,
        grid=(rows // tr,),
        in_specs=[pl.BlockSpec((ne, tr), lambda i: (0, i)), pl.BlockSpec((ne, 1), lambda i: (0, 0))],
        out_specs=[pl.BlockSpec((8, tr), lambda i: (0, i)), pl.BlockSpec((8, tr), lambda i: (0, i)),
                   pl.BlockSpec((ne, LANES), lambda i: (0, 0))],
        out_shape=[jax.ShapeDtypeStruct((8, rows), jnp.int32), jax.ShapeDtypeStruct((8, rows), F32),
                   jax.ShapeDtypeStruct((ne, LANES), F32)],
        scratch_shapes=[pltpu.VMEM((ne, LANES), F32)],
        compiler_params=_cparams(("arbitrary",)),
        name="moe_router",
    )(logits_t, router_b.astype(F32).reshape(ne, 1))


def _tile_copy(src, dst, sem):
    return pltpu.make_async_copy(src, dst, sem)


def _dispatch_kernel(dest_ref, h_ref, zero_hbm, xs_hbm, sem):
    del zero_hbm
    n_tok = h_ref.shape[0]

    def start(t, carry):
        for k in range(TOP_K):
            _tile_copy(h_ref.at[t], xs_hbm.at[dest_ref[0, 0, TOP_K * t + k]], sem).start()
        return carry

    lax.fori_loop(0, n_tok, start, 0, unroll=DMA_UNROLL)
    for k in range(TOP_K):
        _tile_copy(h_ref, xs_hbm.at[pl.ds(0, n_tok)], sem).wait()


def moe_dispatch(h2, dest, n_rows):
    rows, ns, nl = h2.shape
    n_t = rows // MOE_TOK
    dest3 = dest.reshape(n_t, 1, MOE_TOK * TOP_K)
    zeros = jnp.zeros((n_rows, ns, nl), F32)
    return pl.pallas_call(
        _dispatch_kernel,
        grid=(n_t,),
        in_specs=[
            pl.BlockSpec((1, 1, MOE_TOK * TOP_K), lambda i: (i, 0, 0), memory_space=pltpu.SMEM),
            pl.BlockSpec((MOE_TOK, ns, nl), lambda i: (i, 0, 0)),
            pl.BlockSpec(memory_space=pl.ANY),
        ],
        out_specs=pl.BlockSpec(memory_space=pl.ANY),
        out_shape=jax.ShapeDtypeStruct((n_rows, ns, nl), F32),
        scratch_shapes=[pltpu.SemaphoreType.DMA],
        input_output_aliases={2: 0},
        compiler_params=_cparams(("arbitrary",)),
        name="moe_dispatch",
    )(dest3, h2, zeros)


def _expert_kernel(be_ref, nv_ref, x_ref, wg_ref, wu_ref, wd_ref, o_ref, wg_s, wu_s, wd_s):
    i = pl.program_id(0)
    new_expert = jnp.logical_or(i == 0, be_ref[i] != be_ref[jnp.maximum(i - 1, 0)])

    @pl.when(new_expert)
    def _():
        wg_s[...] = wg_ref[...].astype(BF16)
        wu_s[...] = wu_ref[...].astype(BF16)
        wd_s[...] = wd_ref[...].astype(BF16)

    @pl.when(i < nv_ref[0])
    def _():
        x = _from_row_tiles(x_ref).astype(BF16)
        hid = (_silu(jnp.dot(x, wg_s[...], preferred_element_type=F32))
               * jnp.dot(x, wu_s[...], preferred_element_type=F32))
        _to_row_tiles(o_ref, jnp.dot(hid.astype(BF16), wd_s[...], preferred_element_type=F32))

    @pl.when(i >= nv_ref[0])
    def _():
        o_ref[...] = jnp.zeros_like(o_ref)


def moe_experts(x_sorted, block_e, n_valid, layer, w_gate, w_up, w_down):
    n_rows, ns, nl = x_sorted.shape
    d, de = w_gate.shape[2], w_gate.shape[3]
    n_blocks = n_rows // MOE_TILE
    grid_spec = pltpu.PrefetchScalarGridSpec(
        num_scalar_prefetch=2,
        grid=(n_blocks,),
        in_specs=[
            pl.BlockSpec((MOE_TILE, ns, nl), lambda i, be, nv: (i, 0, 0)),
            pl.BlockSpec((None, None, d, de), lambda i, be, nv: (layer, be[i], 0, 0)),
            pl.BlockSpec((None, None, d, de), lambda i, be, nv: (layer, be[i], 0, 0)),
            pl.BlockSpec((None, None, de, d), lambda i, be, nv: (layer, be[i], 0, 0)),
        ],
        out_specs=pl.BlockSpec((MOE_TILE, ns, nl), lambda i, be, nv: (i, 0, 0)),
        scratch_shapes=[pltpu.VMEM((d, de), BF16), pltpu.VMEM((d, de), BF16), pltpu.VMEM((de, d), BF16)],
    )
    return pl.pallas_call(
        _expert_kernel,
        grid_spec=grid_spec,
        out_shape=jax.ShapeDtypeStruct((n_rows, ns, nl), F32),
        compiler_params=_cparams(("arbitrary",)),
        name="moe_experts",
    )(block_e, n_valid, x_sorted, w_gate, w_up, w_down)


def _combine_kernel(dest_ref, dnext_ref, ys_hbm, w_ref, x_ref, g2_ref, gf_ref, o_ref, buf, sem, *, final_norm):
    i = pl.program_id(0)
    n_tok = buf.shape[2]

    def gather(d_ref, slot):
        def start(t, carry):
            for k in range(TOP_K):
                _tile_copy(ys_hbm.at[d_ref[0, 0, TOP_K * t + k]], buf.at[slot, k, t], sem.at[slot]).start()
            return carry

        lax.fori_loop(0, n_tok, start, 0, unroll=DMA_UNROLL)

    slot = i % 2

    @pl.when(i == 0)
    def _():
        gather(dest_ref, slot)

    @pl.when(i + 1 < pl.num_programs(0))
    def _():
        gather(dnext_ref, 1 - slot)

    for k in range(TOP_K):
        _tile_copy(ys_hbm.at[pl.ds(0, n_tok)], buf.at[slot, k], sem.at[slot]).wait()
    w = w_ref[...]
    f = w[:, 0:1] * _from_row_tiles(buf.at[slot, 0]) + w[:, 1:2] * _from_row_tiles(buf.at[slot, 1])
    xn = x_ref[...] + g2_ref[...] * f
    if final_norm:
        xn = xn * lax.rsqrt(jnp.mean(xn * xn, axis=-1, keepdims=True) + EPS) * gf_ref[...]
    o_ref[...] = xn


def moe_combine(y_sorted, dest, wts, x_new, mods, final_g, final_norm, mod_idx_tok):
    rows, d = x_new.shape
    n_t = rows // MOE_TOK
    dest3 = dest.reshape(n_t, 1, MOE_TOK * TOP_K)
    kern = functools.partial(_combine_kernel, final_norm=final_norm)
    return pl.pallas_call(
        kern,
        grid=(n_t,),
        in_specs=[
            pl.BlockSpec((1, 1, MOE_TOK * TOP_K), lambda i: (i, 0, 0), memory_space=pltpu.SMEM),
            pl.BlockSpec((1, 1, MOE_TOK * TOP_K), lambda i: (jnp.minimum(i + 1, n_t - 1), 0, 0),
                         memory_space=pltpu.SMEM),
            pl.BlockSpec(memory_space=pl.ANY),
            pl.BlockSpec((MOE_TOK, TOP_K), lambda i: (i, 0)),
            pl.BlockSpec((MOE_TOK, d), lambda i: (i, 0)),
            pl.BlockSpec((None, 1, d), lambda i: (mod_idx_tok(i), 0, 5)),
            pl.BlockSpec((1, d), lambda i: (0, 0)),
        ],
        out_specs=pl.BlockSpec((MOE_TOK, d), lambda i: (i, 0)),
        out_shape=jax.ShapeDtypeStruct((rows, d), F32),
        scratch_shapes=[pltpu.VMEM((2, TOP_K, MOE_TOK, d // LANES, LANES), F32),
                        pltpu.SemaphoreType.DMA((2,))],
        compiler_params=_cparams(("arbitrary",)),
        name="moe_combine",
    )(dest3, dest3, y_sorted, wts, x_new, mods, final_g)


def _row_plan(ids, counts, n_tok):
    counts = counts[:, 0].astype(jnp.int32)
    padded = (counts + MOE_TILE - 1) // MOE_TILE * MOE_TILE
    pad_end = jnp.cumsum(padded)
    pad_start = pad_end - padded
    experts = jnp.arange(N_EXPERTS, dtype=jnp.int32)[:, None]
    dest = [jnp.sum(jnp.where(ids[k][None, :] == experts, pad_start[:, None], 0), axis=0) + ids[TOP_K + k]
            for k in range(TOP_K)]
    dest = jnp.stack(dest, axis=1).astype(jnp.int32)
    n_blocks = -(-(n_tok * TOP_K) // MOE_TILE) + N_EXPERTS
    block_start = jnp.arange(n_blocks, dtype=jnp.int32) * MOE_TILE
    block_e = jnp.minimum(jnp.sum(pad_end[None, :] <= block_start[:, None], axis=1), N_EXPERTS - 1)
    n_valid = (pad_end[-1] // MOE_TILE).reshape(1)
    return dest, block_e.astype(jnp.int32), n_valid.astype(jnp.int32), n_blocks


def moe_layer(h2, logits_t, router_b, layer, w_gate, w_up, w_down, x_new, mods, final_g, final_norm,
              mod_idx_tok):
    n_tok = h2.shape[0]
    ids, wts8, counts = moe_router(logits_t, router_b)
    dest, block_e, n_valid, n_blocks = _row_plan(ids, counts, n_tok)
    wts = wts8[:TOP_K].T
    x_sorted = moe_dispatch(h2, dest, n_blocks * MOE_TILE)
    y_sorted = moe_experts(x_sorted, block_e, n_valid, layer, w_gate, w_up, w_down)
    return moe_combine(y_sorted, dest, wts, x_new, mods, final_g, final_norm, mod_idx_tok)


def _rope_tables(batch, t_len, c_len):
    n_rows = t_len // GRID_W
    row = jnp.repeat(jnp.arange(n_rows), GRID_W).astype(F32)
    col = jnp.tile(jnp.arange(GRID_W), n_rows).astype(F32)
    half = B_HALF // 2
    inv = 1.0 / (ROPE_BASE ** (jnp.arange(0, half, 2, dtype=F32) / half))
    ar, ac = row[:, None] * inv, col[:, None] * inv
    ang = jnp.concatenate([ar, ar, ac, ac], axis=-1)
    cos, sin = jnp.cos(ang), jnp.sin(ang)
    first = (jnp.arange(B_HALF) % 32) < 16
    sin_a = jnp.where(first, -sin, 0.0)
    sin_b = jnp.where(first, 0.0, sin)

    def full(tab, ctx_val):
        lat = jnp.tile(jnp.tile(tab, (1, LANES // B_HALF)), (batch, 1))
        ctx = jnp.full((batch * c_len, LANES), ctx_val, F32)
        return jnp.concatenate([lat, ctx], axis=0)

    return full(cos, 1.0), full(sin_a, 0.0), full(sin_b, 0.0)


def kernel(x, c, ctx, c_ctx, ada_w, ada_b, norm_mix_g, norm_ffn_g, even_w_in, even_w_out, hgrn_lb_logits,
           hgrn_onorm_g, diff_lambda, diff_subln_g, odd_w_qkv, odd_w_out, swa_sink, router_w, router_b,
           moe_w_gate, moe_w_up, moe_w_down, final_norm_g):
    batch, t_len, d = x.shape
    c_len = ctx.shape[1]
    n_lat = batch * t_len
    assert t_len % ROW_TILE == 0 and (batch * c_len) % ROW_TILE == 0 and c_len == HGRN_TILE
    assert batch < 8 and t_len % DIFF_TK == 0 and t_len % GRID_W == 0

    x_all = jnp.concatenate([x.reshape(n_lat, d), ctx.reshape(batch * c_len, d)], axis=0)
    rope = _rope_tables(batch, t_len, c_len)

    def mod_idx_for(tile):
        per_batch = t_len // tile
        return lambda i: jnp.minimum(i // per_batch, batch)

    mod_idx = mod_idx_for(ROW_TILE)
    mod_idx_tok = mod_idx_for(MOE_TOK)

    cvec = jnp.zeros((8, d), F32).at[:batch].set(c).at[batch].set(c_ctx)
    mods_all = ada_modulation(cvec, ada_w, ada_b)
    lower_bounds = jnp.cumsum(jax.nn.softmax(hgrn_lb_logits.astype(F32), axis=0), axis=0)
    router_wt = router_w.astype(F32).T

    mods = mods_all[0].reshape(8, 1, 6 * d)
    g_mix = norm_mix_g[0].reshape(1, d)
    w_in = even_w_in[0].astype(BF16)
    n_a = 5 * A_WIDTH
    p_a = norm_mod_matmul(x_all, g_mix, mods, 1, 0, w_in[:, :n_a], rope, (0, 0), F32, mod_idx)
    n_qk = n_a + 2 * B_WIDTH
    p_qk, v_t = norm_mod_matmul(x_all, g_mix, mods, 1, 0, w_in[:, n_a:n_qk], rope, (0, 2 * B_WIDTH), BF16,
                                mod_idx, wt=w_in[:, n_qk:].T)
    o_f, o_b = hgrn_scan(p_a, lower_bounds[0], batch, t_len, c_len)
    ya = hgrn_finish(o_f, o_b, p_a, hgrn_onorm_g[0].reshape(1, A_DK))
    lambda_init = 0.8 - 0.6 * math.exp(-0.3 * 0)
    yb, yb_ctx = diff_attention(p_qk, v_t, diff_lambda[0], diff_subln_g[0], lambda_init, batch, t_len, c_len)
    na = x_all.shape[0]
    x_new, h2, logits = out_proj(ya, yb, 0, even_w_out[0].astype(BF16), x_all, mods,
                                 norm_ffn_g[0].reshape(1, d), router_wt, na, mod_idx, yb_tail=yb_ctx)
    x_all = moe_layer(h2, logits, router_b, 0, moe_w_gate, moe_w_up, moe_w_down, x_new, mods,
                      final_norm_g.reshape(1, d), False, mod_idx_tok)

    mods = mods_all[1].reshape(8, 1, 6 * d)
    w_qkv = odd_w_qkv[0]
    q_cols = C_HEADS * C_HEAD_DIM
    kv_cols = C_KV_HEADS * C_HEAD_DIM

    def rep_heads(wkv):
        wkv = wkv.reshape(d, C_KV_HEADS, 1, C_HEAD_DIM)
        return jnp.broadcast_to(wkv, (d, C_KV_HEADS, C_GROUP, C_HEAD_DIM)).reshape(d, q_cols)

    w1 = jnp.concatenate([w_qkv[:, :q_cols], rep_heads(w_qkv[:, q_cols:q_cols + kv_cols]),
                          rep_heads(w_qkv[:, q_cols + kv_cols:])], axis=1).astype(BF16)
    p1 = norm_mod_matmul(x_all, norm_mix_g[1].reshape(1, d), mods, 1, 0, w1, rope, (0, 2 * q_cols),
                         BF16, mod_idx)
    sink_exp = jnp.repeat(swa_sink[0].astype(F32), C_HEAD_DIM).reshape(C_KV_HEADS, 1, C_GROUP * C_HEAD_DIM)
    o1 = window_attention(p1, sink_exp, batch, t_len, c_len)
    x_new, h2, logits = out_proj(o1, o1, 1, odd_w_out[0].astype(BF16), x_all, mods,
                                 norm_ffn_g[1].reshape(1, d), router_wt, n_lat, mod_idx)
    out = moe_layer(h2, logits, router_b, 1, moe_w_gate, moe_w_up, moe_w_down, x_new, mods,
                    final_norm_g.reshape(1, d), True, mod_idx_tok)
    return out.reshape(batch, t_len, d)
```

```python
import functools
import math

import jax
import jax.numpy as jnp
from jax import lax
from jax.experimental import pallas as pl
from jax.experimental.pallas import tpu as pltpu

F32 = jnp.float32
BF16 = jnp.bfloat16
EPS = 1e-6
ROPE_BASE = 10000.0
GRID_W = 64

A_HEADS, A_DK = 4, 128
A_WIDTH = A_HEADS * A_DK
HGRN_CHUNK = 16
B_HEADS, B_HALF = 4, 64
B_DV = 2 * B_HALF
B_WIDTH = B_HEADS * B_DV
C_HEADS, C_KV_HEADS, C_HEAD_DIM, C_WINDOW = 16, 4, 64, 128
C_GROUP = C_HEADS // C_KV_HEADS
N_EXPERTS, N_GROUPS, TOP_K = 16, 4, 2
EXPERTS_PER_GROUP = N_EXPERTS // N_GROUPS

LANES = 128
ROW_TILE = 512
HGRN_TILE = 256
HGRN_HEADS_PER_STEP = 1
ATTN_TQ = 256
DIFF_TK = 512
DIFF_UNROLL = 4
DIFF_TQ = 512
MOE_TILE = 256
MOE_SPLIT = 1
MOE_TOK = 256
DMA_UNROLL = 8
VMEM_LIMIT = 56 * 1024 * 1024
HIGHEST = lax.Precision.HIGHEST
NEG_INF = float("-inf")
LOG2_E = math.log2(math.e)


def _cparams(sem):
    return pltpu.CompilerParams(dimension_semantics=sem, vmem_limit_bytes=VMEM_LIMIT)


def _nt_dot(a, b):
    return lax.dot_general(a, b, (((1,), (1,)), ((), ())), preferred_element_type=F32)


def _sigmoid(x):
    return 0.5 * jnp.tanh(0.5 * x) + 0.5


def _silu(x):
    return x * _sigmoid(x)


def _ada_kernel(c_ref, w_ref, b_ref, o_ref):
    s = _silu(c_ref[...])
    o_ref[...] = jnp.dot(s, w_ref[...], preferred_element_type=F32, precision=HIGHEST) + b_ref[...]


def ada_modulation(cvec, ada_w, ada_b):
    n_layers, d, n6 = ada_w.shape
    tn = 512
    return pl.pallas_call(
        _ada_kernel,
        grid=(n_layers, n6 // tn),
        in_specs=[
            pl.BlockSpec((8, d), lambda l, j: (0, 0)),
            pl.BlockSpec((None, d, tn), lambda l, j: (l, 0, j)),
            pl.BlockSpec((None, 1, tn), lambda l, j: (l, 0, j)),
        ],
        out_specs=pl.BlockSpec((None, 8, tn), lambda l, j: (l, 0, j)),
        out_shape=jax.ShapeDtypeStruct((n_layers, 8, n6), F32),
        compiler_params=_cparams(("parallel", "parallel")),
        name="ada_modulation",
    )(cvec, ada_w, ada_b.reshape(n_layers, 1, n6))


def _rope_slab(x, cos, sin_a, sin_b):
    return x * cos + pltpu.roll(x, LANES - 16, 1) * sin_a + pltpu.roll(x, 16, 1) * sin_b


def _nmm_kernel(x_ref, g_ref, sc_ref, sh_ref, w_ref, cos_ref, sa_ref, sb_ref, *rest, rope_cols, tn):
    x = x_ref[...]
    y = x * lax.rsqrt(jnp.mean(x * x, axis=-1, keepdims=True) + EPS) * g_ref[...]
    h = (y * (1.0 + sc_ref[...]) + sh_ref[...]).astype(BF16)
    if len(rest) == 3:
        wt_ref, o_ref, ot_ref = rest
        ot_ref[...] = _nt_dot(wt_ref[...], h).astype(ot_ref.dtype)
    else:
        (o_ref,) = rest
    n = w_ref.shape[1]
    for c0 in range(0, n, tn):
        acc = jnp.dot(h, w_ref[:, c0:c0 + tn], preferred_element_type=F32)
        if rope_cols[0] <= c0 < rope_cols[1]:
            cos, sa, sb = cos_ref[...], sa_ref[...], sb_ref[...]
            for c in range(tn // LANES):
                sl = slice(c * LANES, (c + 1) * LANES)
                o_ref[:, c0 + c * LANES:c0 + (c + 1) * LANES] = (
                    _rope_slab(acc[:, sl], cos, sa, sb).astype(o_ref.dtype))
        else:
            o_ref[:, c0:c0 + tn] = acc.astype(o_ref.dtype)


def norm_mod_matmul(x_all, g, mods, sc_chunk, sh_chunk, w, rope, rope_cols, out_dtype, mod_idx, tn=512,
                    wt=None):
    na, d = x_all.shape
    n = w.shape[1]
    cos, sa, sb = rope
    assert n % tn == 0 and rope_cols[0] % tn == 0 and rope_cols[1] % tn == 0
    kern = functools.partial(_nmm_kernel, rope_cols=rope_cols, tn=tn)
    row = lambda i: (i, 0)
    in_specs = [
        pl.BlockSpec((ROW_TILE, d), row),
        pl.BlockSpec((1, d), lambda i: (0, 0)),
        pl.BlockSpec((None, 1, d), lambda i: (mod_idx(i), 0, sc_chunk)),
        pl.BlockSpec((None, 1, d), lambda i: (mod_idx(i), 0, sh_chunk)),
        pl.BlockSpec((d, n), lambda i: (0, 0)),
        pl.BlockSpec((ROW_TILE, LANES), row),
        pl.BlockSpec((ROW_TILE, LANES), row),
        pl.BlockSpec((ROW_TILE, LANES), row),
    ]
    out_specs = [pl.BlockSpec((ROW_TILE, n), row)]
    out_shape = [jax.ShapeDtypeStruct((na, n), out_dtype)]
    args = [x_all, g, mods, mods, w, cos, sa, sb]
    if wt is not None:
        nt = wt.shape[0]
        in_specs.append(pl.BlockSpec((nt, d), lambda i: (0, 0)))
        out_specs.append(pl.BlockSpec((nt, ROW_TILE), lambda i: (0, i)))
        out_shape.append(jax.ShapeDtypeStruct((nt, na), out_dtype))
        args.append(wt)
    outs = pl.pallas_call(
        kern,
        grid=(na // ROW_TILE,),
        in_specs=in_specs,
        out_specs=out_specs,
        out_shape=out_shape,
        compiler_params=_cparams(("parallel",)),
        name="norm_mod_matmul",
    )(*args)
    return outs[0] if wt is None else outs


def _hgrn_direction(q_raw, f_raw, v, lb_row, st, reverse):
    tt = q_raw.shape[0]
    n_chunks = tt // HGRN_CHUNK
    qs = _silu(q_raw) * (A_DK ** -0.5)
    f = lb_row + (1.0 - lb_row) * _sigmoid(f_raw)
    kk = 1.0 - f
    lf = jnp.log(f)

    r = lax.broadcasted_iota(jnp.int32, (tt, tt), 0)
    c = lax.broadcasted_iota(jnp.int32, (tt, tt), 1)
    same = (r // HGRN_CHUNK) == (c // HGRN_CHUNK)
    tri = (c >= r) if reverse else (c <= r)
    m_cum = jnp.where(jnp.logical_and(same, tri), 1.0, 0.0).astype(BF16)
    m_tot = jnp.where(same, 1.0, 0.0).astype(BF16)
    hi = lf.astype(BF16)
    r1 = lf - hi.astype(F32)
    mid = r1.astype(BF16)
    lo = (r1 - mid.astype(F32)).astype(BF16)

    def split_dot(m):
        return (jnp.dot(m, hi, preferred_element_type=F32) + jnp.dot(m, mid, preferred_element_type=F32)
                + jnp.dot(m, lo, preferred_element_type=F32))

    cum = split_dot(m_cum)
    tot = split_dot(m_tot)
    q_in = qs * jnp.exp(cum)
    k_out = kk * jnp.exp(tot - cum)

    pos = lax.broadcasted_iota(jnp.int32, (tt, 1), 0) % HGRN_CHUNK
    o = jnp.zeros((tt, A_DK), F32)
    for dist in range(HGRN_CHUNK):
        if dist == 0:
            ks, cs, vs = kk, cum, v
            diff = jnp.zeros_like(cum)
        else:
            shift = (tt - dist) if reverse else dist
            ks = pltpu.roll(kk, shift, 0)
            cs = pltpu.roll(cum, shift, 0)
            vs = pltpu.roll(v, shift, 0)
            valid = (pos <= HGRN_CHUNK - 1 - dist) if reverse else (pos >= dist)
            diff = jnp.where(valid, cum - cs, NEG_INF)
        score = jnp.sum(qs * ks * jnp.exp(diff), axis=-1, keepdims=True)
        o = o + score * vs

    v_t = v.T.astype(BF16)
    q_in_t = q_in.T.astype(BF16)
    k_out_b = k_out.astype(BF16)
    lane_chunk = lax.broadcasted_iota(jnp.int32, (1, tt), 1) // HGRN_CHUNK
    updates = [jnp.dot(v_t * (lane_chunk == ci).astype(BF16), k_out_b, preferred_element_type=F32)
               for ci in range(n_chunks)]
    o_t = jnp.zeros((A_DK, tt), F32)
    order = range(n_chunks - 1, -1, -1) if reverse else range(n_chunks)
    for ci in order:
        lo_r = ci * HGRN_CHUNK
        from_state = jnp.dot(st.astype(BF16), q_in_t, preferred_element_type=F32)
        o_t = o_t + jnp.where(lane_chunk == ci, from_state, 0.0)
        st = st * jnp.exp(tot[lo_r:lo_r + 1]) + updates[ci]
    return o + o_t.T, st


def _hgrn_kernel(qf_ref, ff_ref, vf_ref, qb_ref, fb_ref, vb_ref, lb_ref, of_ref, ob_ref, st_scr):
    @pl.when(pl.program_id(2) == 0)
    def _():
        st_scr[...] = jnp.zeros_like(st_scr)

    for hh in range(HGRN_HEADS_PER_STEP):
        sl = slice(hh * A_DK, (hh + 1) * A_DK)
        o_f, st_f = _hgrn_direction(qf_ref[:, sl], ff_ref[:, sl], vf_ref[:, sl], lb_ref[0:1, sl],
                                    st_scr[2 * hh], False)
        of_ref[:, sl] = o_f
        st_scr[2 * hh] = st_f
        o_b, st_b = _hgrn_direction(qb_ref[:, sl], fb_ref[:, sl], vb_ref[:, sl], lb_ref[1:2, sl],
                                    st_scr[2 * hh + 1], True)
        ob_ref[:, sl] = o_b
        st_scr[2 * hh + 1] = st_b


def hgrn_scan(p_a, lb, batch, t_len, c_len):
    na = p_a.shape[0]
    tt = HGRN_TILE
    n_c, n_t = c_len // tt, t_len // tt
    off_c = batch * n_t

    def rb_f(b, s):
        return jnp.where(s < n_c, off_c + b * n_c + s, b * n_t + (s - n_c))

    def rb_b(b, s):
        return jnp.where(s < n_c, off_c + b * n_c + (n_c - 1 - s), b * n_t + (n_t - 1 - (s - n_c)))

    hps = HGRN_HEADS_PER_STEP
    n_hb = A_HEADS // hps

    def spec(rb, group):
        return pl.BlockSpec((tt, hps * A_DK), lambda b, h, s: (rb(b, s), group * n_hb + h))

    out_sds = jax.ShapeDtypeStruct((na, A_WIDTH), F32)
    return pl.pallas_call(
        _hgrn_kernel,
        grid=(batch, n_hb, n_c + n_t),
        in_specs=[spec(rb_f, 0), spec(rb_f, 1), spec(rb_f, 3),
                  spec(rb_b, 0), spec(rb_b, 2), spec(rb_b, 3),
                  pl.BlockSpec((2, hps * A_DK), lambda b, h, s: (0, h))],
        out_specs=[spec(rb_f, 0), spec(rb_b, 0)],
        out_shape=[out_sds, out_sds],
        scratch_shapes=[pltpu.VMEM((2 * hps, A_DK, A_DK), F32)],
        compiler_params=_cparams(("parallel", "parallel", "arbitrary")),
        name="hgrn_scan",
    )(p_a, p_a, p_a, p_a, p_a, p_a, lb)


def _hgrn_finish_kernel(of_ref, ob_ref, g_ref, gn_ref, o_ref):
    o = of_ref[...] + ob_ref[...]
    gate = _silu(g_ref[...])
    for h in range(A_HEADS):
        sl = slice(h * A_DK, (h + 1) * A_DK)
        oh = o[:, sl]
        y = oh * lax.rsqrt(jnp.mean(oh * oh, axis=-1, keepdims=True) + EPS) * gn_ref[...]
        o_ref[:, sl] = (y * gate[:, sl]).astype(o_ref.dtype)


def hgrn_finish(o_f, o_b, p_a, onorm_g):
    na = o_f.shape[0]
    blk = pl.BlockSpec((ROW_TILE, A_WIDTH), lambda i: (i, 0))
    return pl.pallas_call(
        _hgrn_finish_kernel,
        grid=(na // ROW_TILE,),
        in_specs=[blk, blk, pl.BlockSpec((ROW_TILE, A_WIDTH), lambda i: (i, 4)),
                  pl.BlockSpec((1, A_DK), lambda i: (0, 0))],
        out_specs=blk,
        out_shape=jax.ShapeDtypeStruct((na, A_WIDTH), BF16),
        compiler_params=_cparams(("parallel",)),
        name="hgrn_finish",
    )(o_f, o_b, p_a, onorm_g)


def _diff_attn_kernel(q_ref, kc_ref, vtc_ref, k_ref, vt_ref, lam_ref, gn_ref, o_ref,
                      m_scr, l_scr, acc_scr, *, n_k, unroll, lambda_init):
    q_t = (q_ref[...].astype(F32) * (B_HALF ** -0.5 * LOG2_E)).T.astype(BF16)
    tq = q_t.shape[1]
    sub = lax.broadcasted_iota(jnp.int32, (B_DV, 1), 0)
    zero = jnp.zeros_like(q_t)
    qm = jnp.concatenate([jnp.where(sub < B_HALF, q_t, zero), jnp.where(sub >= B_HALF, q_t, zero)], axis=1)
    nc = 2 * tq

    m_scr[...] = jnp.full_like(m_scr, NEG_INF)
    l_scr[...] = jnp.zeros_like(l_scr)
    acc_scr[...] = jnp.zeros_like(acc_scr)

    def fold8(s, op):
        return op(s.reshape(s.shape[0] // 8, 8, nc), axis=0)

    def update(kvs):
        ss = [jnp.dot(kb, qm, preferred_element_type=F32) for kb, _ in kvs]
        part = functools.reduce(jnp.maximum, [fold8(s, jnp.max) for s in ss])
        m_old = m_scr[0:1, :]
        m_new = jnp.maximum(m_old, jnp.max(part, axis=0, keepdims=True))
        alpha = jnp.exp2(m_old - m_new)
        lsum = alpha * l_scr[...]
        acc = alpha * acc_scr[...]
        for s, (_, vtb) in zip(ss, kvs):
            p = jnp.exp2(s - m_new)
            lsum = lsum + fold8(p, jnp.sum)
            acc = acc + jnp.dot(vtb, p.astype(BF16), preferred_element_type=F32)
        l_scr[...] = lsum
        acc_scr[...] = acc
        m_scr[...] = jnp.broadcast_to(m_new, (8, nc))

    update([(kc_ref[...], vtc_ref[...])])
    for j in range(n_k):
        kvs = []
        for u in range(unroll):
            start = (j * unroll + u) * DIFF_TK
            kvs.append((k_ref[start:start + DIFF_TK, :], vt_ref[:, start:start + DIFF_TK]))
        update(kvs)

    lp = lam_ref[...]
    lam = (jnp.exp(jnp.sum(lp[0:1] * lp[1:2], axis=-1, keepdims=True))
           - jnp.exp(jnp.sum(lp[2:3] * lp[3:4], axis=-1, keepdims=True)) + lambda_init)
    on = acc_scr[...] / jnp.sum(l_scr[...], axis=0, keepdims=True)
    o = on[:, :tq] - lam * on[:, tq:]
    y = o * lax.rsqrt(jnp.mean(o * o, axis=0, keepdims=True) + EPS) * gn_ref[...]
    o_ref[...] = (y * (1.0 - lambda_init)).T.astype(o_ref.dtype)


def diff_attention(p_qk, v_t, lam_p, subln_g, lambda_init, batch, t_len, c_len):
    off_c = batch * (t_len // c_len)
    unroll = min(DIFF_UNROLL, t_len // DIFF_TK)
    assert t_len % (unroll * DIFF_TK) == 0 and t_len % DIFF_TQ == 0

    def call(tq, n_q, q_off, n_k):
        kern = functools.partial(_diff_attn_kernel, n_k=n_k, unroll=unroll, lambda_init=lambda_init)
        in_specs = [
            pl.BlockSpec((tq, B_DV), lambda b, h, i: (q_off + b * n_q + i, h)),
            pl.BlockSpec((c_len, B_DV), lambda b, h, i: (off_c + b, B_HEADS + h)),
            pl.BlockSpec((B_DV, c_len), lambda b, h, i: (h, off_c + b)),
            pl.BlockSpec((t_len, B_DV), lambda b, h, i: (b, B_HEADS + h)),
            pl.BlockSpec((B_DV, t_len), lambda b, h, i: (h, b)),
            pl.BlockSpec((4, B_HALF), lambda b, h, i: (0, 0)),
            pl.BlockSpec((B_DV, 1), lambda b, h, i: (0, 0)),
        ]
        args = [p_qk, p_qk, v_t, p_qk, v_t, lam_p, subln_g.reshape(B_DV, 1)]
        return pl.pallas_call(
            kern,
            grid=(batch, B_HEADS, n_q),
            in_specs=in_specs,
            out_specs=pl.BlockSpec((tq, B_DV), lambda b, h, i: (b * n_q + i, h)),
            out_shape=jax.ShapeDtypeStruct((batch * n_q * tq, B_WIDTH), BF16),
            scratch_shapes=[pltpu.VMEM((8, 2 * tq), F32), pltpu.VMEM((8, 2 * tq), F32),
                            pltpu.VMEM((B_DV, 2 * tq), F32)],
            compiler_params=_cparams(("parallel", "parallel", "arbitrary")),
            name="diff_attention",
        )(*args)

    y_lat = call(DIFF_TQ, t_len // DIFF_TQ, 0, t_len // (unroll * DIFF_TK))
    y_ctx = call(c_len, 1, off_c, 0)
    return y_lat, y_ctx


def _swa_kernel(q_ref, kc_ref, vtc_ref, k_ref, *rest, t_len, n_vt):
    vt_refs, (sink_ref, o_ref) = rest[:n_vt], rest[n_vt:]
    i = pl.program_id(1)
    tq = q_ref.shape[0]
    kvw = C_KV_HEADS * C_HEAD_DIM
    win = tq + 2 * C_WINDOW
    a = i * tq
    start = pl.multiple_of(jnp.clip(a - C_WINDOW, 0, t_len - win), C_WINDOW)
    kw = k_ref[pl.ds(start, win), :]
    vtw = jnp.concatenate([r[...] for r in vt_refs], axis=1)
    kc, vtc = kc_ref[...], vtc_ref[...]
    kpos = start + lax.broadcasted_iota(jnp.int32, (win, 1), 0)
    qpos = a + lax.broadcasted_iota(jnp.int32, (1, tq), 1)
    valid = jnp.abs(kpos - qpos) <= C_WINDOW
    q_t = (q_ref[...].astype(F32) * (C_HEAD_DIM ** -0.5)).T.astype(BF16)
    valid_g = jnp.concatenate([valid] * C_GROUP, axis=1)
    for g in range(C_KV_HEADS):
        lo, hi = g * C_HEAD_DIM, (g + 1) * C_HEAD_DIM
        heads = range(g * C_GROUP, (g + 1) * C_GROUP)
        q_g = jnp.concatenate([q_t[h * C_HEAD_DIM:(h + 1) * C_HEAD_DIM] for h in heads], axis=1)
        pieces = []
        if lo:
            pieces.append(jnp.zeros((lo, C_GROUP * tq), BF16))
        pieces.append(q_g)
        if kvw - hi:
            pieces.append(jnp.zeros((kvw - hi, C_GROUP * tq), BF16))
        qm = jnp.concatenate(pieces, axis=0)
        sink = jnp.concatenate([jnp.broadcast_to(sink_ref[h:h + 1, :], (1, tq)) for h in heads], axis=1)
        s_c = jnp.dot(kc, qm, preferred_element_type=F32)
        s_w = jnp.where(valid_g, jnp.dot(kw, qm, preferred_element_type=F32), NEG_INF)
        m = jnp.maximum(jnp.maximum(jnp.max(s_c, axis=0, keepdims=True),
                                    jnp.max(s_w, axis=0, keepdims=True)), sink)
        p_c = jnp.exp(s_c - m)
        p_w = jnp.exp(s_w - m)
        den = (jnp.exp(sink - m) + jnp.sum(p_c, axis=0, keepdims=True)
               + jnp.sum(p_w, axis=0, keepdims=True))
        o_t = (jnp.dot(vtc[lo:hi], p_c.astype(BF16), preferred_element_type=F32)
               + jnp.dot(vtw[lo:hi], p_w.astype(BF16), preferred_element_type=F32)) / den
        per_slab = LANES // C_HEAD_DIM
        for s in range(C_GROUP // per_slab):
            slab = jnp.concatenate([o_t[:, (s * per_slab + j) * tq:(s * per_slab + j + 1) * tq]
                                    for j in range(per_slab)], axis=0)
            c0 = (g * C_GROUP + s * per_slab) * C_HEAD_DIM
            o_ref[:, c0:c0 + LANES] = slab.T.astype(o_ref.dtype)


def window_attention(p_qk, v_t, sink, batch, t_len, c_len):
    tq = ATTN_TQ
    qw = C_HEADS * C_HEAD_DIM
    kvw = C_KV_HEADS * C_HEAD_DIM
    assert qw % kvw == 0
    k_col = qw // kvw
    n_q = t_len // tq
    off_c = batch * (t_len // c_len)
    win = tq + 2 * C_WINDOW
    n_vt = win // C_WINDOW
    per_b = t_len // C_WINDOW

    def vt_blk(j):
        def index(b, i):
            first = jnp.clip(i * (tq // C_WINDOW) - 1, 0, per_b - n_vt)
            return (0, b * per_b + first + j)
        return pl.BlockSpec((kvw, C_WINDOW), index)

    kern = functools.partial(_swa_kernel, t_len=t_len, n_vt=n_vt)
    return pl.pallas_call(
        kern,
        grid=(batch, n_q),
        in_specs=[
            pl.BlockSpec((tq, qw), lambda b, i: (b * n_q + i, 0)),
            pl.BlockSpec((c_len, kvw), lambda b, i: (off_c + b, k_col)),
            pl.BlockSpec((kvw, c_len), lambda b, i: (0, off_c + b)),
            pl.BlockSpec((t_len, kvw), lambda b, i: (b, k_col)),
            *[vt_blk(j) for j in range(n_vt)],
            pl.BlockSpec((C_HEADS, 1), lambda b, i: (0, 0)),
        ],
        out_specs=pl.BlockSpec((tq, qw), lambda b, i: (b * n_q + i, 0)),
        out_shape=jax.ShapeDtypeStruct((batch * t_len, qw), BF16),
        compiler_params=_cparams(("parallel", "arbitrary")),
        name="window_attention",
    )(p_qk, p_qk, v_t, p_qk, *([v_t] * n_vt), sink.astype(F32).reshape(C_HEADS, 1))


def _to_row_tiles(ref, val):
    for s in range(val.shape[1] // LANES):
        ref[:, s, :] = val[:, s * LANES:(s + 1) * LANES]


def _from_row_tiles(ref):
    return jnp.concatenate([ref[:, s, :] for s in range(ref.shape[1])], axis=1)


def _out_proj_kernel(ya_ref, yb_ref, wa_ref, wb_ref, x_ref, g1_ref, g_ref, sc_ref, sh_ref, rw_ref,
                     *rest, n_main):
    if len(rest) == 4:
        yt_ref, xo_ref, h_ref, lg_ref = rest
        yb = jnp.where(pl.program_id(0) < n_main, yb_ref[...], yt_ref[...])
    else:
        xo_ref, h_ref, lg_ref = rest
        yb = yb_ref[...]
    y = (jnp.dot(ya_ref[...], wa_ref[...], preferred_element_type=F32)
         + jnp.dot(yb, wb_ref[...], preferred_element_type=F32))
    xn = x_ref[...] + g1_ref[...] * y
    xo_ref[...] = xn
    hn = xn * lax.rsqrt(jnp.mean(xn * xn, axis=-1, keepdims=True) + EPS) * g_ref[...]
    h2 = hn * (1.0 + sc_ref[...]) + sh_ref[...]
    _to_row_tiles(h_ref, h2)
    lg_ref[...] = lax.dot_general(rw_ref[...], h2, (((1,), (1,)), ((), ())),
                                  preferred_element_type=F32, precision=HIGHEST)


def out_proj(ya, yb_arr, yb_col, w_out, x_all, mods, g2n, router_wt, rows, mod_idx, yb_tail=None):
    d = x_all.shape[1]
    half = d // 2
    ne = router_wt.shape[0]
    row = lambda i: (i, 0)
    n_main = yb_arr.shape[0] // ROW_TILE
    in_specs = [
        pl.BlockSpec((ROW_TILE, half), row),
        pl.BlockSpec((ROW_TILE, half), lambda i: (jnp.minimum(i, n_main - 1), yb_col)),
        pl.BlockSpec((half, d), lambda i: (0, 0)),
        pl.BlockSpec((half, d), lambda i: (1, 0)),
        pl.BlockSpec((ROW_TILE, d), row),
        pl.BlockSpec((None, 1, d), lambda i: (mod_idx(i), 0, 2)),
        pl.BlockSpec((1, d), lambda i: (0, 0)),
        pl.BlockSpec((None, 1, d), lambda i: (mod_idx(i), 0, 4)),
        pl.BlockSpec((None, 1, d), lambda i: (mod_idx(i), 0, 3)),
        pl.BlockSpec((ne, d), lambda i: (0, 0)),
    ]
    args = [ya, yb_arr, w_out, w_out, x_all, mods, g2n, mods, mods, router_wt]
    if yb_tail is not None:
        in_specs.append(pl.BlockSpec((ROW_TILE, half), lambda i: (jnp.maximum(i - n_main, 0), 0)))
        args.append(yb_tail)
    return pl.pallas_call(
        functools.partial(_out_proj_kernel, n_main=n_main),
        grid=(rows // ROW_TILE,),
        in_specs=in_specs,
        out_specs=[pl.BlockSpec((ROW_TILE, d), row),
                   pl.BlockSpec((ROW_TILE, d // LANES, LANES), lambda i: (i, 0, 0)),
                   pl.BlockSpec((ne, ROW_TILE), lambda i: (0, i))],
        out_shape=[jax.ShapeDtypeStruct((rows, d), F32),
                   jax.ShapeDtypeStruct((rows, d // LANES, LANES), F32),
                   jax.ShapeDtypeStruct((ne, rows), F32)],
        compiler_params=_cparams(("parallel",)),
        name="out_proj",
    )(*args)


def _router_kernel(lg_ref, rb_ref, ids_ref, w_ref, cnt_ref, carry):
    @pl.when(pl.program_id(0) == 0)
    def _():
        carry[...] = jnp.zeros_like(carry)

    tr = lg_ref.shape[1]
    aff = jax.nn.sigmoid(lg_ref[...])
    sel = aff + rb_ref[...]
    rows = [sel[e:e + 1] for e in range(N_EXPERTS)]

    def beats(a, b, a_first):
        return jnp.logical_or(a > b, jnp.logical_and(a == b, a_first))

    def rank_among(vals, j):
        r = jnp.zeros(vals[0].shape, jnp.int32)
        for i2 in range(len(vals)):
            if i2 != j:
                r = r + beats(vals[i2], vals[j], i2 < j).astype(jnp.int32)
        return r

    rank, gscore = [], []
    for g in range(N_GROUPS):
        grp_rows = rows[g * EXPERTS_PER_GROUP:(g + 1) * EXPERTS_PER_GROUP]
        grp_rank = [rank_among(grp_rows, j) for j in range(EXPERTS_PER_GROUP)]
        rank += grp_rank
        gscore.append(sum(jnp.where(grp_rank[j] < TOP_K, grp_rows[j], 0.0) for j in range(EXPERTS_PER_GROUP)))
    chosen = [rank_among(gscore, g) == 0 for g in range(N_GROUPS)]
    onehot = []
    for k in range(TOP_K):
        hot_rows = [jnp.logical_and(chosen[e // EXPERTS_PER_GROUP], rank[e] == k).astype(F32)
                    for e in range(N_EXPERTS)]
        onehot.append(jnp.concatenate(hot_rows, axis=0))

    e_idx = lax.broadcasted_iota(jnp.int32, (N_EXPERTS, 1), 0).astype(F32)
    picked = [jnp.sum(onehot[k] * aff, axis=0, keepdims=True) for k in range(TOP_K)]
    denom = picked[0] + picked[1]
    before = (lax.broadcasted_iota(jnp.int32, (tr, tr), 0)
              < lax.broadcasted_iota(jnp.int32, (tr, tr), 1)).astype(BF16)
    both = onehot[0] + onehot[1]
    seen = carry[:, 0:1] + jnp.dot(both.astype(BF16), before, preferred_element_type=F32)
    id_rows = [jnp.sum(onehot[k] * e_idx, axis=0, keepdims=True) for k in range(TOP_K)]
    id_rows += [jnp.sum(onehot[k] * seen, axis=0, keepdims=True) for k in range(TOP_K)]
    pad = jnp.zeros((8 - 2 * TOP_K, tr), F32)
    ids_ref[...] = jnp.concatenate(id_rows + [pad], axis=0).astype(jnp.int32)
    w_ref[...] = jnp.concatenate([picked[0] / denom, picked[1] / denom,
                                  jnp.zeros((8 - TOP_K, tr), F32)], axis=0)
    carry[...] = carry[...] + jnp.sum(both, axis=1, keepdims=True)
    cnt_ref[...] = carry[...]


def moe_router(logits_t, router_b):
    ne, rows = logits_t.shape
    tr = ROW_TILE
    return pl.pallas_call(
        _router_kernel,
        grid=(rows // tr,),
        in_specs=[pl.BlockSpec((ne, tr), lambda i: (0, i)), pl.BlockSpec((ne, 1), lambda i: (0, 0))],
        out_specs=[pl.BlockSpec((8, tr), lambda i: (0, i)), pl.BlockSpec((8, tr), lambda i: (0, i)),
                   pl.BlockSpec((ne, LANES), lambda i: (0, 0))],
        out_shape=[jax.ShapeDtypeStruct((8, rows), jnp.int32), jax.ShapeDtypeStruct((8, rows), F32),
                   jax.ShapeDtypeStruct((ne, LANES), F32)],
        scratch_shapes=[pltpu.VMEM((ne, LANES), F32)],
        compiler_params=_cparams(("arbitrary",)),
        name="moe_router",
    )(logits_t, router_b.astype(F32).reshape(ne, 1))


def _tile_copy(src, dst, sem):
    return pltpu.make_async_copy(src, dst, sem)


def _dispatch_kernel(dest_ref, h_ref, zero_hbm, xs_hbm, sem):
    del zero_hbm
    n_tok = h_ref.shape[0]

    def start(t, carry):
        for k in range(TOP_K):
            _tile_copy(h_ref.at[t], xs_hbm.at[dest_ref[0, 0, TOP_K * t + k]], sem).start(priority=k)
        return carry

    lax.fori_loop(0, n_tok, start, 0, unroll=DMA_UNROLL)
    for k in range(TOP_K):
        _tile_copy(h_ref, xs_hbm.at[pl.ds(0, n_tok)], sem).wait()


def moe_dispatch(h2, dest, n_rows):
    rows, ns, nl = h2.shape
    n_t = rows // MOE_TOK
    dest3 = dest.reshape(n_t, 1, MOE_TOK * TOP_K)
    zeros = jnp.zeros((n_rows, ns, nl), F32)
    return pl.pallas_call(
        _dispatch_kernel,
        grid=(n_t,),
        in_specs=[
            pl.BlockSpec((1, 1, MOE_TOK * TOP_K), lambda i: (i, 0, 0), memory_space=pltpu.SMEM),
            pl.BlockSpec((MOE_TOK, ns, nl), lambda i: (i, 0, 0)),
            pl.BlockSpec(memory_space=pl.ANY),
        ],
        out_specs=pl.BlockSpec(memory_space=pl.ANY),
        out_shape=jax.ShapeDtypeStruct((n_rows, ns, nl), F32),
        scratch_shapes=[pltpu.SemaphoreType.DMA],
        input_output_aliases={2: 0},
        compiler_params=_cparams(("arbitrary",)),
        name="moe_dispatch",
    )(dest3, h2, zeros)


def _expert_kernel(be_ref, nv_ref, x_ref, wg_ref, wu_ref, wd_ref, o_ref, wg_s, wu_s, wd_s):
    i = pl.program_id(0)
    new_expert = jnp.logical_or(i == 0, be_ref[i] != be_ref[jnp.maximum(i - 1, 0)])

    @pl.when(new_expert)
    def _():
        wg_s[...] = wg_ref[...].astype(BF16)
        wu_s[...] = wu_ref[...].astype(BF16)
        wd_s[...] = wd_ref[...].astype(BF16)

    @pl.when(i < nv_ref[0])
    def _():
        rows = x_ref.shape[0] // MOE_SPLIT
        for part in range(MOE_SPLIT):
            xr = x_ref.at[part * rows:(part + 1) * rows]
            x = _from_row_tiles(xr).astype(BF16)
            hid = (_silu(jnp.dot(x, wg_s[...], preferred_element_type=F32))
                   * jnp.dot(x, wu_s[...], preferred_element_type=F32))
            _to_row_tiles(o_ref.at[part * rows:(part + 1) * rows],
                          jnp.dot(hid.astype(BF16), wd_s[...], preferred_element_type=F32))

    @pl.when(i >= nv_ref[0])
    def _():
        o_ref[...] = jnp.zeros_like(o_ref)


def moe_experts(x_sorted, block_e, n_valid, layer, w_gate, w_up, w_down):
    n_rows, ns, nl = x_sorted.shape
    d, de = w_gate.shape[2], w_gate.shape[3]
    n_blocks = n_rows // MOE_TILE
    grid_spec = pltpu.PrefetchScalarGridSpec(
        num_scalar_prefetch=2,
        grid=(n_blocks,),
        in_specs=[
            pl.BlockSpec((MOE_TILE, ns, nl), lambda i, be, nv: (i, 0, 0)),
            pl.BlockSpec((None, None, d, de), lambda i, be, nv: (layer, be[i], 0, 0)),
            pl.BlockSpec((None, None, d, de), lambda i, be, nv: (layer, be[i], 0, 0)),
            pl.BlockSpec((None, None, de, d), lambda i, be, nv: (layer, be[i], 0, 0)),
        ],
        out_specs=pl.BlockSpec((MOE_TILE, ns, nl), lambda i, be, nv: (i, 0, 0)),
        scratch_shapes=[pltpu.VMEM((d, de), BF16), pltpu.VMEM((d, de), BF16), pltpu.VMEM((de, d), BF16)],
    )
    return pl.pallas_call(
        _expert_kernel,
        grid_spec=grid_spec,
        out_shape=jax.ShapeDtypeStruct((n_rows, ns, nl), F32),
        compiler_params=_cparams(("arbitrary",)),
        name="moe_experts",
    )(block_e, n_valid, x_sorted, w_gate, w_up, w_down)


def _combine_kernel(dest_ref, dnext_ref, ys_hbm, w_ref, x_ref, g2_ref, gf_ref, o_ref, buf, sem, *, final_norm):
    i = pl.program_id(0)
    n_tok = buf.shape[2]

    def gather(d_ref, slot):
        def start(t, carry):
            for k in range(TOP_K):
                _tile_copy(ys_hbm.at[d_ref[0, 0, TOP_K * t + k]], buf.at[slot, k, t],
                           sem.at[slot]).start(priority=k)
            return carry

        lax.fori_loop(0, n_tok, start, 0, unroll=DMA_UNROLL)

    slot = i % 2

    @pl.when(i == 0)
    def _():
        gather(dest_ref, slot)

    @pl.when(i + 1 < pl.num_programs(0))
    def _():
        gather(dnext_ref, 1 - slot)

    for k in range(TOP_K):
        _tile_copy(ys_hbm.at[pl.ds(0, n_tok)], buf.at[slot, k], sem.at[slot]).wait()
    w = w_ref[...]
    f = w[:, 0:1] * _from_row_tiles(buf.at[slot, 0]) + w[:, 1:2] * _from_row_tiles(buf.at[slot, 1])
    xn = x_ref[...] + g2_ref[...] * f
    if final_norm:
        xn = xn * lax.rsqrt(jnp.mean(xn * xn, axis=-1, keepdims=True) + EPS) * gf_ref[...]
    o_ref[...] = xn


def moe_combine(y_sorted, dest, wts, x_new, mods, final_g, final_norm, mod_idx_tok):
    rows, d = x_new.shape
    n_t = rows // MOE_TOK
    dest3 = dest.reshape(n_t, 1, MOE_TOK * TOP_K)
    kern = functools.partial(_combine_kernel, final_norm=final_norm)
    return pl.pallas_call(
        kern,
        grid=(n_t,),
        in_specs=[
            pl.BlockSpec((1, 1, MOE_TOK * TOP_K), lambda i: (i, 0, 0), memory_space=pltpu.SMEM),
            pl.BlockSpec((1, 1, MOE_TOK * TOP_K), lambda i: (jnp.minimum(i + 1, n_t - 1), 0, 0),
                         memory_space=pltpu.SMEM),
            pl.BlockSpec(memory_space=pl.ANY),
            pl.BlockSpec((MOE_TOK, TOP_K), lambda i: (i, 0)),
            pl.BlockSpec((MOE_TOK, d), lambda i: (i, 0)),
            pl.BlockSpec((None, 1, d), lambda i: (mod_idx_tok(i), 0, 5)),
            pl.BlockSpec((1, d), lambda i: (0, 0)),
        ],
        out_specs=pl.BlockSpec((MOE_TOK, d), lambda i: (i, 0)),
        out_shape=jax.ShapeDtypeStruct((rows, d), F32),
        scratch_shapes=[pltpu.VMEM((2, TOP_K, MOE_TOK, d // LANES, LANES), F32),
                        pltpu.SemaphoreType.DMA((2,))],
        compiler_params=_cparams(("arbitrary",)),
        name="moe_combine",
    )(dest3, dest3, y_sorted, wts, x_new, mods, final_g)


def _row_plan(ids, counts, n_tok):
    counts = counts[:, 0].astype(jnp.int32)
    padded = (counts + MOE_TILE - 1) // MOE_TILE * MOE_TILE
    pad_end = jnp.cumsum(padded)
    pad_start = pad_end - padded
    experts = jnp.arange(N_EXPERTS, dtype=jnp.int32)[:, None]
    dest = [jnp.sum(jnp.where(ids[k][None, :] == experts, pad_start[:, None], 0), axis=0) + ids[TOP_K + k]
            for k in range(TOP_K)]
    dest = jnp.stack(dest, axis=1).astype(jnp.int32)
    n_blocks = -(-(n_tok * TOP_K) // MOE_TILE) + N_EXPERTS
    block_start = jnp.arange(n_blocks, dtype=jnp.int32) * MOE_TILE
    block_e = jnp.minimum(jnp.sum(pad_end[None, :] <= block_start[:, None], axis=1), N_EXPERTS - 1)
    n_valid = (pad_end[-1] // MOE_TILE).reshape(1)
    return dest, block_e.astype(jnp.int32), n_valid.astype(jnp.int32), n_blocks


def moe_layer(h2, logits_t, router_b, layer, w_gate, w_up, w_down, x_new, mods, final_g, final_norm,
              mod_idx_tok):
    n_tok = h2.shape[0]
    ids, wts8, counts = moe_router(logits_t, router_b)
    dest, block_e, n_valid, n_blocks = _row_plan(ids, counts, n_tok)
    wts = wts8[:TOP_K].T
    x_sorted = moe_dispatch(h2, dest, n_blocks * MOE_TILE)
    y_sorted = moe_experts(x_sorted, block_e, n_valid, layer, w_gate, w_up, w_down)
    return moe_combine(y_sorted, dest, wts, x_new, mods, final_g, final_norm, mod_idx_tok)


def _rope_tables(batch, t_len, c_len):
    n_rows = t_len // GRID_W
    row = jnp.repeat(jnp.arange(n_rows), GRID_W).astype(F32)
    col = jnp.tile(jnp.arange(GRID_W), n_rows).astype(F32)
    half = B_HALF // 2
    inv = 1.0 / (ROPE_BASE ** (jnp.arange(0, half, 2, dtype=F32) / half))
    ar, ac = row[:, None] * inv, col[:, None] * inv
    ang = jnp.concatenate([ar, ar, ac, ac], axis=-1)
    cos, sin = jnp.cos(ang), jnp.sin(ang)
    first = (jnp.arange(B_HALF) % 32) < 16
    sin_a = jnp.where(first, -sin, 0.0)
    sin_b = jnp.where(first, 0.0, sin)

    def full(tab, ctx_val):
        lat = jnp.tile(jnp.tile(tab, (1, LANES // B_HALF)), (batch, 1))
        ctx = jnp.full((batch * c_len, LANES), ctx_val, F32)
        return jnp.concatenate([lat, ctx], axis=0)

    return full(cos, 1.0), full(sin_a, 0.0), full(sin_b, 0.0)


def kernel(x, c, ctx, c_ctx, ada_w, ada_b, norm_mix_g, norm_ffn_g, even_w_in, even_w_out, hgrn_lb_logits,
           hgrn_onorm_g, diff_lambda, diff_subln_g, odd_w_qkv, odd_w_out, swa_sink, router_w, router_b,
           moe_w_gate, moe_w_up, moe_w_down, final_norm_g):
    batch, t_len, d = x.shape
    c_len = ctx.shape[1]
    n_lat = batch * t_len
    assert t_len % ROW_TILE == 0 and (batch * c_len) % ROW_TILE == 0 and c_len == HGRN_TILE
    assert batch < 8 and t_len % DIFF_TK == 0 and t_len % GRID_W == 0

    x_all = jnp.concatenate([x.reshape(n_lat, d), ctx.reshape(batch * c_len, d)], axis=0)
    rope = _rope_tables(batch, t_len, c_len)

    def mod_idx_for(tile):
        per_batch = t_len // tile
        return lambda i: jnp.minimum(i // per_batch, batch)

    mod_idx = mod_idx_for(ROW_TILE)
    mod_idx_tok = mod_idx_for(MOE_TOK)

    cvec = jnp.zeros((8, d), F32).at[:batch].set(c).at[batch].set(c_ctx)
    mods_all = ada_modulation(cvec, ada_w, ada_b)
    lower_bounds = jnp.cumsum(jax.nn.softmax(hgrn_lb_logits.astype(F32), axis=0), axis=0)
    router_wt = router_w.astype(F32).T

    mods = mods_all[0].reshape(8, 1, 6 * d)
    g_mix = norm_mix_g[0].reshape(1, d)
    w_in = even_w_in[0].astype(BF16)
    n_a = 5 * A_WIDTH
    p_a = norm_mod_matmul(x_all, g_mix, mods, 1, 0, w_in[:, :n_a], rope, (0, 0), F32, mod_idx)
    n_qk = n_a + 2 * B_WIDTH
    p_qk, v_t = norm_mod_matmul(x_all, g_mix, mods, 1, 0, w_in[:, n_a:n_qk], rope, (0, 2 * B_WIDTH), BF16,
                                mod_idx, wt=w_in[:, n_qk:].T)
    o_f, o_b = hgrn_scan(p_a, lower_bounds[0], batch, t_len, c_len)
    ya = hgrn_finish(o_f, o_b, p_a, hgrn_onorm_g[0].reshape(1, A_DK))
    lambda_init = 0.8 - 0.6 * math.exp(-0.3 * 0)
    yb, yb_ctx = diff_attention(p_qk, v_t, diff_lambda[0], diff_subln_g[0], lambda_init, batch, t_len, c_len)
    na = x_all.shape[0]
    x_new, h2, logits = out_proj(ya, yb, 0, even_w_out[0].astype(BF16), x_all, mods,
                                 norm_ffn_g[0].reshape(1, d), router_wt, na, mod_idx, yb_tail=yb_ctx)
    x_all = moe_layer(h2, logits, router_b, 0, moe_w_gate, moe_w_up, moe_w_down, x_new, mods,
                      final_norm_g.reshape(1, d), False, mod_idx_tok)

    mods = mods_all[1].reshape(8, 1, 6 * d)
    w_qkv = odd_w_qkv[0]
    q_cols = C_HEADS * C_HEAD_DIM
    kv_cols = C_KV_HEADS * C_HEAD_DIM

    n_qk1 = q_cols + kv_cols
    w_qkv = w_qkv.astype(BF16)
    p_qk1, v_t1 = norm_mod_matmul(x_all, norm_mix_g[1].reshape(1, d), mods, 1, 0, w_qkv[:, :n_qk1], rope,
                                  (0, n_qk1), BF16, mod_idx, tn=kv_cols, wt=w_qkv[:, n_qk1:].T)
    o1 = window_attention(p_qk1, v_t1, swa_sink[0], batch, t_len, c_len)
    x_new, h2, logits = out_proj(o1, o1, 1, odd_w_out[0].astype(BF16), x_all, mods,
                                 norm_ffn_g[1].reshape(1, d), router_wt, n_lat, mod_idx)
    out = moe_layer(h2, logits, router_b, 1, moe_w_gate, moe_w_up, moe_w_down, x_new, mods,
                    final_norm_g.reshape(1, d), True, mod_idx_tok)
    return out.reshape(batch, t_len, d)
```

```python
import functools
import math

import jax
import jax.numpy as jnp
from jax import lax
from jax.experimental import pallas as pl
from jax.experimental.pallas import tpu as pltpu

F32 = jnp.float32
BF16 = jnp.bfloat16
EPS = 1e-6
ROPE_BASE = 10000.0
GRID_W = 64

A_HEADS, A_DK = 4, 128
A_WIDTH = A_HEADS * A_DK
HGRN_CHUNK = 16
B_HEADS, B_HALF = 4, 64
B_DV = 2 * B_HALF
B_WIDTH = B_HEADS * B_DV
C_HEADS, C_KV_HEADS, C_HEAD_DIM, C_WINDOW = 16, 4, 64, 128
C_GROUP = C_HEADS // C_KV_HEADS
N_EXPERTS, N_GROUPS, TOP_K = 16, 4, 2
EXPERTS_PER_GROUP = N_EXPERTS // N_GROUPS

LANES = 128
ROW_TILE = 512
HGRN_TILE = 256
HGRN_HEADS_PER_STEP = 1
ATTN_TQ = 256
DIFF_TK = 512
DIFF_UNROLL = 4
DIFF_TQ = 512
MOE_TILE = 256
MOE_SPLIT = 1
MOE_TOK = 256
DMA_UNROLL = 8
VMEM_LIMIT = 56 * 1024 * 1024
HIGHEST = lax.Precision.HIGHEST
NEG_INF = float("-inf")
LOG2_E = math.log2(math.e)


def _cparams(sem):
    return pltpu.CompilerParams(dimension_semantics=sem, vmem_limit_bytes=VMEM_LIMIT)


def _nt_dot(a, b):
    return lax.dot_general(a, b, (((1,), (1,)), ((), ())), preferred_element_type=F32)


def _sigmoid(x):
    return 0.5 * jnp.tanh(0.5 * x) + 0.5


def _silu(x):
    return x * _sigmoid(x)


def _ada_kernel(c_ref, w_ref, b_ref, o_ref):
    s = _silu(c_ref[...])
    o_ref[...] = jnp.dot(s, w_ref[...], preferred_element_type=F32, precision=HIGHEST) + b_ref[...]


def ada_modulation(cvec, ada_w, ada_b):
    n_layers, d, n6 = ada_w.shape
    tn = 512
    return pl.pallas_call(
        _ada_kernel,
        grid=(n_layers, n6 // tn),
        in_specs=[
            pl.BlockSpec((8, d), lambda l, j: (0, 0)),
            pl.BlockSpec((None, d, tn), lambda l, j: (l, 0, j)),
            pl.BlockSpec((None, 1, tn), lambda l, j: (l, 0, j)),
        ],
        out_specs=pl.BlockSpec((None, 8, tn), lambda l, j: (l, 0, j)),
        out_shape=jax.ShapeDtypeStruct((n_layers, 8, n6), F32),
        compiler_params=_cparams(("parallel", "parallel")),
        name="ada_modulation",
    )(cvec, ada_w, ada_b.reshape(n_layers, 1, n6))


def _rope_slab(x, cos, sin_a, sin_b):
    return x * cos + pltpu.roll(x, LANES - 16, 1) * sin_a + pltpu.roll(x, 16, 1) * sin_b


def _nmm_kernel(x_ref, g_ref, sc_ref, sh_ref, w_ref, cos_ref, sa_ref, sb_ref, *rest, rope_cols, tn):
    x = x_ref[...]
    y = x * lax.rsqrt(jnp.mean(x * x, axis=-1, keepdims=True) + EPS) * g_ref[...]
    h = (y * (1.0 + sc_ref[...]) + sh_ref[...]).astype(BF16)
    if len(rest) == 3:
        wt_ref, o_ref, ot_ref = rest
        ot_ref[...] = _nt_dot(wt_ref[...], h).astype(ot_ref.dtype)
    else:
        (o_ref,) = rest
    n = w_ref.shape[1]
    for c0 in range(0, n, tn):
        acc = jnp.dot(h, w_ref[:, c0:c0 + tn], preferred_element_type=F32)
        if rope_cols[0] <= c0 < rope_cols[1]:
            cos, sa, sb = cos_ref[...], sa_ref[...], sb_ref[...]
            for c in range(tn // LANES):
                sl = slice(c * LANES, (c + 1) * LANES)
                o_ref[:, c0 + c * LANES:c0 + (c + 1) * LANES] = (
                    _rope_slab(acc[:, sl], cos, sa, sb).astype(o_ref.dtype))
        else:
            o_ref[:, c0:c0 + tn] = acc.astype(o_ref.dtype)


def norm_mod_matmul(x_all, g, mods, sc_chunk, sh_chunk, w, rope, rope_cols, out_dtype, mod_idx, tn=512,
                    wt=None):
    na, d = x_all.shape
    n = w.shape[1]
    cos, sa, sb = rope
    assert n % tn == 0 and rope_cols[0] % tn == 0 and rope_cols[1] % tn == 0
    kern = functools.partial(_nmm_kernel, rope_cols=rope_cols, tn=tn)
    row = lambda i: (i, 0)
    in_specs = [
        pl.BlockSpec((ROW_TILE, d), row),
        pl.BlockSpec((1, d), lambda i: (0, 0)),
        pl.BlockSpec((None, 1, d), lambda i: (mod_idx(i), 0, sc_chunk)),
        pl.BlockSpec((None, 1, d), lambda i: (mod_idx(i), 0, sh_chunk)),
        pl.BlockSpec((d, n), lambda i: (0, 0)),
        pl.BlockSpec((ROW_TILE, LANES), row),
        pl.BlockSpec((ROW_TILE, LANES), row),
        pl.BlockSpec((ROW_TILE, LANES), row),
    ]
    out_specs = [pl.BlockSpec((ROW_TILE, n), row)]
    out_shape = [jax.ShapeDtypeStruct((na, n), out_dtype)]
    args = [x_all, g, mods, mods, w, cos, sa, sb]
    if wt is not None:
        nt = wt.shape[0]
        in_specs.append(pl.BlockSpec((nt, d), lambda i: (0, 0)))
        out_specs.append(pl.BlockSpec((nt, ROW_TILE), lambda i: (0, i)))
        out_shape.append(jax.ShapeDtypeStruct((nt, na), out_dtype))
        args.append(wt)
    outs = pl.pallas_call(
        kern,
        grid=(na // ROW_TILE,),
        in_specs=in_specs,
        out_specs=out_specs,
        out_shape=out_shape,
        compiler_params=_cparams(("parallel",)),
        name="norm_mod_matmul",
    )(*args)
    return outs[0] if wt is None else outs


def _hgrn_chunk_matrices(tt):
    r = lax.broadcasted_iota(jnp.int32, (tt, tt), 0)
    c = lax.broadcasted_iota(jnp.int32, (tt, tt), 1)
    same = (r // HGRN_CHUNK) == (c // HGRN_CHUNK)
    mats = [jnp.logical_and(same, c <= r), jnp.logical_and(same, c >= r), same]
    return jnp.stack(mats).astype(BF16)


def _hgrn_direction(q_raw, f_raw, v, lb_row, st, m_cum, m_tot, reverse):
    tt = q_raw.shape[0]
    n_chunks = tt // HGRN_CHUNK
    qs = _silu(q_raw) * (A_DK ** -0.5)
    f = lb_row + (1.0 - lb_row) * _sigmoid(f_raw)
    kk = 1.0 - f
    lf = jnp.log(f)

    hi = lf.astype(BF16)
    r1 = lf - hi.astype(F32)
    mid = r1.astype(BF16)
    lo = (r1 - mid.astype(F32)).astype(BF16)

    def split_dot(m):
        return (jnp.dot(m, hi, preferred_element_type=F32) + jnp.dot(m, mid, preferred_element_type=F32)
                + jnp.dot(m, lo, preferred_element_type=F32))

    cum = split_dot(m_cum)
    tot = split_dot(m_tot)
    q_in = qs * jnp.exp(cum)
    k_out = kk * jnp.exp(tot - cum)

    half = HGRN_CHUNK // 2
    assert half == 8

    def halves(x):
        x4 = x.reshape(n_chunks, 2, half, A_DK)
        return (x4[:, 1], x4[:, 0]) if reverse else (x4[:, 0], x4[:, 1])

    def rot(x, d):
        if d == 0:
            return x
        return pltpu.roll(x, (half - d) if reverse else d, 1)

    sub = lax.broadcasted_iota(jnp.int32, (1, half, 1), 1)
    if reverse:
        sub = half - 1 - sub

    def contrib(q_t, c_t, k_s, c_s, v_s, valid):
        diff = c_t - c_s
        if valid is not None:
            diff = jnp.where(valid, diff, NEG_INF)
        return jnp.sum(q_t * k_s * jnp.exp(diff), axis=-1, keepdims=True) * v_s

    (q_n, q_f), (k_n, k_f), (c_n, c_f), (v_n, v_f) = halves(qs), halves(kk), halves(cum), halves(v)
    o_n = jnp.sum(q_n * k_n, axis=-1, keepdims=True) * v_n
    o_f = jnp.sum(q_f * k_f, axis=-1, keepdims=True) * v_f
    for dist in range(1, half):
        ok = sub >= dist
        kn, cn, vn = rot(k_n, dist), rot(c_n, dist), rot(v_n, dist)
        kf, cf, vf = rot(k_f, dist), rot(c_f, dist), rot(v_f, dist)
        o_n = o_n + contrib(q_n, c_n, kn, cn, vn, ok)
        o_f = o_f + contrib(q_f, c_f, jnp.where(ok, kf, kn), jnp.where(ok, cf, cn), jnp.where(ok, vf, vn), None)
    for dist in range(half, HGRN_CHUNK):
        e = dist - half
        o_f = o_f + contrib(q_f, c_f, rot(k_n, e), rot(c_n, e), rot(v_n, e), sub >= e)
    o = jnp.stack([o_f, o_n] if reverse else [o_n, o_f], axis=1).reshape(tt, A_DK)

    v_t = v.T.astype(BF16)
    q_in_t = q_in.T.astype(BF16)
    k_out_b = k_out.astype(BF16)
    lane_chunk = lax.broadcasted_iota(jnp.int32, (1, tt), 1) // HGRN_CHUNK
    updates = [jnp.dot(v_t * (lane_chunk == ci).astype(BF16), k_out_b, preferred_element_type=F32)
               for ci in range(n_chunks)]
    o_t = jnp.zeros((A_DK, tt), F32)
    order = range(n_chunks - 1, -1, -1) if reverse else range(n_chunks)
    for ci in order:
        lo_r = ci * HGRN_CHUNK
        from_state = jnp.dot(st.astype(BF16), q_in_t, preferred_element_type=F32)
        o_t = jnp.where(lane_chunk == ci, from_state, o_t)
        st = st * jnp.exp(tot[lo_r:lo_r + 1]) + updates[ci]
    return o + o_t.T, st


def _hgrn_kernel(qf_ref, ff_ref, vf_ref, qb_ref, fb_ref, vb_ref, lb_ref, mats_ref, of_ref, ob_ref, st_scr):
    @pl.when(pl.program_id(2) == 0)
    def _():
        st_scr[...] = jnp.zeros_like(st_scr)

    for hh in range(HGRN_HEADS_PER_STEP):
        sl = slice(hh * A_DK, (hh + 1) * A_DK)
        o_f, st_f = _hgrn_direction(qf_ref[:, sl], ff_ref[:, sl], vf_ref[:, sl], lb_ref[0:1, sl],
                                    st_scr[2 * hh], mats_ref[0], mats_ref[2], False)
        of_ref[:, sl] = o_f
        st_scr[2 * hh] = st_f
        o_b, st_b = _hgrn_direction(qb_ref[:, sl], fb_ref[:, sl], vb_ref[:, sl], lb_ref[1:2, sl],
                                    st_scr[2 * hh + 1], mats_ref[1], mats_ref[2], True)
        ob_ref[:, sl] = o_b
        st_scr[2 * hh + 1] = st_b


def hgrn_scan(p_a, lb, batch, t_len, c_len):
    na = p_a.shape[0]
    tt = HGRN_TILE
    n_c, n_t = c_len // tt, t_len // tt
    off_c = batch * n_t

    def rb_f(b, s):
        return jnp.where(s < n_c, off_c + b * n_c + s, b * n_t + (s - n_c))

    def rb_b(b, s):
        return jnp.where(s < n_c, off_c + b * n_c + (n_c - 1 - s), b * n_t + (n_t - 1 - (s - n_c)))

    hps = HGRN_HEADS_PER_STEP
    n_hb = A_HEADS // hps

    def spec(rb, group):
        return pl.BlockSpec((tt, hps * A_DK), lambda b, h, s: (rb(b, s), group * n_hb + h))

    out_sds = jax.ShapeDtypeStruct((na, A_WIDTH), F32)
    return pl.pallas_call(
        _hgrn_kernel,
        grid=(batch, n_hb, n_c + n_t),
        in_specs=[spec(rb_f, 0), spec(rb_f, 1), spec(rb_f, 3),
                  spec(rb_b, 0), spec(rb_b, 2), spec(rb_b, 3),
                  pl.BlockSpec((2, hps * A_DK), lambda b, h, s: (0, h)),
                  pl.BlockSpec((3, tt, tt), lambda b, h, s: (0, 0, 0))],
        out_specs=[spec(rb_f, 0), spec(rb_b, 0)],
        out_shape=[out_sds, out_sds],
        scratch_shapes=[pltpu.VMEM((2 * hps, A_DK, A_DK), F32)],
        compiler_params=_cparams(("parallel", "parallel", "arbitrary")),
        name="hgrn_scan",
    )(p_a, p_a, p_a, p_a, p_a, p_a, lb, _hgrn_chunk_matrices(tt))


def _hgrn_finish_kernel(of_ref, ob_ref, g_ref, gn_ref, o_ref):
    o = of_ref[...] + ob_ref[...]
    gate = _silu(g_ref[...])
    for h in range(A_HEADS):
        sl = slice(h * A_DK, (h + 1) * A_DK)
        oh = o[:, sl]
        y = oh * lax.rsqrt(jnp.mean(oh * oh, axis=-1, keepdims=True) + EPS) * gn_ref[...]
        o_ref[:, sl] = (y * gate[:, sl]).astype(o_ref.dtype)


def hgrn_finish(o_f, o_b, p_a, onorm_g):
    na = o_f.shape[0]
    blk = pl.BlockSpec((ROW_TILE, A_WIDTH), lambda i: (i, 0))
    return pl.pallas_call(
        _hgrn_finish_kernel,
        grid=(na // ROW_TILE,),
        in_specs=[blk, blk, pl.BlockSpec((ROW_TILE, A_WIDTH), lambda i: (i, 4)),
                  pl.BlockSpec((1, A_DK), lambda i: (0, 0))],
        out_specs=blk,
        out_shape=jax.ShapeDtypeStruct((na, A_WIDTH), BF16),
        compiler_params=_cparams(("parallel",)),
        name="hgrn_finish",
    )(o_f, o_b, p_a, onorm_g)


def _diff_attn_kernel(q_ref, kc_ref, vtc_ref, k_ref, vt_ref, lam_ref, gn_ref, o_ref,
                      m_scr, l_scr, acc_scr, *, n_k, unroll, lambda_init):
    q_t = (q_ref[...].astype(F32) * (B_HALF ** -0.5 * LOG2_E)).T.astype(BF16)
    tq = q_t.shape[1]
    sub = lax.broadcasted_iota(jnp.int32, (B_DV, 1), 0)
    zero = jnp.zeros_like(q_t)
    qm = jnp.concatenate([jnp.where(sub < B_HALF, q_t, zero), jnp.where(sub >= B_HALF, q_t, zero)], axis=1)
    nc = 2 * tq

    m_scr[...] = jnp.full_like(m_scr, NEG_INF)
    l_scr[...] = jnp.zeros_like(l_scr)
    acc_scr[...] = jnp.zeros_like(acc_scr)

    def fold8(s, op):
        return op(s.reshape(s.shape[0] // 8, 8, nc), axis=0)

    def update(kvs):
        ss = [jnp.dot(kb, qm, preferred_element_type=F32) for kb, _ in kvs]
        part = functools.reduce(jnp.maximum, [fold8(s, jnp.max) for s in ss])
        m_old = m_scr[0:1, :]
        m_new = jnp.maximum(m_old, jnp.max(part, axis=0, keepdims=True))
        alpha = jnp.exp2(m_old - m_new)
        lsum = alpha * l_scr[...]
        acc = alpha * acc_scr[...]
        for s, (_, vtb) in zip(ss, kvs):
            p = jnp.exp2(s - m_new)
            lsum = lsum + fold8(p, jnp.sum)
            acc = acc + jnp.dot(vtb, p.astype(BF16), preferred_element_type=F32)
        l_scr[...] = lsum
        acc_scr[...] = acc
        m_scr[...] = jnp.broadcast_to(m_new, (8, nc))

    update([(kc_ref[...], vtc_ref[...])])
    for j in range(n_k):
        kvs = []
        for u in range(unroll):
            start = (j * unroll + u) * DIFF_TK
            kvs.append((k_ref[start:start + DIFF_TK, :], vt_ref[:, start:start + DIFF_TK]))
        update(kvs)

    lp = lam_ref[...]
    lam = (jnp.exp(jnp.sum(lp[0:1] * lp[1:2], axis=-1, keepdims=True))
           - jnp.exp(jnp.sum(lp[2:3] * lp[3:4], axis=-1, keepdims=True)) + lambda_init)
    on = acc_scr[...] / jnp.sum(l_scr[...], axis=0, keepdims=True)
    o = on[:, :tq] - lam * on[:, tq:]
    y = o * lax.rsqrt(jnp.mean(o * o, axis=0, keepdims=True) + EPS) * gn_ref[...]
    o_ref[...] = (y * (1.0 - lambda_init)).T.astype(o_ref.dtype)


def diff_attention(p_qk, v_t, lam_p, subln_g, lambda_init, batch, t_len, c_len):
    off_c = batch * (t_len // c_len)
    unroll = min(DIFF_UNROLL, t_len // DIFF_TK)
    assert t_len % (unroll * DIFF_TK) == 0 and t_len % DIFF_TQ == 0

    def call(tq, n_q, q_off, n_k):
        kern = functools.partial(_diff_attn_kernel, n_k=n_k, unroll=unroll, lambda_init=lambda_init)
        in_specs = [
            pl.BlockSpec((tq, B_DV), lambda b, h, i: (q_off + b * n_q + i, h)),
            pl.BlockSpec((c_len, B_DV), lambda b, h, i: (off_c + b, B_HEADS + h)),
            pl.BlockSpec((B_DV, c_len), lambda b, h, i: (h, off_c + b)),
            pl.BlockSpec((t_len, B_DV), lambda b, h, i: (b, B_HEADS + h)),
            pl.BlockSpec((B_DV, t_len), lambda b, h, i: (h, b)),
            pl.BlockSpec((4, B_HALF), lambda b, h, i: (0, 0)),
            pl.BlockSpec((B_DV, 1), lambda b, h, i: (0, 0)),
        ]
        args = [p_qk, p_qk, v_t, p_qk, v_t, lam_p, subln_g.reshape(B_DV, 1)]
        return pl.pallas_call(
            kern,
            grid=(batch, B_HEADS, n_q),
            in_specs=in_specs,
            out_specs=pl.BlockSpec((tq, B_DV), lambda b, h, i: (b * n_q + i, h)),
            out_shape=jax.ShapeDtypeStruct((batch * n_q * tq, B_WIDTH), BF16),
            scratch_shapes=[pltpu.VMEM((8, 2 * tq), F32), pltpu.VMEM((8, 2 * tq), F32),
                            pltpu.VMEM((B_DV, 2 * tq), F32)],
            compiler_params=_cparams(("parallel", "parallel", "arbitrary")),
            name="diff_attention",
        )(*args)

    y_lat = call(DIFF_TQ, t_len // DIFF_TQ, 0, t_len // (unroll * DIFF_TK))
    y_ctx = call(c_len, 1, off_c, 0)
    return y_lat, y_ctx


def _swa_kernel(q_ref, kc_ref, vtc_ref, k_ref, *rest, t_len, n_vt):
    vt_refs, (sink_ref, o_ref) = rest[:n_vt], rest[n_vt:]
    i = pl.program_id(1)
    tq = q_ref.shape[0]
    kvw = C_KV_HEADS * C_HEAD_DIM
    win = tq + 2 * C_WINDOW
    a = i * tq
    start = pl.multiple_of(jnp.clip(a - C_WINDOW, 0, t_len - win), C_WINDOW)
    kw = k_ref[pl.ds(start, win), :]
    vtw = jnp.concatenate([r[...] for r in vt_refs], axis=1)
    kc, vtc = kc_ref[...], vtc_ref[...]
    kpos = start + lax.broadcasted_iota(jnp.int32, (win, 1), 0)
    qpos = a + lax.broadcasted_iota(jnp.int32, (1, tq), 1)
    valid = jnp.abs(kpos - qpos) <= C_WINDOW
    q_t = (q_ref[...].astype(F32) * (C_HEAD_DIM ** -0.5)).T.astype(BF16)
    valid_g = jnp.concatenate([valid] * C_GROUP, axis=1)
    for g in range(C_KV_HEADS):
        lo, hi = g * C_HEAD_DIM, (g + 1) * C_HEAD_DIM
        heads = range(g * C_GROUP, (g + 1) * C_GROUP)
        q_g = jnp.concatenate([q_t[h * C_HEAD_DIM:(h + 1) * C_HEAD_DIM] for h in heads], axis=1)
        pieces = []
        if lo:
            pieces.append(jnp.zeros((lo, C_GROUP * tq), BF16))
        pieces.append(q_g)
        if kvw - hi:
            pieces.append(jnp.zeros((kvw - hi, C_GROUP * tq), BF16))
        qm = jnp.concatenate(pieces, axis=0)
        sink = jnp.concatenate([jnp.broadcast_to(sink_ref[h:h + 1, :], (1, tq)) for h in heads], axis=1)
        s_c = jnp.dot(kc, qm, preferred_element_type=F32)
        s_w = jnp.where(valid_g, jnp.dot(kw, qm, preferred_element_type=F32), NEG_INF)
        m = jnp.maximum(jnp.maximum(jnp.max(s_c, axis=0, keepdims=True),
                                    jnp.max(s_w, axis=0, keepdims=True)), sink)
        p_c = jnp.exp(s_c - m)
        p_w = jnp.exp(s_w - m)
        den = (jnp.exp(sink - m) + jnp.sum(p_c, axis=0, keepdims=True)
               + jnp.sum(p_w, axis=0, keepdims=True))
        o_t = (jnp.dot(vtc[lo:hi], p_c.astype(BF16), preferred_element_type=F32)
               + jnp.dot(vtw[lo:hi], p_w.astype(BF16), preferred_element_type=F32)) / den
        per_slab = LANES // C_HEAD_DIM
        for s in range(C_GROUP // per_slab):
            slab = jnp.concatenate([o_t[:, (s * per_slab + j) * tq:(s * per_slab + j + 1) * tq]
                                    for j in range(per_slab)], axis=0)
            c0 = (g * C_GROUP + s * per_slab) * C_HEAD_DIM
            o_ref[:, c0:c0 + LANES] = slab.T.astype(o_ref.dtype)


def window_attention(p_qk, v_t, sink, batch, t_len, c_len):
    tq = ATTN_TQ
    qw = C_HEADS * C_HEAD_DIM
    kvw = C_KV_HEADS * C_HEAD_DIM
    assert qw % kvw == 0
    k_col = qw // kvw
    n_q = t_len // tq
    off_c = batch * (t_len // c_len)
    win = tq + 2 * C_WINDOW
    n_vt = win // C_WINDOW
    per_b = t_len // C_WINDOW

    def vt_blk(j):
        def index(b, i):
            first = jnp.clip(i * (tq // C_WINDOW) - 1, 0, per_b - n_vt)
            return (0, b * per_b + first + j)
        return pl.BlockSpec((kvw, C_WINDOW), index)

    kern = functools.partial(_swa_kernel, t_len=t_len, n_vt=n_vt)
    return pl.pallas_call(
        kern,
        grid=(batch, n_q),
        in_specs=[
            pl.BlockSpec((tq, qw), lambda b, i: (b * n_q + i, 0)),
            pl.BlockSpec((c_len, kvw), lambda b, i: (off_c + b, k_col)),
            pl.BlockSpec((kvw, c_len), lambda b, i: (0, off_c + b)),
            pl.BlockSpec((t_len, kvw), lambda b, i: (b, k_col)),
            *[vt_blk(j) for j in range(n_vt)],
            pl.BlockSpec((C_HEADS, 1), lambda b, i: (0, 0)),
        ],
        out_specs=pl.BlockSpec((tq, qw), lambda b, i: (b * n_q + i, 0)),
        out_shape=jax.ShapeDtypeStruct((batch * t_len, qw), BF16),
        compiler_params=_cparams(("parallel", "arbitrary")),
        name="window_attention",
    )(p_qk, p_qk, v_t, p_qk, *([v_t] * n_vt), sink.astype(F32).reshape(C_HEADS, 1))


def _to_row_tiles(ref, val):
    for s in range(val.shape[1] // LANES):
        ref[:, s, :] = val[:, s * LANES:(s + 1) * LANES]


def _from_row_tiles(ref):
    return jnp.concatenate([ref[:, s, :] for s in range(ref.shape[1])], axis=1)


def _out_proj_kernel(ya_ref, yb_ref, wa_ref, wb_ref, x_ref, g1_ref, g_ref, sc_ref, sh_ref, rw_ref,
                     *rest, n_main):
    if len(rest) == 4:
        yt_ref, xo_ref, h_ref, lg_ref = rest
        yb = jnp.where(pl.program_id(0) < n_main, yb_ref[...], yt_ref[...])
    else:
        xo_ref, h_ref, lg_ref = rest
        yb = yb_ref[...]
    y = (jnp.dot(ya_ref[...], wa_ref[...], preferred_element_type=F32)
         + jnp.dot(yb, wb_ref[...], preferred_element_type=F32))
    xn = x_ref[...] + g1_ref[...] * y
    xo_ref[...] = xn
    hn = xn * lax.rsqrt(jnp.mean(xn * xn, axis=-1, keepdims=True) + EPS) * g_ref[...]
    h2 = hn * (1.0 + sc_ref[...]) + sh_ref[...]
    _to_row_tiles(h_ref, h2)
    lg_ref[...] = lax.dot_general(rw_ref[...], h2, (((1,), (1,)), ((), ())),
                                  preferred_element_type=F32, precision=HIGHEST)


def out_proj(ya, yb_arr, yb_col, w_out, x_all, mods, g2n, router_wt, rows, mod_idx, yb_tail=None):
    d = x_all.shape[1]
    half = d // 2
    ne = router_wt.shape[0]
    row = lambda i: (i, 0)
    n_main = yb_arr.shape[0] // ROW_TILE
    in_specs = [
        pl.BlockSpec((ROW_TILE, half), row),
        pl.BlockSpec((ROW_TILE, half), lambda i: (jnp.minimum(i, n_main - 1), yb_col)),
        pl.BlockSpec((half, d), lambda i: (0, 0)),
        pl.BlockSpec((half, d), lambda i: (1, 0)),
        pl.BlockSpec((ROW_TILE, d), row),
        pl.BlockSpec((None, 1, d), lambda i: (mod_idx(i), 0, 2)),
        pl.BlockSpec((1, d), lambda i: (0, 0)),
        pl.BlockSpec((None, 1, d), lambda i: (mod_idx(i), 0, 4)),
        pl.BlockSpec((None, 1, d), lambda i: (mod_idx(i), 0, 3)),
        pl.BlockSpec((ne, d), lambda i: (0, 0)),
    ]
    args = [ya, yb_arr, w_out, w_out, x_all, mods, g2n, mods, mods, router_wt]
    if yb_tail is not None:
        in_specs.append(pl.BlockSpec((ROW_TILE, half), lambda i: (jnp.maximum(i - n_main, 0), 0)))
        args.append(yb_tail)
    return pl.pallas_call(
        functools.partial(_out_proj_kernel, n_main=n_main),
        grid=(rows // ROW_TILE,),
        in_specs=in_specs,
        out_specs=[pl.BlockSpec((ROW_TILE, d), row),
                   pl.BlockSpec((ROW_TILE, d // LANES, LANES), lambda i: (i, 0, 0)),
                   pl.BlockSpec((ne, ROW_TILE), lambda i: (0, i))],
        out_shape=[jax.ShapeDtypeStruct((rows, d), F32),
                   jax.ShapeDtypeStruct((rows, d // LANES, LANES), F32),
                   jax.ShapeDtypeStruct((ne, rows), F32)],
        compiler_params=_cparams(("parallel",)),
        name="out_proj",
    )(*args)


def _router_kernel(lg_ref, rb_ref, ids_ref, w_ref, cnt_ref, carry):
    @pl.when(pl.program_id(0) == 0)
    def _():
        carry[...] = jnp.zeros_like(carry)

    tr = lg_ref.shape[1]
    aff = jax.nn.sigmoid(lg_ref[...])
    sel = aff + rb_ref[...]
    rows = [sel[e:e + 1] for e in range(N_EXPERTS)]

    def beats(a, b, a_first):
        return jnp.logical_or(a > b, jnp.logical_and(a == b, a_first))

    def rank_among(vals, j):
        r = jnp.zeros(vals[0].shape, jnp.int32)
        for i2 in range(len(vals)):
            if i2 != j:
                r = r + beats(vals[i2], vals[j], i2 < j).astype(jnp.int32)
        return r

    rank, gscore = [], []
    for g in range(N_GROUPS):
        grp_rows = rows[g * EXPERTS_PER_GROUP:(g + 1) * EXPERTS_PER_GROUP]
        grp_rank = [rank_among(grp_rows, j) for j in range(EXPERTS_PER_GROUP)]
        rank += grp_rank
        gscore.append(sum(jnp.where(grp_rank[j] < TOP_K, grp_rows[j], 0.0) for j in range(EXPERTS_PER_GROUP)))
    chosen = [rank_among(gscore, g) == 0 for g in range(N_GROUPS)]
    onehot = []
    for k in range(TOP_K):
        hot_rows = [jnp.logical_and(chosen[e // EXPERTS_PER_GROUP], rank[e] == k).astype(F32)
                    for e in range(N_EXPERTS)]
        onehot.append(jnp.concatenate(hot_rows, axis=0))

    e_idx = lax.broadcasted_iota(jnp.int32, (N_EXPERTS, 1), 0).astype(F32)
    picked = [jnp.sum(onehot[k] * aff, axis=0, keepdims=True) for k in range(TOP_K)]
    denom = picked[0] + picked[1]
    before = (lax.broadcasted_iota(jnp.int32, (tr, tr), 0)
              < lax.broadcasted_iota(jnp.int32, (tr, tr), 1)).astype(BF16)
    both = onehot[0] + onehot[1]
    seen = carry[:, 0:1] + jnp.dot(both.astype(BF16), before, preferred_element_type=F32)
    id_rows = [jnp.sum(onehot[k] * e_idx, axis=0, keepdims=True) for k in range(TOP_K)]
    id_rows += [jnp.sum(onehot[k] * seen, axis=0, keepdims=True) for k in range(TOP_K)]
    pad = jnp.zeros((8 - 2 * TOP_K, tr), F32)
    ids_ref[...] = jnp.concatenate(id_rows + [pad], axis=0).astype(jnp.int32)
    w_ref[...] = jnp.concatenate([picked[0] / denom, picked[1] / denom,
                                  jnp.zeros((8 - TOP_K, tr), F32)], axis=0)
    carry[...] = carry[...] + jnp.sum(both, axis=1, keepdims=True)
    cnt_ref[...] = carry[...]


def moe_router(logits_t, router_b):
    ne, rows = logits_t.shape
    tr = ROW_TILE
    return pl.pallas_call(
        _router_kernel,
        grid=(rows // tr,),
        in_specs=[pl.BlockSpec((ne, tr), lambda i: (0, i)), pl.BlockSpec((ne, 1), lambda i: (0, 0))],
        out_specs=[pl.BlockSpec((8, tr), lambda i: (0, i)), pl.BlockSpec((8, tr), lambda i: (0, i)),
                   pl.BlockSpec((ne, LANES), lambda i: (0, 0))],
        out_shape=[jax.ShapeDtypeStruct((8, rows), jnp.int32), jax.ShapeDtypeStruct((8, rows), F32),
                   jax.ShapeDtypeStruct((ne, LANES), F32)],
        scratch_shapes=[pltpu.VMEM((ne, LANES), F32)],
        compiler_params=_cparams(("arbitrary",)),
        name="moe_router",
    )(logits_t, router_b.astype(F32).reshape(ne, 1))


def _tile_copy(src, dst, sem):
    return pltpu.make_async_copy(src, dst, sem)


def _dispatch_kernel(dest_ref, h_ref, zero_hbm, xs_hbm, sem):
    del zero_hbm
    n_tok = h_ref.shape[0]

    def start(t, carry):
        for k in range(TOP_K):
            _tile_copy(h_ref.at[t], xs_hbm.at[dest_ref[0, 0, TOP_K * t + k]], sem).start(priority=k)
        return carry

    lax.fori_loop(0, n_tok, start, 0, unroll=DMA_UNROLL)
    for k in range(TOP_K):
        _tile_copy(h_ref, xs_hbm.at[pl.ds(0, n_tok)], sem).wait()


def moe_dispatch(h2, dest, n_rows):
    rows, ns, nl = h2.shape
    n_t = rows // MOE_TOK
    dest3 = dest.reshape(n_t, 1, MOE_TOK * TOP_K)
    zeros = jnp.zeros((n_rows, ns, nl), F32)
    return pl.pallas_call(
        _dispatch_kernel,
        grid=(n_t,),
        in_specs=[
            pl.BlockSpec((1, 1, MOE_TOK * TOP_K), lambda i: (i, 0, 0), memory_space=pltpu.SMEM),
            pl.BlockSpec((MOE_TOK, ns, nl), lambda i: (i, 0, 0)),
            pl.BlockSpec(memory_space=pl.ANY),
        ],
        out_specs=pl.BlockSpec(memory_space=pl.ANY),
        out_shape=jax.ShapeDtypeStruct((n_rows, ns, nl), F32),
        scratch_shapes=[pltpu.SemaphoreType.DMA],
        input_output_aliases={2: 0},
        compiler_params=_cparams(("arbitrary",)),
        name="moe_dispatch",
    )(dest3, h2, zeros)


def _expert_kernel(be_ref, nv_ref, x_ref, wg_ref, wu_ref, wd_ref, o_ref, wg_s, wu_s, wd_s):
    i = pl.program_id(0)
    new_expert = jnp.logical_or(i == 0, be_ref[i] != be_ref[jnp.maximum(i - 1, 0)])

    @pl.when(new_expert)
    def _():
        wg_s[...] = wg_ref[...].astype(BF16)
        wu_s[...] = wu_ref[...].astype(BF16)
        wd_s[...] = wd_ref[...].astype(BF16)

    @pl.when(i < nv_ref[0])
    def _():
        rows = x_ref.shape[0] // MOE_SPLIT
        for part in range(MOE_SPLIT):
            xr = x_ref.at[part * rows:(part + 1) * rows]
            x = _from_row_tiles(xr).astype(BF16)
            hid = (_silu(jnp.dot(x, wg_s[...], preferred_element_type=F32))
                   * jnp.dot(x, wu_s[...], preferred_element_type=F32))
            _to_row_tiles(o_ref.at[part * rows:(part + 1) * rows],
                          jnp.dot(hid.astype(BF16), wd_s[...], preferred_element_type=F32))

    @pl.when(i >= nv_ref[0])
    def _():
        o_ref[...] = jnp.zeros_like(o_ref)


def moe_experts(x_sorted, block_e, n_valid, layer, w_gate, w_up, w_down):
    n_rows, ns, nl = x_sorted.shape
    d, de = w_gate.shape[2], w_gate.shape[3]
    n_blocks = n_rows // MOE_TILE
    grid_spec = pltpu.PrefetchScalarGridSpec(
        num_scalar_prefetch=2,
        grid=(n_blocks,),
        in_specs=[
            pl.BlockSpec((MOE_TILE, ns, nl), lambda i, be, nv: (i, 0, 0)),
            pl.BlockSpec((None, None, d, de), lambda i, be, nv: (layer, be[i], 0, 0)),
            pl.BlockSpec((None, None, d, de), lambda i, be, nv: (layer, be[i], 0, 0)),
            pl.BlockSpec((None, None, de, d), lambda i, be, nv: (layer, be[i], 0, 0)),
        ],
        out_specs=pl.BlockSpec((MOE_TILE, ns, nl), lambda i, be, nv: (i, 0, 0)),
        scratch_shapes=[pltpu.VMEM((d, de), BF16), pltpu.VMEM((d, de), BF16), pltpu.VMEM((de, d), BF16)],
    )
    return pl.pallas_call(
        _expert_kernel,
        grid_spec=grid_spec,
        out_shape=jax.ShapeDtypeStruct((n_rows, ns, nl), F32),
        compiler_params=_cparams(("arbitrary",)),
        name="moe_experts",
    )(block_e, n_valid, x_sorted, w_gate, w_up, w_down)


def _combine_kernel(dest_ref, dnext_ref, ys_hbm, w_ref, x_ref, g2_ref, gf_ref, o_ref, buf, sem, *, final_norm):
    i = pl.program_id(0)
    n_tok = buf.shape[2]

    def gather(d_ref, slot):
        def start(t, carry):
            for k in range(TOP_K):
                _tile_copy(ys_hbm.at[d_ref[0, 0, TOP_K * t + k]], buf.at[slot, k, t],
                           sem.at[slot]).start(priority=k)
            return carry

        lax.fori_loop(0, n_tok, start, 0, unroll=DMA_UNROLL)

    slot = i % 2

    @pl.when(i == 0)
    def _():
        gather(dest_ref, slot)

    @pl.when(i + 1 < pl.num_programs(0))
    def _():
        gather(dnext_ref, 1 - slot)

    for k in range(TOP_K):
        _tile_copy(ys_hbm.at[pl.ds(0, n_tok)], buf.at[slot, k], sem.at[slot]).wait()
    w = w_ref[...]
    f = w[:, 0:1] * _from_row_tiles(buf.at[slot, 0]) + w[:, 1:2] * _from_row_tiles(buf.at[slot, 1])
    xn = x_ref[...] + g2_ref[...] * f
    if final_norm:
        xn = xn * lax.rsqrt(jnp.mean(xn * xn, axis=-1, keepdims=True) + EPS) * gf_ref[...]
    o_ref[...] = xn


def moe_combine(y_sorted, dest, wts, x_new, mods, final_g, final_norm, mod_idx_tok):
    rows, d = x_new.shape
    n_t = rows // MOE_TOK
    dest3 = dest.reshape(n_t, 1, MOE_TOK * TOP_K)
    kern = functools.partial(_combine_kernel, final_norm=final_norm)
    return pl.pallas_call(
        kern,
        grid=(n_t,),
        in_specs=[
            pl.BlockSpec((1, 1, MOE_TOK * TOP_K), lambda i: (i, 0, 0), memory_space=pltpu.SMEM),
            pl.BlockSpec((1, 1, MOE_TOK * TOP_K), lambda i: (jnp.minimum(i + 1, n_t - 1), 0, 0),
                         memory_space=pltpu.SMEM),
            pl.BlockSpec(memory_space=pl.ANY),
            pl.BlockSpec((MOE_TOK, TOP_K), lambda i: (i, 0)),
            pl.BlockSpec((MOE_TOK, d), lambda i: (i, 0)),
            pl.BlockSpec((None, 1, d), lambda i: (mod_idx_tok(i), 0, 5)),
            pl.BlockSpec((1, d), lambda i: (0, 0)),
        ],
        out_specs=pl.BlockSpec((MOE_TOK, d), lambda i: (i, 0)),
        out_shape=jax.ShapeDtypeStruct((rows, d), F32),
        scratch_shapes=[pltpu.VMEM((2, TOP_K, MOE_TOK, d // LANES, LANES), F32),
                        pltpu.SemaphoreType.DMA((2,))],
        compiler_params=_cparams(("arbitrary",)),
        name="moe_combine",
    )(dest3, dest3, y_sorted, wts, x_new, mods, final_g)


def _row_plan(ids, counts, n_tok):
    counts = counts[:, 0].astype(jnp.int32)
    padded = (counts + MOE_TILE - 1) // MOE_TILE * MOE_TILE
    pad_end = jnp.cumsum(padded)
    pad_start = pad_end - padded
    experts = jnp.arange(N_EXPERTS, dtype=jnp.int32)[:, None]
    dest = [jnp.sum(jnp.where(ids[k][None, :] == experts, pad_start[:, None], 0), axis=0) + ids[TOP_K + k]
            for k in range(TOP_K)]
    dest = jnp.stack(dest, axis=1).astype(jnp.int32)
    n_blocks = -(-(n_tok * TOP_K) // MOE_TILE) + N_EXPERTS
    block_start = jnp.arange(n_blocks, dtype=jnp.int32) * MOE_TILE
    block_e = jnp.minimum(jnp.sum(pad_end[None, :] <= block_start[:, None], axis=1), N_EXPERTS - 1)
    n_valid = (pad_end[-1] // MOE_TILE).reshape(1)
    return dest, block_e.astype(jnp.int32), n_valid.astype(jnp.int32), n_blocks


def moe_layer(h2, logits_t, router_b, layer, w_gate, w_up, w_down, x_new, mods, final_g, final_norm,
              mod_idx_tok):
    n_tok = h2.shape[0]
    ids, wts8, counts = moe_router(logits_t, router_b)
    dest, block_e, n_valid, n_blocks = _row_plan(ids, counts, n_tok)
    wts = wts8[:TOP_K].T
    x_sorted = moe_dispatch(h2, dest, n_blocks * MOE_TILE)
    y_sorted = moe_experts(x_sorted, block_e, n_valid, layer, w_gate, w_up, w_down)
    return moe_combine(y_sorted, dest, wts, x_new, mods, final_g, final_norm, mod_idx_tok)


def _rope_tables(batch, t_len, c_len):
    n_rows = t_len // GRID_W
    row = jnp.repeat(jnp.arange(n_rows), GRID_W).astype(F32)
    col = jnp.tile(jnp.arange(GRID_W), n_rows).astype(F32)
    half = B_HALF // 2
    inv = 1.0 / (ROPE_BASE ** (jnp.arange(0, half, 2, dtype=F32) / half))
    ar, ac = row[:, None] * inv, col[:, None] * inv
    ang = jnp.concatenate([ar, ar, ac, ac], axis=-1)
    cos, sin = jnp.cos(ang), jnp.sin(ang)
    first = (jnp.arange(B_HALF) % 32) < 16
    sin_a = jnp.where(first, -sin, 0.0)
    sin_b = jnp.where(first, 0.0, sin)

    def full(tab, ctx_val):
        lat = jnp.tile(jnp.tile(tab, (1, LANES // B_HALF)), (batch, 1))
        ctx = jnp.full((batch * c_len, LANES), ctx_val, F32)
        return jnp.concatenate([lat, ctx], axis=0)

    return full(cos, 1.0), full(sin_a, 0.0), full(sin_b, 0.0)


def kernel(x, c, ctx, c_ctx, ada_w, ada_b, norm_mix_g, norm_ffn_g, even_w_in, even_w_out, hgrn_lb_logits,
           hgrn_onorm_g, diff_lambda, diff_subln_g, odd_w_qkv, odd_w_out, swa_sink, router_w, router_b,
           moe_w_gate, moe_w_up, moe_w_down, final_norm_g):
    batch, t_len, d = x.shape
    c_len = ctx.shape[1]
    n_lat = batch * t_len
    assert t_len % ROW_TILE == 0 and (batch * c_len) % ROW_TILE == 0 and c_len == HGRN_TILE
    assert batch < 8 and t_len % DIFF_TK == 0 and t_len % GRID_W == 0

    x_all = jnp.concatenate([x.reshape(n_lat, d), ctx.reshape(batch * c_len, d)], axis=0)
    rope = _rope_tables(batch, t_len, c_len)

    def mod_idx_for(tile):
        per_batch = t_len // tile
        return lambda i: jnp.minimum(i // per_batch, batch)

    mod_idx = mod_idx_for(ROW_TILE)
    mod_idx_tok = mod_idx_for(MOE_TOK)

    cvec = jnp.zeros((8, d), F32).at[:batch].set(c).at[batch].set(c_ctx)
    mods_all = ada_modulation(cvec, ada_w, ada_b)
    lower_bounds = jnp.cumsum(jax.nn.softmax(hgrn_lb_logits.astype(F32), axis=0), axis=0)
    router_wt = router_w.astype(F32).T

    mods = mods_all[0].reshape(8, 1, 6 * d)
    g_mix = norm_mix_g[0].reshape(1, d)
    w_in = even_w_in[0].astype(BF16)
    n_a = 5 * A_WIDTH
    p_a = norm_mod_matmul(x_all, g_mix, mods, 1, 0, w_in[:, :n_a], rope, (0, 0), F32, mod_idx)
    n_qk = n_a + 2 * B_WIDTH
    p_qk, v_t = norm_mod_matmul(x_all, g_mix, mods, 1, 0, w_in[:, n_a:n_qk], rope, (0, 2 * B_WIDTH), BF16,
                                mod_idx, wt=w_in[:, n_qk:].T)
    o_f, o_b = hgrn_scan(p_a, lower_bounds[0], batch, t_len, c_len)
    ya = hgrn_finish(o_f, o_b, p_a, hgrn_onorm_g[0].reshape(1, A_DK))
    lambda_init = 0.8 - 0.6 * math.exp(-0.3 * 0)
    yb, yb_ctx = diff_attention(p_qk, v_t, diff_lambda[0], diff_subln_g[0], lambda_init, batch, t_len, c_len)
    na = x_all.shape[0]
    x_new, h2, logits = out_proj(ya, yb, 0, even_w_out[0].astype(BF16), x_all, mods,
                                 norm_ffn_g[0].reshape(1, d), router_wt, na, mod_idx, yb_tail=yb_ctx)
    x_all = moe_layer(h2, logits, router_b, 0, moe_w_gate, moe_w_up, moe_w_down, x_new, mods,
                      final_norm_g.reshape(1, d), False, mod_idx_tok)

    mods = mods_all[1].reshape(8, 1, 6 * d)
    w_qkv = odd_w_qkv[0]
    q_cols = C_HEADS * C_HEAD_DIM
    kv_cols = C_KV_HEADS * C_HEAD_DIM

    n_qk1 = q_cols + kv_cols
    w_qkv = w_qkv.astype(BF16)
    p_qk1, v_t1 = norm_mod_matmul(x_all, norm_mix_g[1].reshape(1, d), mods, 1, 0, w_qkv[:, :n_qk1], rope,
                                  (0, n_qk1), BF16, mod_idx, tn=kv_cols, wt=w_qkv[:, n_qk1:].T)
    o1 = window_attention(p_qk1, v_t1, swa_sink[0], batch, t_len, c_len)
    x_new, h2, logits = out_proj(o1, o1, 1, odd_w_out[0].astype(BF16), x_all, mods,
                                 norm_ffn_g[1].reshape(1, d), router_wt, n_lat, mod_idx)
    out = moe_layer(h2, logits, router_b, 1, moe_w_gate, moe_w_up, moe_w_down, x_new, mods,
                    final_norm_g.reshape(1, d), True, mod_idx_tok)
    return out.reshape(batch, t_len, d)
```

```python
import functools
import math

import jax
import jax.numpy as jnp
from jax import lax
from jax.experimental import pallas as pl
from jax.experimental.pallas import tpu as pltpu

F32 = jnp.float32
BF16 = jnp.bfloat16
EPS = 1e-6
ROPE_BASE = 10000.0
GRID_W = 64

A_HEADS, A_DK = 4, 128
A_WIDTH = A_HEADS * A_DK
HGRN_CHUNK = 16
B_HEADS, B_HALF = 4, 64
B_DV = 2 * B_HALF
B_WIDTH = B_HEADS * B_DV
C_HEADS, C_KV_HEADS, C_HEAD_DIM, C_WINDOW = 16, 4, 64, 128
C_GROUP = C_HEADS // C_KV_HEADS
N_EXPERTS, N_GROUPS, TOP_K = 16, 4, 2
EXPERTS_PER_GROUP = N_EXPERTS // N_GROUPS

LANES = 128
ROW_TILE = 512
HGRN_TILE = 256
HGRN_HEADS_PER_STEP = 1
ATTN_TQ = 256
DIFF_TK = 512
DIFF_UNROLL = 4
DIFF_TQ = 512
MOE_TILE = 256
MOE_SPLIT = 1
MOE_TOK = 256
DMA_UNROLL = 8
VMEM_LIMIT = 56 * 1024 * 1024
HIGHEST = lax.Precision.HIGHEST
NEG_INF = float("-inf")
LOG2_E = math.log2(math.e)


def _cparams(sem):
    return pltpu.CompilerParams(dimension_semantics=sem, vmem_limit_bytes=VMEM_LIMIT)


def _nt_dot(a, b):
    return lax.dot_general(a, b, (((1,), (1,)), ((), ())), preferred_element_type=F32)


def _sigmoid(x):
    return 0.5 * jnp.tanh(0.5 * x) + 0.5


def _silu(x):
    return x * _sigmoid(x)


def _ada_kernel(c_ref, w_ref, b_ref, o_ref):
    s = _silu(c_ref[...])
    o_ref[...] = jnp.dot(s, w_ref[...], preferred_element_type=F32, precision=HIGHEST) + b_ref[...]


def ada_modulation(cvec, ada_w, ada_b):
    n_layers, d, n6 = ada_w.shape
    tn = 512
    return pl.pallas_call(
        _ada_kernel,
        grid=(n_layers, n6 // tn),
        in_specs=[
            pl.BlockSpec((8, d), lambda l, j: (0, 0)),
            pl.BlockSpec((None, d, tn), lambda l, j: (l, 0, j)),
            pl.BlockSpec((None, 1, tn), lambda l, j: (l, 0, j)),
        ],
        out_specs=pl.BlockSpec((None, 8, tn), lambda l, j: (l, 0, j)),
        out_shape=jax.ShapeDtypeStruct((n_layers, 8, n6), F32),
        compiler_params=_cparams(("parallel", "parallel")),
        name="ada_modulation",
    )(cvec, ada_w, ada_b.reshape(n_layers, 1, n6))


def _rope_slab(x, cos, sin_a, sin_b):
    return x * cos + pltpu.roll(x, LANES - 16, 1) * sin_a + pltpu.roll(x, 16, 1) * sin_b


def _nmm_kernel(x_ref, g_ref, sc_ref, sh_ref, w_ref, cos_ref, sa_ref, sb_ref, *rest, rope_cols, tn):
    x = x_ref[...]
    y = x * lax.rsqrt(jnp.mean(x * x, axis=-1, keepdims=True) + EPS) * g_ref[...]
    h = (y * (1.0 + sc_ref[...]) + sh_ref[...]).astype(BF16)
    if len(rest) == 3:
        wt_ref, o_ref, ot_ref = rest
        ot_ref[...] = _nt_dot(wt_ref[...], h).astype(ot_ref.dtype)
    else:
        (o_ref,) = rest
    n = w_ref.shape[1]
    for c0 in range(0, n, tn):
        acc = jnp.dot(h, w_ref[:, c0:c0 + tn], preferred_element_type=F32)
        if rope_cols[0] <= c0 < rope_cols[1]:
            cos, sa, sb = cos_ref[...], sa_ref[...], sb_ref[...]
            for c in range(tn // LANES):
                sl = slice(c * LANES, (c + 1) * LANES)
                o_ref[:, c0 + c * LANES:c0 + (c + 1) * LANES] = (
                    _rope_slab(acc[:, sl], cos, sa, sb).astype(o_ref.dtype))
        else:
            o_ref[:, c0:c0 + tn] = acc.astype(o_ref.dtype)


def norm_mod_matmul(x_all, g, mods, sc_chunk, sh_chunk, w, rope, rope_cols, out_dtype, mod_idx, tn=512,
                    wt=None):
    na, d = x_all.shape
    n = w.shape[1]
    cos, sa, sb = rope
    assert n % tn == 0 and rope_cols[0] % tn == 0 and rope_cols[1] % tn == 0
    kern = functools.partial(_nmm_kernel, rope_cols=rope_cols, tn=tn)
    row = lambda i: (i, 0)
    in_specs = [
        pl.BlockSpec((ROW_TILE, d), row),
        pl.BlockSpec((1, d), lambda i: (0, 0)),
        pl.BlockSpec((None, 1, d), lambda i: (mod_idx(i), 0, sc_chunk)),
        pl.BlockSpec((None, 1, d), lambda i: (mod_idx(i), 0, sh_chunk)),
        pl.BlockSpec((d, n), lambda i: (0, 0)),
        pl.BlockSpec((ROW_TILE, LANES), row),
        pl.BlockSpec((ROW_TILE, LANES), row),
        pl.BlockSpec((ROW_TILE, LANES), row),
    ]
    out_specs = [pl.BlockSpec((ROW_TILE, n), row)]
    out_shape = [jax.ShapeDtypeStruct((na, n), out_dtype)]
    args = [x_all, g, mods, mods, w, cos, sa, sb]
    if wt is not None:
        nt = wt.shape[0]
        in_specs.append(pl.BlockSpec((nt, d), lambda i: (0, 0)))
        out_specs.append(pl.BlockSpec((nt, ROW_TILE), lambda i: (0, i)))
        out_shape.append(jax.ShapeDtypeStruct((nt, na), out_dtype))
        args.append(wt)
    outs = pl.pallas_call(
        kern,
        grid=(na // ROW_TILE,),
        in_specs=in_specs,
        out_specs=out_specs,
        out_shape=out_shape,
        compiler_params=_cparams(("parallel",)),
        name="norm_mod_matmul",
    )(*args)
    return outs[0] if wt is None else outs


def _hgrn_chunk_matrices(tt):
    r = lax.broadcasted_iota(jnp.int32, (tt, tt), 0)
    c = lax.broadcasted_iota(jnp.int32, (tt, tt), 1)
    same = (r // HGRN_CHUNK) == (c // HGRN_CHUNK)
    mats = [jnp.logical_and(same, c <= r), jnp.logical_and(same, c >= r), same]
    return jnp.stack(mats).astype(BF16)


def _hgrn_direction(q_raw, f_raw, v, lb_row, st, m_cum, m_tot, reverse):
    tt = q_raw.shape[0]
    n_chunks = tt // HGRN_CHUNK
    qs = _silu(q_raw) * (A_DK ** -0.5)
    f = lb_row + (1.0 - lb_row) * _sigmoid(f_raw)
    kk = 1.0 - f
    lf = jnp.log(f)

    hi = lf.astype(BF16)
    r1 = lf - hi.astype(F32)
    mid = r1.astype(BF16)
    lo = (r1 - mid.astype(F32)).astype(BF16)

    def split_dot(m):
        return (jnp.dot(m, hi, preferred_element_type=F32) + jnp.dot(m, mid, preferred_element_type=F32)
                + jnp.dot(m, lo, preferred_element_type=F32))

    cum = split_dot(m_cum)
    tot = split_dot(m_tot)
    q_in = qs * jnp.exp(cum)
    k_out = kk * jnp.exp(tot - cum)

    half = HGRN_CHUNK // 2
    assert half == 8

    def halves(x):
        x4 = x.reshape(n_chunks, 2, half, A_DK)
        return (x4[:, 1], x4[:, 0]) if reverse else (x4[:, 0], x4[:, 1])

    def rot(x, d):
        if d == 0:
            return x
        return pltpu.roll(x, (half - d) if reverse else d, 1)

    sub = lax.broadcasted_iota(jnp.int32, (1, half, 1), 1)
    if reverse:
        sub = half - 1 - sub

    def contrib(q_t, c_t, k_s, c_s, v_s, valid):
        diff = c_t - c_s
        if valid is not None:
            diff = jnp.where(valid, diff, NEG_INF)
        return jnp.sum(q_t * k_s * jnp.exp(diff), axis=-1, keepdims=True) * v_s

    (q_n, q_f), (k_n, k_f), (c_n, c_f), (v_n, v_f) = halves(qs), halves(kk), halves(cum), halves(v)
    o_n = jnp.sum(q_n * k_n, axis=-1, keepdims=True) * v_n
    o_f = jnp.sum(q_f * k_f, axis=-1, keepdims=True) * v_f
    for dist in range(1, half):
        ok = sub >= dist
        kn, cn, vn = rot(k_n, dist), rot(c_n, dist), rot(v_n, dist)
        kf, cf, vf = rot(k_f, dist), rot(c_f, dist), rot(v_f, dist)
        o_n = o_n + contrib(q_n, c_n, kn, cn, vn, ok)
        o_f = o_f + contrib(q_f, c_f, jnp.where(ok, kf, kn), jnp.where(ok, cf, cn), jnp.where(ok, vf, vn), None)
    for dist in range(half, HGRN_CHUNK):
        e = dist - half
        o_f = o_f + contrib(q_f, c_f, rot(k_n, e), rot(c_n, e), rot(v_n, e), sub >= e)
    o = jnp.stack([o_f, o_n] if reverse else [o_n, o_f], axis=1).reshape(tt, A_DK)

    v_t = v.T.astype(BF16)
    q_in_t = q_in.T.astype(BF16)
    k_out_b = k_out.astype(BF16)
    lane_chunk = lax.broadcasted_iota(jnp.int32, (1, tt), 1) // HGRN_CHUNK
    updates = [jnp.dot(v_t * (lane_chunk == ci).astype(BF16), k_out_b, preferred_element_type=F32)
               for ci in range(n_chunks)]
    o_t = jnp.zeros((A_DK, tt), F32)
    order = range(n_chunks - 1, -1, -1) if reverse else range(n_chunks)
    for ci in order:
        lo_r = ci * HGRN_CHUNK
        from_state = jnp.dot(st.astype(BF16), q_in_t, preferred_element_type=F32)
        o_t = jnp.where(lane_chunk == ci, from_state, o_t)
        st = st * jnp.exp(tot[lo_r:lo_r + 1]) + updates[ci]
    return o + o_t.T, st


def _hgrn_kernel(qf_ref, ff_ref, vf_ref, qb_ref, fb_ref, vb_ref, lb_ref, mats_ref, of_ref, ob_ref, st_scr):
    @pl.when(pl.program_id(2) == 0)
    def _():
        st_scr[...] = jnp.zeros_like(st_scr)

    for hh in range(HGRN_HEADS_PER_STEP):
        sl = slice(hh * A_DK, (hh + 1) * A_DK)
        o_f, st_f = _hgrn_direction(qf_ref[:, sl], ff_ref[:, sl], vf_ref[:, sl], lb_ref[0:1, sl],
                                    st_scr[2 * hh], mats_ref[0], mats_ref[2], False)
        of_ref[:, sl] = o_f
        st_scr[2 * hh] = st_f
        o_b, st_b = _hgrn_direction(qb_ref[:, sl], fb_ref[:, sl], vb_ref[:, sl], lb_ref[1:2, sl],
                                    st_scr[2 * hh + 1], mats_ref[1], mats_ref[2], True)
        ob_ref[:, sl] = o_b
        st_scr[2 * hh + 1] = st_b


def hgrn_scan(p_a, lb, batch, t_len, c_len):
    na = p_a.shape[0]
    tt = HGRN_TILE
    n_c, n_t = c_len // tt, t_len // tt
    off_c = batch * n_t

    def rb_f(b, s):
        return jnp.where(s < n_c, off_c + b * n_c + s, b * n_t + (s - n_c))

    def rb_b(b, s):
        return jnp.where(s < n_c, off_c + b * n_c + (n_c - 1 - s), b * n_t + (n_t - 1 - (s - n_c)))

    hps = HGRN_HEADS_PER_STEP
    n_hb = A_HEADS // hps

    def spec(rb, group):
        return pl.BlockSpec((tt, hps * A_DK), lambda b, h, s: (rb(b, s), group * n_hb + h))

    out_sds = jax.ShapeDtypeStruct((na, A_WIDTH), F32)
    return pl.pallas_call(
        _hgrn_kernel,
        grid=(batch, n_hb, n_c + n_t),
        in_specs=[spec(rb_f, 0), spec(rb_f, 1), spec(rb_f, 3),
                  spec(rb_b, 0), spec(rb_b, 2), spec(rb_b, 3),
                  pl.BlockSpec((2, hps * A_DK), lambda b, h, s: (0, h)),
                  pl.BlockSpec((3, tt, tt), lambda b, h, s: (0, 0, 0))],
        out_specs=[spec(rb_f, 0), spec(rb_b, 0)],
        out_shape=[out_sds, out_sds],
        scratch_shapes=[pltpu.VMEM((2 * hps, A_DK, A_DK), F32)],
        compiler_params=_cparams(("parallel", "parallel", "arbitrary")),
        name="hgrn_scan",
    )(p_a, p_a, p_a, p_a, p_a, p_a, lb, _hgrn_chunk_matrices(tt))


def _hgrn_finish_kernel(of_ref, ob_ref, g_ref, gn_ref, o_ref):
    o = of_ref[...] + ob_ref[...]
    gate = _silu(g_ref[...])
    for h in range(A_HEADS):
        sl = slice(h * A_DK, (h + 1) * A_DK)
        oh = o[:, sl]
        y = oh * lax.rsqrt(jnp.mean(oh * oh, axis=-1, keepdims=True) + EPS) * gn_ref[...]
        o_ref[:, sl] = (y * gate[:, sl]).astype(o_ref.dtype)


def hgrn_finish(o_f, o_b, p_a, onorm_g):
    na = o_f.shape[0]
    blk = pl.BlockSpec((ROW_TILE, A_WIDTH), lambda i: (i, 0))
    return pl.pallas_call(
        _hgrn_finish_kernel,
        grid=(na // ROW_TILE,),
        in_specs=[blk, blk, pl.BlockSpec((ROW_TILE, A_WIDTH), lambda i: (i, 4)),
                  pl.BlockSpec((1, A_DK), lambda i: (0, 0))],
        out_specs=blk,
        out_shape=jax.ShapeDtypeStruct((na, A_WIDTH), BF16),
        compiler_params=_cparams(("parallel",)),
        name="hgrn_finish",
    )(o_f, o_b, p_a, onorm_g)


def _diff_attn_kernel(q_ref, kc_ref, vtc_ref, k_ref, vt_ref, lam_ref, gn_ref, o_ref,
                      m_scr, l_scr, acc_scr, *, n_k, unroll, lambda_init):
    q_t = (q_ref[...].astype(F32) * (B_HALF ** -0.5 * LOG2_E)).T.astype(BF16)
    tq = q_t.shape[1]
    sub = lax.broadcasted_iota(jnp.int32, (B_DV, 1), 0)
    zero = jnp.zeros_like(q_t)
    qm = jnp.concatenate([jnp.where(sub < B_HALF, q_t, zero), jnp.where(sub >= B_HALF, q_t, zero)], axis=1)
    nc = 2 * tq

    m_scr[...] = jnp.full_like(m_scr, NEG_INF)
    l_scr[...] = jnp.zeros_like(l_scr)
    acc_scr[...] = jnp.zeros_like(acc_scr)

    def fold8(s, op):
        return op(s.reshape(s.shape[0] // 8, 8, nc), axis=0)

    def update(kvs):
        ss = [jnp.dot(kb, qm, preferred_element_type=F32) for kb, _ in kvs]
        part = functools.reduce(jnp.maximum, [fold8(s, jnp.max) for s in ss])
        m_old = m_scr[0:1, :]
        m_new = jnp.maximum(m_old, jnp.max(part, axis=0, keepdims=True))
        alpha = jnp.exp2(m_old - m_new)
        lsum = alpha * l_scr[...]
        acc = alpha * acc_scr[...]
        for s, (_, vtb) in zip(ss, kvs):
            p = jnp.exp2(s - m_new)
            lsum = lsum + fold8(p, jnp.sum)
            acc = acc + jnp.dot(vtb, p.astype(BF16), preferred_element_type=F32)
        l_scr[...] = lsum
        acc_scr[...] = acc
        m_scr[...] = jnp.broadcast_to(m_new, (8, nc))

    update([(kc_ref[...], vtc_ref[...])])
    for j in range(n_k):
        kvs = []
        for u in range(unroll):
            start = (j * unroll + u) * DIFF_TK
            kvs.append((k_ref[start:start + DIFF_TK, :], vt_ref[:, start:start + DIFF_TK]))
        update(kvs)

    lp = lam_ref[...]
    lam = (jnp.exp(jnp.sum(lp[0:1] * lp[1:2], axis=-1, keepdims=True))
           - jnp.exp(jnp.sum(lp[2:3] * lp[3:4], axis=-1, keepdims=True)) + lambda_init)
    on = acc_scr[...] / jnp.sum(l_scr[...], axis=0, keepdims=True)
    o = on[:, :tq] - lam * on[:, tq:]
    y = o * lax.rsqrt(jnp.mean(o * o, axis=0, keepdims=True) + EPS) * gn_ref[...]
    o_ref[...] = (y * (1.0 - lambda_init)).T.astype(o_ref.dtype)


def diff_attention(p_qk, v_t, lam_p, subln_g, lambda_init, batch, t_len, c_len):
    off_c = batch * (t_len // c_len)
    unroll = min(DIFF_UNROLL, t_len // DIFF_TK)
    assert t_len % (unroll * DIFF_TK) == 0 and t_len % DIFF_TQ == 0

    def call(tq, n_q, q_off, n_k):
        kern = functools.partial(_diff_attn_kernel, n_k=n_k, unroll=unroll, lambda_init=lambda_init)
        in_specs = [
            pl.BlockSpec((tq, B_DV), lambda b, h, i: (q_off + b * n_q + i, h)),
            pl.BlockSpec((c_len, B_DV), lambda b, h, i: (off_c + b, B_HEADS + h)),
            pl.BlockSpec((B_DV, c_len), lambda b, h, i: (h, off_c + b)),
            pl.BlockSpec((t_len, B_DV), lambda b, h, i: (b, B_HEADS + h)),
            pl.BlockSpec((B_DV, t_len), lambda b, h, i: (h, b)),
            pl.BlockSpec((4, B_HALF), lambda b, h, i: (0, 0)),
            pl.BlockSpec((B_DV, 1), lambda b, h, i: (0, 0)),
        ]
        args = [p_qk, p_qk, v_t, p_qk, v_t, lam_p, subln_g.reshape(B_DV, 1)]
        return pl.pallas_call(
            kern,
            grid=(batch, B_HEADS, n_q),
            in_specs=in_specs,
            out_specs=pl.BlockSpec((tq, B_DV), lambda b, h, i: (b * n_q + i, h)),
            out_shape=jax.ShapeDtypeStruct((batch * n_q * tq, B_WIDTH), BF16),
            scratch_shapes=[pltpu.VMEM((8, 2 * tq), F32), pltpu.VMEM((8, 2 * tq), F32),
                            pltpu.VMEM((B_DV, 2 * tq), F32)],
            compiler_params=_cparams(("parallel", "parallel", "arbitrary")),
            name="diff_attention",
        )(*args)

    y_lat = call(DIFF_TQ, t_len // DIFF_TQ, 0, t_len // (unroll * DIFF_TK))
    y_ctx = call(c_len, 1, off_c, 0)
    return y_lat, y_ctx


def _swa_kernel(q_ref, kc_ref, vtc_ref, k_ref, *rest, t_len, n_vt):
    vt_refs, (sink_ref, o_ref) = rest[:n_vt], rest[n_vt:]
    i = pl.program_id(1)
    tq = q_ref.shape[0]
    kvw = C_KV_HEADS * C_HEAD_DIM
    win = tq + 2 * C_WINDOW
    a = i * tq
    start = pl.multiple_of(jnp.clip(a - C_WINDOW, 0, t_len - win), C_WINDOW)
    kw = k_ref[pl.ds(start, win), :]
    vtw = jnp.concatenate([r[...] for r in vt_refs], axis=1)
    kc, vtc = kc_ref[...], vtc_ref[...]
    kpos = start + lax.broadcasted_iota(jnp.int32, (win, 1), 0)
    qpos = a + lax.broadcasted_iota(jnp.int32, (1, tq), 1)
    valid = jnp.abs(kpos - qpos) <= C_WINDOW
    q_t = (q_ref[...].astype(F32) * (C_HEAD_DIM ** -0.5)).T.astype(BF16)
    valid_g = jnp.concatenate([valid] * C_GROUP, axis=1)
    for g in range(C_KV_HEADS):
        lo, hi = g * C_HEAD_DIM, (g + 1) * C_HEAD_DIM
        heads = range(g * C_GROUP, (g + 1) * C_GROUP)
        q_g = jnp.concatenate([q_t[h * C_HEAD_DIM:(h + 1) * C_HEAD_DIM] for h in heads], axis=1)
        pieces = []
        if lo:
            pieces.append(jnp.zeros((lo, C_GROUP * tq), BF16))
        pieces.append(q_g)
        if kvw - hi:
            pieces.append(jnp.zeros((kvw - hi, C_GROUP * tq), BF16))
        qm = jnp.concatenate(pieces, axis=0)
        sink = jnp.concatenate([jnp.broadcast_to(sink_ref[h:h + 1, :], (1, tq)) for h in heads], axis=1)
        s_c = jnp.dot(kc, qm, preferred_element_type=F32)
        s_w = jnp.where(valid_g, jnp.dot(kw, qm, preferred_element_type=F32), NEG_INF)
        m = jnp.maximum(jnp.maximum(jnp.max(s_c, axis=0, keepdims=True),
                                    jnp.max(s_w, axis=0, keepdims=True)), sink)
        p_c = jnp.exp(s_c - m)
        p_w = jnp.exp(s_w - m)
        den = (jnp.exp(sink - m) + jnp.sum(p_c, axis=0, keepdims=True)
               + jnp.sum(p_w, axis=0, keepdims=True))
        o_t = (jnp.dot(vtc[lo:hi], p_c.astype(BF16), preferred_element_type=F32)
               + jnp.dot(vtw[lo:hi], p_w.astype(BF16), preferred_element_type=F32)) / den
        per_slab = LANES // C_HEAD_DIM
        for s in range(C_GROUP // per_slab):
            slab = jnp.concatenate([o_t[:, (s * per_slab + j) * tq:(s * per_slab + j + 1) * tq]
                                    for j in range(per_slab)], axis=0)
            c0 = (g * C_GROUP + s * per_slab) * C_HEAD_DIM
            o_ref[:, c0:c0 + LANES] = slab.T.astype(o_ref.dtype)


def window_attention(p_qk, v_t, sink, batch, t_len, c_len):
    tq = ATTN_TQ
    qw = C_HEADS * C_HEAD_DIM
    kvw = C_KV_HEADS * C_HEAD_DIM
    assert qw % kvw == 0
    k_col = qw // kvw
    n_q = t_len // tq
    off_c = batch * (t_len // c_len)
    win = tq + 2 * C_WINDOW
    n_vt = win // C_WINDOW
    per_b = t_len // C_WINDOW

    def vt_blk(j):
        def index(b, i):
            first = jnp.clip(i * (tq // C_WINDOW) - 1, 0, per_b - n_vt)
            return (0, b * per_b + first + j)
        return pl.BlockSpec((kvw, C_WINDOW), index)

    kern = functools.partial(_swa_kernel, t_len=t_len, n_vt=n_vt)
    return pl.pallas_call(
        kern,
        grid=(batch, n_q),
        in_specs=[
            pl.BlockSpec((tq, qw), lambda b, i: (b * n_q + i, 0)),
            pl.BlockSpec((c_len, kvw), lambda b, i: (off_c + b, k_col)),
            pl.BlockSpec((kvw, c_len), lambda b, i: (0, off_c + b)),
            pl.BlockSpec((t_len, kvw), lambda b, i: (b, k_col)),
            *[vt_blk(j) for j in range(n_vt)],
            pl.BlockSpec((C_HEADS, 1), lambda b, i: (0, 0)),
        ],
        out_specs=pl.BlockSpec((tq, qw), lambda b, i: (b * n_q + i, 0)),
        out_shape=jax.ShapeDtypeStruct((batch * t_len, qw), BF16),
        compiler_params=_cparams(("parallel", "arbitrary")),
        name="window_attention",
    )(p_qk, p_qk, v_t, p_qk, *([v_t] * n_vt), sink.astype(F32).reshape(C_HEADS, 1))


def _to_row_tiles(ref, val):
    ref[...] = val.reshape(ref.shape)


def _from_row_tiles(ref):
    x = ref[...]
    return x.reshape(x.shape[0], x.shape[1] * x.shape[2])


def _out_proj_kernel(ya_ref, yb_ref, wa_ref, wb_ref, x_ref, g1_ref, g_ref, sc_ref, sh_ref, rw_ref,
                     *rest, n_main):
    if len(rest) == 4:
        yt_ref, xo_ref, h_ref, lg_ref = rest
        yb = jnp.where(pl.program_id(0) < n_main, yb_ref[...], yt_ref[...])
    else:
        xo_ref, h_ref, lg_ref = rest
        yb = yb_ref[...]
    y = (jnp.dot(ya_ref[...], wa_ref[...], preferred_element_type=F32)
         + jnp.dot(yb, wb_ref[...], preferred_element_type=F32))
    xn = x_ref[...] + g1_ref[...] * y
    xo_ref[...] = xn
    hn = xn * lax.rsqrt(jnp.mean(xn * xn, axis=-1, keepdims=True) + EPS) * g_ref[...]
    h2 = hn * (1.0 + sc_ref[...]) + sh_ref[...]
    _to_row_tiles(h_ref, h2)
    lg_ref[...] = lax.dot_general(rw_ref[...], h2, (((1,), (1,)), ((), ())),
                                  preferred_element_type=F32, precision=HIGHEST)


def out_proj(ya, yb_arr, yb_col, w_out, x_all, mods, g2n, router_wt, rows, mod_idx, yb_tail=None):
    d = x_all.shape[1]
    half = d // 2
    ne = router_wt.shape[0]
    row = lambda i: (i, 0)
    n_main = yb_arr.shape[0] // ROW_TILE
    in_specs = [
        pl.BlockSpec((ROW_TILE, half), row),
        pl.BlockSpec((ROW_TILE, half), lambda i: (jnp.minimum(i, n_main - 1), yb_col)),
        pl.BlockSpec((half, d), lambda i: (0, 0)),
        pl.BlockSpec((half, d), lambda i: (1, 0)),
        pl.BlockSpec((ROW_TILE, d), row),
        pl.BlockSpec((None, 1, d), lambda i: (mod_idx(i), 0, 2)),
        pl.BlockSpec((1, d), lambda i: (0, 0)),
        pl.BlockSpec((None, 1, d), lambda i: (mod_idx(i), 0, 4)),
        pl.BlockSpec((None, 1, d), lambda i: (mod_idx(i), 0, 3)),
        pl.BlockSpec((ne, d), lambda i: (0, 0)),
    ]
    args = [ya, yb_arr, w_out, w_out, x_all, mods, g2n, mods, mods, router_wt]
    if yb_tail is not None:
        in_specs.append(pl.BlockSpec((ROW_TILE, half), lambda i: (jnp.maximum(i - n_main, 0), 0)))
        args.append(yb_tail)
    return pl.pallas_call(
        functools.partial(_out_proj_kernel, n_main=n_main),
        grid=(rows // ROW_TILE,),
        in_specs=in_specs,
        out_specs=[pl.BlockSpec((ROW_TILE, d), row),
                   pl.BlockSpec((ROW_TILE, d // LANES, LANES), lambda i: (i, 0, 0)),
                   pl.BlockSpec((ne, ROW_TILE), lambda i: (0, i))],
        out_shape=[jax.ShapeDtypeStruct((rows, d), F32),
                   jax.ShapeDtypeStruct((rows, d // LANES, LANES), F32),
                   jax.ShapeDtypeStruct((ne, rows), F32)],
        compiler_params=_cparams(("parallel",)),
        name="out_proj",
    )(*args)


def _router_kernel(lg_ref, rb_ref, ids_ref, w_ref, cnt_ref, carry):
    @pl.when(pl.program_id(0) == 0)
    def _():
        carry[...] = jnp.zeros_like(carry)

    tr = lg_ref.shape[1]
    aff = jax.nn.sigmoid(lg_ref[...])
    sel = aff + rb_ref[...]
    rows = [sel[e:e + 1] for e in range(N_EXPERTS)]

    def beats(a, b, a_first):
        return jnp.logical_or(a > b, jnp.logical_and(a == b, a_first))

    def rank_among(vals, j):
        r = jnp.zeros(vals[0].shape, jnp.int32)
        for i2 in range(len(vals)):
            if i2 != j:
                r = r + beats(vals[i2], vals[j], i2 < j).astype(jnp.int32)
        return r

    rank, gscore = [], []
    for g in range(N_GROUPS):
        grp_rows = rows[g * EXPERTS_PER_GROUP:(g + 1) * EXPERTS_PER_GROUP]
        grp_rank = [rank_among(grp_rows, j) for j in range(EXPERTS_PER_GROUP)]
        rank += grp_rank
        gscore.append(sum(jnp.where(grp_rank[j] < TOP_K, grp_rows[j], 0.0) for j in range(EXPERTS_PER_GROUP)))
    chosen = [rank_among(gscore, g) == 0 for g in range(N_GROUPS)]
    onehot = []
    for k in range(TOP_K):
        hot_rows = [jnp.logical_and(chosen[e // EXPERTS_PER_GROUP], rank[e] == k).astype(F32)
                    for e in range(N_EXPERTS)]
        onehot.append(jnp.concatenate(hot_rows, axis=0))

    e_idx = lax.broadcasted_iota(jnp.int32, (N_EXPERTS, 1), 0).astype(F32)
    picked = [jnp.sum(onehot[k] * aff, axis=0, keepdims=True) for k in range(TOP_K)]
    denom = picked[0] + picked[1]
    before = (lax.broadcasted_iota(jnp.int32, (tr, tr), 0)
              < lax.broadcasted_iota(jnp.int32, (tr, tr), 1)).astype(BF16)
    both = onehot[0] + onehot[1]
    seen = carry[:, 0:1] + jnp.dot(both.astype(BF16), before, preferred_element_type=F32)
    id_rows = [jnp.sum(onehot[k] * e_idx, axis=0, keepdims=True) for k in range(TOP_K)]
    id_rows += [jnp.sum(onehot[k] * seen, axis=0, keepdims=True) for k in range(TOP_K)]
    pad = jnp.zeros((8 - 2 * TOP_K, tr), F32)
    ids_ref[...] = jnp.concatenate(id_rows + [pad], axis=0).astype(jnp.int32)
    w_ref[...] = jnp.concatenate([picked[0] / denom, picked[1] / denom,
                                  jnp.zeros((8 - TOP_K, tr), F32)], axis=0)
    carry[...] = carry[...] + jnp.sum(both, axis=1, keepdims=True)
    cnt_ref[...] = carry[...]


def moe_router(logits_t, router_b):
    ne, rows = logits_t.shape
    tr = ROW_TILE
    return pl.pallas_call(
        _router_kernel,
        grid=(rows // tr,),
        in_specs=[pl.BlockSpec((ne, tr), lambda i: (0, i)), pl.BlockSpec((ne, 1), lambda i: (0, 0))],
        out_specs=[pl.BlockSpec((8, tr), lambda i: (0, i)), pl.BlockSpec((8, tr), lambda i: (0, i)),
                   pl.BlockSpec((ne, LANES), lambda i: (0, 0))],
        out_shape=[jax.ShapeDtypeStruct((8, rows), jnp.int32), jax.ShapeDtypeStruct((8, rows), F32),
                   jax.ShapeDtypeStruct((ne, LANES), F32)],
        scratch_shapes=[pltpu.VMEM((ne, LANES), F32)],
        compiler_params=_cparams(("arbitrary",)),
        name="moe_router",
    )(logits_t, router_b.astype(F32).reshape(ne, 1))


def _tile_copy(src, dst, sem):
    return pltpu.make_async_copy(src, dst, sem)


def _dispatch_kernel(dest_ref, h_ref, zero_hbm, xs_hbm, sem):
    del zero_hbm
    n_tok = h_ref.shape[0]

    def start(t, carry):
        for k in range(TOP_K):
            _tile_copy(h_ref.at[t], xs_hbm.at[dest_ref[0, 0, TOP_K * t + k]], sem).start(priority=k)
        return carry

    lax.fori_loop(0, n_tok, start, 0, unroll=DMA_UNROLL)
    for k in range(TOP_K):
        _tile_copy(h_ref, xs_hbm.at[pl.ds(0, n_tok)], sem).wait()


def moe_dispatch(h2, dest, n_rows):
    rows, ns, nl = h2.shape
    n_t = rows // MOE_TOK
    dest3 = dest.reshape(n_t, 1, MOE_TOK * TOP_K)
    zeros = jnp.zeros((n_rows, ns, nl), F32)
    return pl.pallas_call(
        _dispatch_kernel,
        grid=(n_t,),
        in_specs=[
            pl.BlockSpec((1, 1, MOE_TOK * TOP_K), lambda i: (i, 0, 0), memory_space=pltpu.SMEM),
            pl.BlockSpec((MOE_TOK, ns, nl), lambda i: (i, 0, 0)),
            pl.BlockSpec(memory_space=pl.ANY),
        ],
        out_specs=pl.BlockSpec(memory_space=pl.ANY),
        out_shape=jax.ShapeDtypeStruct((n_rows, ns, nl), F32),
        scratch_shapes=[pltpu.SemaphoreType.DMA],
        input_output_aliases={2: 0},
        compiler_params=_cparams(("arbitrary",)),
        name="moe_dispatch",
    )(dest3, h2, zeros)


def _expert_kernel(be_ref, nv_ref, x_ref, wg_ref, wu_ref, wd_ref, o_ref, wg_s, wu_s, wd_s):
    i = pl.program_id(0)
    new_expert = jnp.logical_or(i == 0, be_ref[i] != be_ref[jnp.maximum(i - 1, 0)])

    @pl.when(new_expert)
    def _():
        wg_s[...] = wg_ref[...].astype(BF16)
        wu_s[...] = wu_ref[...].astype(BF16)
        wd_s[...] = wd_ref[...].astype(BF16)

    @pl.when(i < nv_ref[0])
    def _():
        rows = x_ref.shape[0] // MOE_SPLIT
        for part in range(MOE_SPLIT):
            xr = x_ref.at[part * rows:(part + 1) * rows]
            x = _from_row_tiles(xr).astype(BF16)
            hid = (_silu(jnp.dot(x, wg_s[...], preferred_element_type=F32))
                   * jnp.dot(x, wu_s[...], preferred_element_type=F32))
            _to_row_tiles(o_ref.at[part * rows:(part + 1) * rows],
                          jnp.dot(hid.astype(BF16), wd_s[...], preferred_element_type=F32))

    @pl.when(i >= nv_ref[0])
    def _():
        o_ref[...] = jnp.zeros_like(o_ref)


def moe_experts(x_sorted, block_e, n_valid, layer, w_gate, w_up, w_down):
    n_rows, ns, nl = x_sorted.shape
    d, de = w_gate.shape[2], w_gate.shape[3]
    n_blocks = n_rows // MOE_TILE
    grid_spec = pltpu.PrefetchScalarGridSpec(
        num_scalar_prefetch=2,
        grid=(n_blocks,),
        in_specs=[
            pl.BlockSpec((MOE_TILE, ns, nl), lambda i, be, nv: (i, 0, 0)),
            pl.BlockSpec((None, None, d, de), lambda i, be, nv: (layer, be[i], 0, 0)),
            pl.BlockSpec((None, None, d, de), lambda i, be, nv: (layer, be[i], 0, 0)),
            pl.BlockSpec((None, None, de, d), lambda i, be, nv: (layer, be[i], 0, 0)),
        ],
        out_specs=pl.BlockSpec((MOE_TILE, ns, nl), lambda i, be, nv: (i, 0, 0)),
        scratch_shapes=[pltpu.VMEM((d, de), BF16), pltpu.VMEM((d, de), BF16), pltpu.VMEM((de, d), BF16)],
    )
    return pl.pallas_call(
        _expert_kernel,
        grid_spec=grid_spec,
        out_shape=jax.ShapeDtypeStruct((n_rows, ns, nl), F32),
        compiler_params=_cparams(("arbitrary",)),
        name="moe_experts",
    )(block_e, n_valid, x_sorted, w_gate, w_up, w_down)


def _combine_kernel(dest_ref, dnext_ref, ys_hbm, w_ref, x_ref, g2_ref, gf_ref, o_ref, buf, sem, *, final_norm):
    i = pl.program_id(0)
    n_tok = buf.shape[2]

    def gather(d_ref, slot):
        def start(t, carry):
            for k in range(TOP_K):
                _tile_copy(ys_hbm.at[d_ref[0, 0, TOP_K * t + k]], buf.at[slot, k, t],
                           sem.at[slot]).start(priority=k)
            return carry

        lax.fori_loop(0, n_tok, start, 0, unroll=DMA_UNROLL)

    slot = i % 2

    @pl.when(i == 0)
    def _():
        gather(dest_ref, slot)

    @pl.when(i + 1 < pl.num_programs(0))
    def _():
        gather(dnext_ref, 1 - slot)

    for k in range(TOP_K):
        _tile_copy(ys_hbm.at[pl.ds(0, n_tok)], buf.at[slot, k], sem.at[slot]).wait()
    w = w_ref[...]
    f = w[:, 0:1] * _from_row_tiles(buf.at[slot, 0]) + w[:, 1:2] * _from_row_tiles(buf.at[slot, 1])
    xn = x_ref[...] + g2_ref[...] * f
    if final_norm:
        xn = xn * lax.rsqrt(jnp.mean(xn * xn, axis=-1, keepdims=True) + EPS) * gf_ref[...]
    o_ref[...] = xn


def moe_combine(y_sorted, dest, wts, x_new, mods, final_g, final_norm, mod_idx_tok):
    rows, d = x_new.shape
    n_t = rows // MOE_TOK
    dest3 = dest.reshape(n_t, 1, MOE_TOK * TOP_K)
    kern = functools.partial(_combine_kernel, final_norm=final_norm)
    return pl.pallas_call(
        kern,
        grid=(n_t,),
        in_specs=[
            pl.BlockSpec((1, 1, MOE_TOK * TOP_K), lambda i: (i, 0, 0), memory_space=pltpu.SMEM),
            pl.BlockSpec((1, 1, MOE_TOK * TOP_K), lambda i: (jnp.minimum(i + 1, n_t - 1), 0, 0),
                         memory_space=pltpu.SMEM),
            pl.BlockSpec(memory_space=pl.ANY),
            pl.BlockSpec((MOE_TOK, TOP_K), lambda i: (i, 0)),
            pl.BlockSpec((MOE_TOK, d), lambda i: (i, 0)),
            pl.BlockSpec((None, 1, d), lambda i: (mod_idx_tok(i), 0, 5)),
            pl.BlockSpec((1, d), lambda i: (0, 0)),
        ],
        out_specs=pl.BlockSpec((MOE_TOK, d), lambda i: (i, 0)),
        out_shape=jax.ShapeDtypeStruct((rows, d), F32),
        scratch_shapes=[pltpu.VMEM((2, TOP_K, MOE_TOK, d // LANES, LANES), F32),
                        pltpu.SemaphoreType.DMA((2,))],
        compiler_params=_cparams(("arbitrary",)),
        name="moe_combine",
    )(dest3, dest3, y_sorted, wts, x_new, mods, final_g)


def _row_plan(ids, counts, n_tok):
    counts = counts[:, 0].astype(jnp.int32)
    padded = (counts + MOE_TILE - 1) // MOE_TILE * MOE_TILE
    pad_end = jnp.cumsum(padded)
    pad_start = pad_end - padded
    experts = jnp.arange(N_EXPERTS, dtype=jnp.int32)[:, None]
    dest = [jnp.sum(jnp.where(ids[k][None, :] == experts, pad_start[:, None], 0), axis=0) + ids[TOP_K + k]
            for k in range(TOP_K)]
    dest = jnp.stack(dest, axis=1).astype(jnp.int32)
    n_blocks = -(-(n_tok * TOP_K) // MOE_TILE) + N_EXPERTS
    block_start = jnp.arange(n_blocks, dtype=jnp.int32) * MOE_TILE
    block_e = jnp.minimum(jnp.sum(pad_end[None, :] <= block_start[:, None], axis=1), N_EXPERTS - 1)
    n_valid = (pad_end[-1] // MOE_TILE).reshape(1)
    return dest, block_e.astype(jnp.int32), n_valid.astype(jnp.int32), n_blocks


def moe_layer(h2, logits_t, router_b, layer, w_gate, w_up, w_down, x_new, mods, final_g, final_norm,
              mod_idx_tok):
    n_tok = h2.shape[0]
    ids, wts8, counts = moe_router(logits_t, router_b)
    dest, block_e, n_valid, n_blocks = _row_plan(ids, counts, n_tok)
    wts = wts8[:TOP_K].T
    x_sorted = moe_dispatch(h2, dest, n_blocks * MOE_TILE)
    y_sorted = moe_experts(x_sorted, block_e, n_valid, layer, w_gate, w_up, w_down)
    return moe_combine(y_sorted, dest, wts, x_new, mods, final_g, final_norm, mod_idx_tok)


def _rope_tables(batch, t_len, c_len):
    n_rows = t_len // GRID_W
    row = jnp.repeat(jnp.arange(n_rows), GRID_W).astype(F32)
    col = jnp.tile(jnp.arange(GRID_W), n_rows).astype(F32)
    half = B_HALF // 2
    inv = 1.0 / (ROPE_BASE ** (jnp.arange(0, half, 2, dtype=F32) / half))
    ar, ac = row[:, None] * inv, col[:, None] * inv
    ang = jnp.concatenate([ar, ar, ac, ac], axis=-1)
    cos, sin = jnp.cos(ang), jnp.sin(ang)
    first = (jnp.arange(B_HALF) % 32) < 16
    sin_a = jnp.where(first, -sin, 0.0)
    sin_b = jnp.where(first, 0.0, sin)

    def full(tab, ctx_val):
        lat = jnp.tile(jnp.tile(tab, (1, LANES // B_HALF)), (batch, 1))
        ctx = jnp.full((batch * c_len, LANES), ctx_val, F32)
        return jnp.concatenate([lat, ctx], axis=0)

    return full(cos, 1.0), full(sin_a, 0.0), full(sin_b, 0.0)


def kernel(x, c, ctx, c_ctx, ada_w, ada_b, norm_mix_g, norm_ffn_g, even_w_in, even_w_out, hgrn_lb_logits,
           hgrn_onorm_g, diff_lambda, diff_subln_g, odd_w_qkv, odd_w_out, swa_sink, router_w, router_b,
           moe_w_gate, moe_w_up, moe_w_down, final_norm_g):
    batch, t_len, d = x.shape
    c_len = ctx.shape[1]
    n_lat = batch * t_len
    assert t_len % ROW_TILE == 0 and (batch * c_len) % ROW_TILE == 0 and c_len == HGRN_TILE
    assert batch < 8 and t_len % DIFF_TK == 0 and t_len % GRID_W == 0

    x_all = jnp.concatenate([x.reshape(n_lat, d), ctx.reshape(batch * c_len, d)], axis=0)
    rope = _rope_tables(batch, t_len, c_len)

    def mod_idx_for(tile):
        per_batch = t_len // tile
        return lambda i: jnp.minimum(i // per_batch, batch)

    mod_idx = mod_idx_for(ROW_TILE)
    mod_idx_tok = mod_idx_for(MOE_TOK)

    cvec = jnp.zeros((8, d), F32).at[:batch].set(c).at[batch].set(c_ctx)
    mods_all = ada_modulation(cvec, ada_w, ada_b)
    lower_bounds = jnp.cumsum(jax.nn.softmax(hgrn_lb_logits.astype(F32), axis=0), axis=0)
    router_wt = router_w.astype(F32).T

    mods = mods_all[0].reshape(8, 1, 6 * d)
    g_mix = norm_mix_g[0].reshape(1, d)
    w_in = even_w_in[0].astype(BF16)
    n_a = 5 * A_WIDTH
    p_a = norm_mod_matmul(x_all, g_mix, mods, 1, 0, w_in[:, :n_a], rope, (0, 0), F32, mod_idx)
    n_qk = n_a + 2 * B_WIDTH
    p_qk, v_t = norm_mod_matmul(x_all, g_mix, mods, 1, 0, w_in[:, n_a:n_qk], rope, (0, 2 * B_WIDTH), BF16,
                                mod_idx, wt=w_in[:, n_qk:].T)
    o_f, o_b = hgrn_scan(p_a, lower_bounds[0], batch, t_len, c_len)
    ya = hgrn_finish(o_f, o_b, p_a, hgrn_onorm_g[0].reshape(1, A_DK))
    lambda_init = 0.8 - 0.6 * math.exp(-0.3 * 0)
    yb, yb_ctx = diff_attention(p_qk, v_t, diff_lambda[0], diff_subln_g[0], lambda_init, batch, t_len, c_len)
    na = x_all.shape[0]
    x_new, h2, logits = out_proj(ya, yb, 0, even_w_out[0].astype(BF16), x_all, mods,
                                 norm_ffn_g[0].reshape(1, d), router_wt, na, mod_idx, yb_tail=yb_ctx)
    x_all = moe_layer(h2, logits, router_b, 0, moe_w_gate, moe_w_up, moe_w_down, x_new, mods,
                      final_norm_g.reshape(1, d), False, mod_idx_tok)

    mods = mods_all[1].reshape(8, 1, 6 * d)
    w_qkv = odd_w_qkv[0]
    q_cols = C_HEADS * C_HEAD_DIM
    kv_cols = C_KV_HEADS * C_HEAD_DIM

    n_qk1 = q_cols + kv_cols
    w_qkv = w_qkv.astype(BF16)
    p_qk1, v_t1 = norm_mod_matmul(x_all, norm_mix_g[1].reshape(1, d), mods, 1, 0, w_qkv[:, :n_qk1], rope,
                                  (0, n_qk1), BF16, mod_idx, tn=kv_cols, wt=w_qkv[:, n_qk1:].T)
    o1 = window_attention(p_qk1, v_t1, swa_sink[0], batch, t_len, c_len)
    x_new, h2, logits = out_proj(o1, o1, 1, odd_w_out[0].astype(BF16), x_all, mods,
                                 norm_ffn_g[1].reshape(1, d), router_wt, n_lat, mod_idx)
    out = moe_layer(h2, logits, router_b, 1, moe_w_gate, moe_w_up, moe_w_down, x_new, mods,
                    final_norm_g.reshape(1, d), True, mod_idx_tok)
    return out.reshape(batch, t_len, d)
```

```python
import functools
import math

import jax
import jax.numpy as jnp
from jax import lax
from jax.experimental import pallas as pl
from jax.experimental.pallas import tpu as pltpu

F32 = jnp.float32
BF16 = jnp.bfloat16
EPS = 1e-6
ROPE_BASE = 10000.0
GRID_W = 64

A_HEADS, A_DK = 4, 128
A_WIDTH = A_HEADS * A_DK
HGRN_CHUNK = 16
B_HEADS, B_HALF = 4, 64
B_DV = 2 * B_HALF
B_WIDTH = B_HEADS * B_DV
C_HEADS, C_KV_HEADS, C_HEAD_DIM, C_WINDOW = 16, 4, 64, 128
C_GROUP = C_HEADS // C_KV_HEADS
N_EXPERTS, N_GROUPS, TOP_K = 16, 4, 2
EXPERTS_PER_GROUP = N_EXPERTS // N_GROUPS

LANES = 128
ROW_TILE = 512
HGRN_TILE = 256
HGRN_HEADS_PER_STEP = 1
ATTN_TQ = 256
DIFF_TK = 512
DIFF_UNROLL = 4
DIFF_TQ = 512
MOE_TILE = 256
MOE_SPLIT = 1
MOE_TOK = 512
DMA_UNROLL = 8
VMEM_LIMIT = 56 * 1024 * 1024
HIGHEST = lax.Precision.HIGHEST
NEG_INF = float("-inf")
LOG2_E = math.log2(math.e)


def _cparams(sem):
    return pltpu.CompilerParams(dimension_semantics=sem, vmem_limit_bytes=VMEM_LIMIT)


def _nt_dot(a, b):
    return lax.dot_general(a, b, (((1,), (1,)), ((), ())), preferred_element_type=F32)


def _sigmoid(x):
    return 0.5 * jnp.tanh(0.5 * x) + 0.5


def _silu(x):
    return x * _sigmoid(x)


def _ada_kernel(c_ref, w_ref, b_ref, o_ref):
    s = _silu(c_ref[...])
    o_ref[...] = jnp.dot(s, w_ref[...], preferred_element_type=F32, precision=HIGHEST) + b_ref[...]


def ada_modulation(cvec, ada_w, ada_b):
    n_layers, d, n6 = ada_w.shape
    tn = 512
    return pl.pallas_call(
        _ada_kernel,
        grid=(n_layers, n6 // tn),
        in_specs=[
            pl.BlockSpec((8, d), lambda l, j: (0, 0)),
            pl.BlockSpec((None, d, tn), lambda l, j: (l, 0, j)),
            pl.BlockSpec((None, 1, tn), lambda l, j: (l, 0, j)),
        ],
        out_specs=pl.BlockSpec((None, 8, tn), lambda l, j: (l, 0, j)),
        out_shape=jax.ShapeDtypeStruct((n_layers, 8, n6), F32),
        compiler_params=_cparams(("parallel", "parallel")),
        name="ada_modulation",
    )(cvec, ada_w, ada_b.reshape(n_layers, 1, n6))


def _rope_slab(x, cos, sin_a, sin_b):
    return x * cos + pltpu.roll(x, LANES - 16, 1) * sin_a + pltpu.roll(x, 16, 1) * sin_b


def _nmm_kernel(x_ref, g_ref, sc_ref, sh_ref, w_ref, cos_ref, sa_ref, sb_ref, *rest, rope_cols, tn):
    x = x_ref[...]
    y = x * lax.rsqrt(jnp.mean(x * x, axis=-1, keepdims=True) + EPS) * g_ref[...]
    h = (y * (1.0 + sc_ref[...]) + sh_ref[...]).astype(BF16)
    if len(rest) == 3:
        wt_ref, o_ref, ot_ref = rest
        ot_ref[...] = _nt_dot(wt_ref[...], h).astype(ot_ref.dtype)
    else:
        (o_ref,) = rest
    n = w_ref.shape[1]
    for c0 in range(0, n, tn):
        acc = jnp.dot(h, w_ref[:, c0:c0 + tn], preferred_element_type=F32)
        if rope_cols[0] <= c0 < rope_cols[1]:
            cos, sa, sb = cos_ref[...], sa_ref[...], sb_ref[...]
            for c in range(tn // LANES):
                sl = slice(c * LANES, (c + 1) * LANES)
                o_ref[:, c0 + c * LANES:c0 + (c + 1) * LANES] = (
                    _rope_slab(acc[:, sl], cos, sa, sb).astype(o_ref.dtype))
        else:
            o_ref[:, c0:c0 + tn] = acc.astype(o_ref.dtype)


def norm_mod_matmul(x_all, g, mods, sc_chunk, sh_chunk, w, rope, rope_cols, out_dtype, mod_idx, tn=512,
                    wt=None):
    na, d = x_all.shape
    n = w.shape[1]
    cos, sa, sb = rope
    assert n % tn == 0 and rope_cols[0] % tn == 0 and rope_cols[1] % tn == 0
    kern = functools.partial(_nmm_kernel, rope_cols=rope_cols, tn=tn)
    row = lambda i: (i, 0)
    in_specs = [
        pl.BlockSpec((ROW_TILE, d), row),
        pl.BlockSpec((1, d), lambda i: (0, 0)),
        pl.BlockSpec((None, 1, d), lambda i: (mod_idx(i), 0, sc_chunk)),
        pl.BlockSpec((None, 1, d), lambda i: (mod_idx(i), 0, sh_chunk)),
        pl.BlockSpec((d, n), lambda i: (0, 0)),
        pl.BlockSpec((ROW_TILE, LANES), row),
        pl.BlockSpec((ROW_TILE, LANES), row),
        pl.BlockSpec((ROW_TILE, LANES), row),
    ]
    out_specs = [pl.BlockSpec((ROW_TILE, n), row)]
    out_shape = [jax.ShapeDtypeStruct((na, n), out_dtype)]
    args = [x_all, g, mods, mods, w, cos, sa, sb]
    if wt is not None:
        nt = wt.shape[0]
        in_specs.append(pl.BlockSpec((nt, d), lambda i: (0, 0)))
        out_specs.append(pl.BlockSpec((nt, ROW_TILE), lambda i: (0, i)))
        out_shape.append(jax.ShapeDtypeStruct((nt, na), out_dtype))
        args.append(wt)
    outs = pl.pallas_call(
        kern,
        grid=(na // ROW_TILE,),
        in_specs=in_specs,
        out_specs=out_specs,
        out_shape=out_shape,
        compiler_params=_cparams(("parallel",)),
        name="norm_mod_matmul",
    )(*args)
    return outs[0] if wt is None else outs


def _hgrn_chunk_matrices(tt):
    r = lax.broadcasted_iota(jnp.int32, (tt, tt), 0)
    c = lax.broadcasted_iota(jnp.int32, (tt, tt), 1)
    same = (r // HGRN_CHUNK) == (c // HGRN_CHUNK)
    mats = [jnp.logical_and(same, c <= r), jnp.logical_and(same, c >= r), same]
    return jnp.stack(mats).astype(BF16)


def _hgrn_direction(q_raw, f_raw, v, lb_row, st, m_cum, m_tot, reverse):
    tt = q_raw.shape[0]
    n_chunks = tt // HGRN_CHUNK
    qs = _silu(q_raw) * (A_DK ** -0.5)
    f = lb_row + (1.0 - lb_row) * _sigmoid(f_raw)
    kk = 1.0 - f
    lf = jnp.log(f)

    hi = lf.astype(BF16)
    r1 = lf - hi.astype(F32)
    mid = r1.astype(BF16)
    lo = (r1 - mid.astype(F32)).astype(BF16)

    def split_dot(m):
        return (jnp.dot(m, hi, preferred_element_type=F32) + jnp.dot(m, mid, preferred_element_type=F32)
                + jnp.dot(m, lo, preferred_element_type=F32))

    cum = split_dot(m_cum)
    tot = split_dot(m_tot)
    q_in = qs * jnp.exp(cum)
    k_out = kk * jnp.exp(tot - cum)

    half = HGRN_CHUNK // 2
    assert half == 8

    def halves(x):
        x4 = x.reshape(n_chunks, 2, half, A_DK)
        return (x4[:, 1], x4[:, 0]) if reverse else (x4[:, 0], x4[:, 1])

    def rot(x, d):
        if d == 0:
            return x
        return pltpu.roll(x, (half - d) if reverse else d, 1)

    sub = lax.broadcasted_iota(jnp.int32, (1, half, 1), 1)
    if reverse:
        sub = half - 1 - sub

    def contrib(q_t, c_t, k_s, c_s, v_s, valid):
        diff = c_t - c_s
        if valid is not None:
            diff = jnp.where(valid, diff, NEG_INF)
        return jnp.sum(q_t * k_s * jnp.exp(diff), axis=-1, keepdims=True) * v_s

    (q_n, q_f), (k_n, k_f), (c_n, c_f), (v_n, v_f) = halves(qs), halves(kk), halves(cum), halves(v)
    o_n = jnp.sum(q_n * k_n, axis=-1, keepdims=True) * v_n
    o_f = jnp.sum(q_f * k_f, axis=-1, keepdims=True) * v_f
    for dist in range(1, half):
        ok = sub >= dist
        kn, cn, vn = rot(k_n, dist), rot(c_n, dist), rot(v_n, dist)
        kf, cf, vf = rot(k_f, dist), rot(c_f, dist), rot(v_f, dist)
        o_n = o_n + contrib(q_n, c_n, kn, cn, vn, ok)
        o_f = o_f + contrib(q_f, c_f, jnp.where(ok, kf, kn), jnp.where(ok, cf, cn), jnp.where(ok, vf, vn), None)
    for dist in range(half, HGRN_CHUNK):
        e = dist - half
        o_f = o_f + contrib(q_f, c_f, rot(k_n, e), rot(c_n, e), rot(v_n, e), sub >= e)
    o = jnp.stack([o_f, o_n] if reverse else [o_n, o_f], axis=1).reshape(tt, A_DK)

    v_t = v.T.astype(BF16)
    q_in_t = q_in.T.astype(BF16)
    k_out_b = k_out.astype(BF16)
    lane_chunk = lax.broadcasted_iota(jnp.int32, (1, tt), 1) // HGRN_CHUNK
    updates = [jnp.dot(v_t * (lane_chunk == ci).astype(BF16), k_out_b, preferred_element_type=F32)
               for ci in range(n_chunks)]
    o_t = jnp.zeros((A_DK, tt), F32)
    order = range(n_chunks - 1, -1, -1) if reverse else range(n_chunks)
    for ci in order:
        lo_r = ci * HGRN_CHUNK
        from_state = jnp.dot(st.astype(BF16), q_in_t, preferred_element_type=F32)
        o_t = jnp.where(lane_chunk == ci, from_state, o_t)
        st = st * jnp.exp(tot[lo_r:lo_r + 1]) + updates[ci]
    return o + o_t.T, st


def _hgrn_kernel(qf_ref, ff_ref, vf_ref, qb_ref, fb_ref, vb_ref, lb_ref, mats_ref, of_ref, ob_ref, st_scr):
    @pl.when(pl.program_id(2) == 0)
    def _():
        st_scr[...] = jnp.zeros_like(st_scr)

    for hh in range(HGRN_HEADS_PER_STEP):
        sl = slice(hh * A_DK, (hh + 1) * A_DK)
        o_f, st_f = _hgrn_direction(qf_ref[:, sl], ff_ref[:, sl], vf_ref[:, sl], lb_ref[0:1, sl],
                                    st_scr[2 * hh], mats_ref[0], mats_ref[2], False)
        of_ref[:, sl] = o_f
        st_scr[2 * hh] = st_f
        o_b, st_b = _hgrn_direction(qb_ref[:, sl], fb_ref[:, sl], vb_ref[:, sl], lb_ref[1:2, sl],
                                    st_scr[2 * hh + 1], mats_ref[1], mats_ref[2], True)
        ob_ref[:, sl] = o_b
        st_scr[2 * hh + 1] = st_b


def hgrn_scan(p_a, lb, batch, t_len, c_len):
    na = p_a.shape[0]
    tt = HGRN_TILE
    n_c, n_t = c_len // tt, t_len // tt
    off_c = batch * n_t

    def rb_f(b, s):
        return jnp.where(s < n_c, off_c + b * n_c + s, b * n_t + (s - n_c))

    def rb_b(b, s):
        return jnp.where(s < n_c, off_c + b * n_c + (n_c - 1 - s), b * n_t + (n_t - 1 - (s - n_c)))

    hps = HGRN_HEADS_PER_STEP
    n_hb = A_HEADS // hps

    def spec(rb, group):
        return pl.BlockSpec((tt, hps * A_DK), lambda b, h, s: (rb(b, s), group * n_hb + h))

    out_sds = jax.ShapeDtypeStruct((na, A_WIDTH), F32)
    return pl.pallas_call(
        _hgrn_kernel,
        grid=(batch, n_hb, n_c + n_t),
        in_specs=[spec(rb_f, 0), spec(rb_f, 1), spec(rb_f, 3),
                  spec(rb_b, 0), spec(rb_b, 2), spec(rb_b, 3),
                  pl.BlockSpec((2, hps * A_DK), lambda b, h, s: (0, h)),
                  pl.BlockSpec((3, tt, tt), lambda b, h, s: (0, 0, 0))],
        out_specs=[spec(rb_f, 0), spec(rb_b, 0)],
        out_shape=[out_sds, out_sds],
        scratch_shapes=[pltpu.VMEM((2 * hps, A_DK, A_DK), F32)],
        compiler_params=_cparams(("parallel", "parallel", "arbitrary")),
        name="hgrn_scan",
    )(p_a, p_a, p_a, p_a, p_a, p_a, lb, _hgrn_chunk_matrices(tt))


def _hgrn_finish_kernel(of_ref, ob_ref, g_ref, gn_ref, o_ref):
    o = of_ref[...] + ob_ref[...]
    gate = _silu(g_ref[...])
    for h in range(A_HEADS):
        sl = slice(h * A_DK, (h + 1) * A_DK)
        oh = o[:, sl]
        y = oh * lax.rsqrt(jnp.mean(oh * oh, axis=-1, keepdims=True) + EPS) * gn_ref[...]
        o_ref[:, sl] = (y * gate[:, sl]).astype(o_ref.dtype)


def hgrn_finish(o_f, o_b, p_a, onorm_g):
    na = o_f.shape[0]
    blk = pl.BlockSpec((ROW_TILE, A_WIDTH), lambda i: (i, 0))
    return pl.pallas_call(
        _hgrn_finish_kernel,
        grid=(na // ROW_TILE,),
        in_specs=[blk, blk, pl.BlockSpec((ROW_TILE, A_WIDTH), lambda i: (i, 4)),
                  pl.BlockSpec((1, A_DK), lambda i: (0, 0))],
        out_specs=blk,
        out_shape=jax.ShapeDtypeStruct((na, A_WIDTH), BF16),
        compiler_params=_cparams(("parallel",)),
        name="hgrn_finish",
    )(o_f, o_b, p_a, onorm_g)


def _diff_attn_kernel(q_ref, kc_ref, vtc_ref, k_ref, vt_ref, lam_ref, gn_ref, o_ref,
                      m_scr, l_scr, acc_scr, *, n_k, unroll, lambda_init):
    q_t = (q_ref[...].astype(F32) * (B_HALF ** -0.5 * LOG2_E)).T.astype(BF16)
    tq = q_t.shape[1]
    sub = lax.broadcasted_iota(jnp.int32, (B_DV, 1), 0)
    zero = jnp.zeros_like(q_t)
    qm = jnp.concatenate([jnp.where(sub < B_HALF, q_t, zero), jnp.where(sub >= B_HALF, q_t, zero)], axis=1)
    nc = 2 * tq

    m_scr[...] = jnp.full_like(m_scr, NEG_INF)
    l_scr[...] = jnp.zeros_like(l_scr)
    acc_scr[...] = jnp.zeros_like(acc_scr)

    def fold8(s, op):
        return op(s.reshape(s.shape[0] // 8, 8, nc), axis=0)

    def update(kvs):
        ss = [jnp.dot(kb, qm, preferred_element_type=F32) for kb, _ in kvs]
        part = functools.reduce(jnp.maximum, [fold8(s, jnp.max) for s in ss])
        m_old = m_scr[0:1, :]
        m_new = jnp.maximum(m_old, jnp.max(part, axis=0, keepdims=True))
        alpha = jnp.exp2(m_old - m_new)
        lsum = alpha * l_scr[...]
        acc = alpha * acc_scr[...]
        for s, (_, vtb) in zip(ss, kvs):
            p = jnp.exp2(s - m_new)
            lsum = lsum + fold8(p, jnp.sum)
            acc = acc + jnp.dot(vtb, p.astype(BF16), preferred_element_type=F32)
        l_scr[...] = lsum
        acc_scr[...] = acc
        m_scr[...] = jnp.broadcast_to(m_new, (8, nc))

    update([(kc_ref[...], vtc_ref[...])])
    for j in range(n_k):
        kvs = []
        for u in range(unroll):
            start = (j * unroll + u) * DIFF_TK
            kvs.append((k_ref[start:start + DIFF_TK, :], vt_ref[:, start:start + DIFF_TK]))
        update(kvs)

    lp = lam_ref[...]
    lam = (jnp.exp(jnp.sum(lp[0:1] * lp[1:2], axis=-1, keepdims=True))
           - jnp.exp(jnp.sum(lp[2:3] * lp[3:4], axis=-1, keepdims=True)) + lambda_init)
    on = acc_scr[...] / jnp.sum(l_scr[...], axis=0, keepdims=True)
    o = on[:, :tq] - lam * on[:, tq:]
    y = o * lax.rsqrt(jnp.mean(o * o, axis=0, keepdims=True) + EPS) * gn_ref[...]
    o_ref[...] = (y * (1.0 - lambda_init)).T.astype(o_ref.dtype)


def diff_attention(p_qk, v_t, lam_p, subln_g, lambda_init, batch, t_len, c_len):
    off_c = batch * (t_len // c_len)
    unroll = min(DIFF_UNROLL, t_len // DIFF_TK)
    assert t_len % (unroll * DIFF_TK) == 0 and t_len % DIFF_TQ == 0

    def call(tq, n_q, q_off, n_k):
        kern = functools.partial(_diff_attn_kernel, n_k=n_k, unroll=unroll, lambda_init=lambda_init)
        in_specs = [
            pl.BlockSpec((tq, B_DV), lambda b, h, i: (q_off + b * n_q + i, h)),
            pl.BlockSpec((c_len, B_DV), lambda b, h, i: (off_c + b, B_HEADS + h)),
            pl.BlockSpec((B_DV, c_len), lambda b, h, i: (h, off_c + b)),
            pl.BlockSpec((t_len, B_DV), lambda b, h, i: (b, B_HEADS + h)),
            pl.BlockSpec((B_DV, t_len), lambda b, h, i: (h, b)),
            pl.BlockSpec((4, B_HALF), lambda b, h, i: (0, 0)),
            pl.BlockSpec((B_DV, 1), lambda b, h, i: (0, 0)),
        ]
        args = [p_qk, p_qk, v_t, p_qk, v_t, lam_p, subln_g.reshape(B_DV, 1)]
        return pl.pallas_call(
            kern,
            grid=(batch, B_HEADS, n_q),
            in_specs=in_specs,
            out_specs=pl.BlockSpec((tq, B_DV), lambda b, h, i: (b * n_q + i, h)),
            out_shape=jax.ShapeDtypeStruct((batch * n_q * tq, B_WIDTH), BF16),
            scratch_shapes=[pltpu.VMEM((8, 2 * tq), F32), pltpu.VMEM((8, 2 * tq), F32),
                            pltpu.VMEM((B_DV, 2 * tq), F32)],
            compiler_params=_cparams(("parallel", "parallel", "arbitrary")),
            name="diff_attention",
        )(*args)

    y_lat = call(DIFF_TQ, t_len // DIFF_TQ, 0, t_len // (unroll * DIFF_TK))
    y_ctx = call(c_len, 1, off_c, 0)
    return y_lat, y_ctx


def _swa_kernel(q_ref, kc_ref, vtc_ref, k_ref, *rest, t_len, n_vt):
    vt_refs, (sink_ref, o_ref) = rest[:n_vt], rest[n_vt:]
    i = pl.program_id(1)
    tq = q_ref.shape[0]
    kvw = C_KV_HEADS * C_HEAD_DIM
    win = tq + 2 * C_WINDOW
    a = i * tq
    start = pl.multiple_of(jnp.clip(a - C_WINDOW, 0, t_len - win), C_WINDOW)
    kw = k_ref[pl.ds(start, win), :]
    vtw = jnp.concatenate([r[...] for r in vt_refs], axis=1)
    kc, vtc = kc_ref[...], vtc_ref[...]
    kpos = start + lax.broadcasted_iota(jnp.int32, (win, 1), 0)
    qpos = a + lax.broadcasted_iota(jnp.int32, (1, tq), 1)
    valid = jnp.abs(kpos - qpos) <= C_WINDOW
    q_t = (q_ref[...].astype(F32) * (C_HEAD_DIM ** -0.5)).T.astype(BF16)
    valid_g = jnp.concatenate([valid] * C_GROUP, axis=1)
    for g in range(C_KV_HEADS):
        lo, hi = g * C_HEAD_DIM, (g + 1) * C_HEAD_DIM
        heads = range(g * C_GROUP, (g + 1) * C_GROUP)
        q_g = jnp.concatenate([q_t[h * C_HEAD_DIM:(h + 1) * C_HEAD_DIM] for h in heads], axis=1)
        pieces = []
        if lo:
            pieces.append(jnp.zeros((lo, C_GROUP * tq), BF16))
        pieces.append(q_g)
        if kvw - hi:
            pieces.append(jnp.zeros((kvw - hi, C_GROUP * tq), BF16))
        qm = jnp.concatenate(pieces, axis=0)
        sink = jnp.concatenate([jnp.broadcast_to(sink_ref[h:h + 1, :], (1, tq)) for h in heads], axis=1)
        s_c = jnp.dot(kc, qm, preferred_element_type=F32)
        s_w = jnp.where(valid_g, jnp.dot(kw, qm, preferred_element_type=F32), NEG_INF)
        m = jnp.maximum(jnp.maximum(jnp.max(s_c, axis=0, keepdims=True),
                                    jnp.max(s_w, axis=0, keepdims=True)), sink)
        p_c = jnp.exp(s_c - m)
        p_w = jnp.exp(s_w - m)
        den = (jnp.exp(sink - m) + jnp.sum(p_c, axis=0, keepdims=True)
               + jnp.sum(p_w, axis=0, keepdims=True))
        o_t = (jnp.dot(vtc[lo:hi], p_c.astype(BF16), preferred_element_type=F32)
               + jnp.dot(vtw[lo:hi], p_w.astype(BF16), preferred_element_type=F32)) / den
        per_slab = LANES // C_HEAD_DIM
        for s in range(C_GROUP // per_slab):
            slab = jnp.concatenate([o_t[:, (s * per_slab + j) * tq:(s * per_slab + j + 1) * tq]
                                    for j in range(per_slab)], axis=0)
            c0 = (g * C_GROUP + s * per_slab) * C_HEAD_DIM
            o_ref[:, c0:c0 + LANES] = slab.T.astype(o_ref.dtype)


def window_attention(p_qk, v_t, sink, batch, t_len, c_len):
    tq = ATTN_TQ
    qw = C_HEADS * C_HEAD_DIM
    kvw = C_KV_HEADS * C_HEAD_DIM
    assert qw % kvw == 0
    k_col = qw // kvw
    n_q = t_len // tq
    off_c = batch * (t_len // c_len)
    win = tq + 2 * C_WINDOW
    n_vt = win // C_WINDOW
    per_b = t_len // C_WINDOW

    def vt_blk(j):
        def index(b, i):
            first = jnp.clip(i * (tq // C_WINDOW) - 1, 0, per_b - n_vt)
            return (0, b * per_b + first + j)
        return pl.BlockSpec((kvw, C_WINDOW), index)

    kern = functools.partial(_swa_kernel, t_len=t_len, n_vt=n_vt)
    return pl.pallas_call(
        kern,
        grid=(batch, n_q),
        in_specs=[
            pl.BlockSpec((tq, qw), lambda b, i: (b * n_q + i, 0)),
            pl.BlockSpec((c_len, kvw), lambda b, i: (off_c + b, k_col)),
            pl.BlockSpec((kvw, c_len), lambda b, i: (0, off_c + b)),
            pl.BlockSpec((t_len, kvw), lambda b, i: (b, k_col)),
            *[vt_blk(j) for j in range(n_vt)],
            pl.BlockSpec((C_HEADS, 1), lambda b, i: (0, 0)),
        ],
        out_specs=pl.BlockSpec((tq, qw), lambda b, i: (b * n_q + i, 0)),
        out_shape=jax.ShapeDtypeStruct((batch * t_len, qw), BF16),
        compiler_params=_cparams(("parallel", "arbitrary")),
        name="window_attention",
    )(p_qk, p_qk, v_t, p_qk, *([v_t] * n_vt), sink.astype(F32).reshape(C_HEADS, 1))


def _to_row_tiles(ref, val):
    ref[...] = val.reshape(ref.shape)


def _from_row_tiles(ref):
    x = ref[...]
    return x.reshape(x.shape[0], x.shape[1] * x.shape[2])


def _out_proj_kernel(ya_ref, yb_ref, wa_ref, wb_ref, x_ref, g1_ref, g_ref, sc_ref, sh_ref, rw_ref,
                     *rest, n_main):
    if len(rest) == 4:
        yt_ref, xo_ref, h_ref, lg_ref = rest
        yb = jnp.where(pl.program_id(0) < n_main, yb_ref[...], yt_ref[...])
    else:
        xo_ref, h_ref, lg_ref = rest
        yb = yb_ref[...]
    y = (jnp.dot(ya_ref[...], wa_ref[...], preferred_element_type=F32)
         + jnp.dot(yb, wb_ref[...], preferred_element_type=F32))
    xn = x_ref[...] + g1_ref[...] * y
    xo_ref[...] = xn
    hn = xn * lax.rsqrt(jnp.mean(xn * xn, axis=-1, keepdims=True) + EPS) * g_ref[...]
    h2 = hn * (1.0 + sc_ref[...]) + sh_ref[...]
    _to_row_tiles(h_ref, h2)
    lg_ref[...] = lax.dot_general(rw_ref[...], h2, (((1,), (1,)), ((), ())),
                                  preferred_element_type=F32, precision=HIGHEST)


def out_proj(ya, yb_arr, yb_col, w_out, x_all, mods, g2n, router_wt, rows, mod_idx, yb_tail=None):
    d = x_all.shape[1]
    half = d // 2
    ne = router_wt.shape[0]
    row = lambda i: (i, 0)
    n_main = yb_arr.shape[0] // ROW_TILE
    in_specs = [
        pl.BlockSpec((ROW_TILE, half), row),
        pl.BlockSpec((ROW_TILE, half), lambda i: (jnp.minimum(i, n_main - 1), yb_col)),
        pl.BlockSpec((half, d), lambda i: (0, 0)),
        pl.BlockSpec((half, d), lambda i: (1, 0)),
        pl.BlockSpec((ROW_TILE, d), row),
        pl.BlockSpec((None, 1, d), lambda i: (mod_idx(i), 0, 2)),
        pl.BlockSpec((1, d), lambda i: (0, 0)),
        pl.BlockSpec((None, 1, d), lambda i: (mod_idx(i), 0, 4)),
        pl.BlockSpec((None, 1, d), lambda i: (mod_idx(i), 0, 3)),
        pl.BlockSpec((ne, d), lambda i: (0, 0)),
    ]
    args = [ya, yb_arr, w_out, w_out, x_all, mods, g2n, mods, mods, router_wt]
    if yb_tail is not None:
        in_specs.append(pl.BlockSpec((ROW_TILE, half), lambda i: (jnp.maximum(i - n_main, 0), 0)))
        args.append(yb_tail)
    return pl.pallas_call(
        functools.partial(_out_proj_kernel, n_main=n_main),
        grid=(rows // ROW_TILE,),
        in_specs=in_specs,
        out_specs=[pl.BlockSpec((ROW_TILE, d), row),
                   pl.BlockSpec((ROW_TILE, d // LANES, LANES), lambda i: (i, 0, 0)),
                   pl.BlockSpec((ne, ROW_TILE), lambda i: (0, i))],
        out_shape=[jax.ShapeDtypeStruct((rows, d), F32),
                   jax.ShapeDtypeStruct((rows, d // LANES, LANES), F32),
                   jax.ShapeDtypeStruct((ne, rows), F32)],
        compiler_params=_cparams(("parallel",)),
        name="out_proj",
    )(*args)


def _router_kernel(lg_ref, rb_ref, ids_ref, w_ref, cnt_ref, carry):
    @pl.when(pl.program_id(0) == 0)
    def _():
        carry[...] = jnp.zeros_like(carry)

    tr = lg_ref.shape[1]
    aff = jax.nn.sigmoid(lg_ref[...])
    sel = aff + rb_ref[...]
    rows = [sel[e:e + 1] for e in range(N_EXPERTS)]

    def beats(a, b, a_first):
        return jnp.logical_or(a > b, jnp.logical_and(a == b, a_first))

    def rank_among(vals, j):
        r = jnp.zeros(vals[0].shape, jnp.int32)
        for i2 in range(len(vals)):
            if i2 != j:
                r = r + beats(vals[i2], vals[j], i2 < j).astype(jnp.int32)
        return r

    rank, gscore = [], []
    for g in range(N_GROUPS):
        grp_rows = rows[g * EXPERTS_PER_GROUP:(g + 1) * EXPERTS_PER_GROUP]
        grp_rank = [rank_among(grp_rows, j) for j in range(EXPERTS_PER_GROUP)]
        rank += grp_rank
        gscore.append(sum(jnp.where(grp_rank[j] < TOP_K, grp_rows[j], 0.0) for j in range(EXPERTS_PER_GROUP)))
    chosen = [rank_among(gscore, g) == 0 for g in range(N_GROUPS)]
    onehot = []
    for k in range(TOP_K):
        hot_rows = [jnp.logical_and(chosen[e // EXPERTS_PER_GROUP], rank[e] == k).astype(F32)
                    for e in range(N_EXPERTS)]
        onehot.append(jnp.concatenate(hot_rows, axis=0))

    e_idx = lax.broadcasted_iota(jnp.int32, (N_EXPERTS, 1), 0).astype(F32)
    picked = [jnp.sum(onehot[k] * aff, axis=0, keepdims=True) for k in range(TOP_K)]
    denom = picked[0] + picked[1]
    before = (lax.broadcasted_iota(jnp.int32, (tr, tr), 0)
              < lax.broadcasted_iota(jnp.int32, (tr, tr), 1)).astype(BF16)
    both = onehot[0] + onehot[1]
    seen = carry[:, 0:1] + jnp.dot(both.astype(BF16), before, preferred_element_type=F32)
    id_rows = [jnp.sum(onehot[k] * e_idx, axis=0, keepdims=True) for k in range(TOP_K)]
    id_rows += [jnp.sum(onehot[k] * seen, axis=0, keepdims=True) for k in range(TOP_K)]
    pad = jnp.zeros((8 - 2 * TOP_K, tr), F32)
    ids_ref[...] = jnp.concatenate(id_rows + [pad], axis=0).astype(jnp.int32)
    w_ref[...] = jnp.concatenate([picked[0] / denom, picked[1] / denom,
                                  jnp.zeros((8 - TOP_K, tr), F32)], axis=0)
    carry[...] = carry[...] + jnp.sum(both, axis=1, keepdims=True)
    cnt_ref[...] = carry[...]


def moe_router(logits_t, router_b):
    ne, rows = logits_t.shape
    tr = ROW_TILE
    return pl.pallas_call(
        _router_kernel,
        grid=(rows // tr,),
        in_specs=[pl.BlockSpec((ne, tr), lambda i: (0, i)), pl.BlockSpec((ne, 1), lambda i: (0, 0))],
        out_specs=[pl.BlockSpec((8, tr), lambda i: (0, i)), pl.BlockSpec((8, tr), lambda i: (0, i)),
                   pl.BlockSpec((ne, LANES), lambda i: (0, 0))],
        out_shape=[jax.ShapeDtypeStruct((8, rows), jnp.int32), jax.ShapeDtypeStruct((8, rows), F32),
                   jax.ShapeDtypeStruct((ne, LANES), F32)],
        scratch_shapes=[pltpu.VMEM((ne, LANES), F32)],
        compiler_params=_cparams(("arbitrary",)),
        name="moe_router",
    )(logits_t, router_b.astype(F32).reshape(ne, 1))


def _tile_copy(src, dst, sem):
    return pltpu.make_async_copy(src, dst, sem)


def _dispatch_kernel(dest_ref, h_ref, zero_hbm, xs_hbm, sem):
    del zero_hbm
    n_tok = h_ref.shape[0]

    def start(t, carry):
        for k in range(TOP_K):
            _tile_copy(h_ref.at[t], xs_hbm.at[dest_ref[0, 0, TOP_K * t + k]], sem).start(priority=k)
        return carry

    lax.fori_loop(0, n_tok, start, 0, unroll=DMA_UNROLL)
    for k in range(TOP_K):
        _tile_copy(h_ref, xs_hbm.at[pl.ds(0, n_tok)], sem).wait()


def moe_dispatch(h2, dest, n_rows):
    rows, ns, nl = h2.shape
    n_t = rows // MOE_TOK
    dest3 = dest.reshape(n_t, 1, MOE_TOK * TOP_K)
    zeros = jnp.zeros((n_rows, ns, nl), F32)
    return pl.pallas_call(
        _dispatch_kernel,
        grid=(n_t,),
        in_specs=[
            pl.BlockSpec((1, 1, MOE_TOK * TOP_K), lambda i: (i, 0, 0), memory_space=pltpu.SMEM),
            pl.BlockSpec((MOE_TOK, ns, nl), lambda i: (i, 0, 0)),
            pl.BlockSpec(memory_space=pl.ANY),
        ],
        out_specs=pl.BlockSpec(memory_space=pl.ANY),
        out_shape=jax.ShapeDtypeStruct((n_rows, ns, nl), F32),
        scratch_shapes=[pltpu.SemaphoreType.DMA],
        input_output_aliases={2: 0},
        compiler_params=_cparams(("arbitrary",)),
        name="moe_dispatch",
    )(dest3, h2, zeros)


def _expert_kernel(be_ref, nv_ref, x_ref, wg_ref, wu_ref, wd_ref, o_ref, wg_s, wu_s, wd_s):
    i = pl.program_id(0)
    new_expert = jnp.logical_or(i == 0, be_ref[i] != be_ref[jnp.maximum(i - 1, 0)])

    @pl.when(new_expert)
    def _():
        wg_s[...] = wg_ref[...].astype(BF16)
        wu_s[...] = wu_ref[...].astype(BF16)
        wd_s[...] = wd_ref[...].astype(BF16)

    @pl.when(i < nv_ref[0])
    def _():
        rows = x_ref.shape[0] // MOE_SPLIT
        for part in range(MOE_SPLIT):
            xr = x_ref.at[part * rows:(part + 1) * rows]
            x = _from_row_tiles(xr).astype(BF16)
            hid = (_silu(jnp.dot(x, wg_s[...], preferred_element_type=F32))
                   * jnp.dot(x, wu_s[...], preferred_element_type=F32))
            _to_row_tiles(o_ref.at[part * rows:(part + 1) * rows],
                          jnp.dot(hid.astype(BF16), wd_s[...], preferred_element_type=F32))

    @pl.when(i >= nv_ref[0])
    def _():
        o_ref[...] = jnp.zeros_like(o_ref)


def moe_experts(x_sorted, block_e, n_valid, layer, w_gate, w_up, w_down):
    n_rows, ns, nl = x_sorted.shape
    d, de = w_gate.shape[2], w_gate.shape[3]
    n_blocks = n_rows // MOE_TILE
    grid_spec = pltpu.PrefetchScalarGridSpec(
        num_scalar_prefetch=2,
        grid=(n_blocks,),
        in_specs=[
            pl.BlockSpec((MOE_TILE, ns, nl), lambda i, be, nv: (i, 0, 0)),
            pl.BlockSpec((None, None, d, de), lambda i, be, nv: (layer, be[i], 0, 0)),
            pl.BlockSpec((None, None, d, de), lambda i, be, nv: (layer, be[i], 0, 0)),
            pl.BlockSpec((None, None, de, d), lambda i, be, nv: (layer, be[i], 0, 0)),
        ],
        out_specs=pl.BlockSpec((MOE_TILE, ns, nl), lambda i, be, nv: (i, 0, 0)),
        scratch_shapes=[pltpu.VMEM((d, de), BF16), pltpu.VMEM((d, de), BF16), pltpu.VMEM((de, d), BF16)],
    )
    return pl.pallas_call(
        _expert_kernel,
        grid_spec=grid_spec,
        out_shape=jax.ShapeDtypeStruct((n_rows, ns, nl), F32),
        compiler_params=_cparams(("arbitrary",)),
        name="moe_experts",
    )(block_e, n_valid, x_sorted, w_gate, w_up, w_down)


def _combine_kernel(dest_ref, dnext_ref, ys_hbm, w_ref, x_ref, g2_ref, gf_ref, o_ref, buf, sem, *, final_norm):
    i = pl.program_id(0)
    n_tok = buf.shape[2]

    def gather(d_ref, slot):
        def start(t, carry):
            for k in range(TOP_K):
                _tile_copy(ys_hbm.at[d_ref[0, 0, TOP_K * t + k]], buf.at[slot, k, t],
                           sem.at[slot]).start(priority=k)
            return carry

        lax.fori_loop(0, n_tok, start, 0, unroll=DMA_UNROLL)

    slot = i % 2

    @pl.when(i == 0)
    def _():
        gather(dest_ref, slot)

    @pl.when(i + 1 < pl.num_programs(0))
    def _():
        gather(dnext_ref, 1 - slot)

    for k in range(TOP_K):
        _tile_copy(ys_hbm.at[pl.ds(0, n_tok)], buf.at[slot, k], sem.at[slot]).wait()
    w = w_ref[...]
    f = w[:, 0:1] * _from_row_tiles(buf.at[slot, 0]) + w[:, 1:2] * _from_row_tiles(buf.at[slot, 1])
    xn = x_ref[...] + g2_ref[...] * f
    if final_norm:
        xn = xn * lax.rsqrt(jnp.mean(xn * xn, axis=-1, keepdims=True) + EPS) * gf_ref[...]
    o_ref[...] = xn


def moe_combine(y_sorted, dest, wts, x_new, mods, final_g, final_norm, mod_idx_tok):
    rows, d = x_new.shape
    n_t = rows // MOE_TOK
    dest3 = dest.reshape(n_t, 1, MOE_TOK * TOP_K)
    kern = functools.partial(_combine_kernel, final_norm=final_norm)
    return pl.pallas_call(
        kern,
        grid=(n_t,),
        in_specs=[
            pl.BlockSpec((1, 1, MOE_TOK * TOP_K), lambda i: (i, 0, 0), memory_space=pltpu.SMEM),
            pl.BlockSpec((1, 1, MOE_TOK * TOP_K), lambda i: (jnp.minimum(i + 1, n_t - 1), 0, 0),
                         memory_space=pltpu.SMEM),
            pl.BlockSpec(memory_space=pl.ANY),
            pl.BlockSpec((MOE_TOK, TOP_K), lambda i: (i, 0)),
            pl.BlockSpec((MOE_TOK, d), lambda i: (i, 0)),
            pl.BlockSpec((None, 1, d), lambda i: (mod_idx_tok(i), 0, 5)),
            pl.BlockSpec((1, d), lambda i: (0, 0)),
        ],
        out_specs=pl.BlockSpec((MOE_TOK, d), lambda i: (i, 0)),
        out_shape=jax.ShapeDtypeStruct((rows, d), F32),
        scratch_shapes=[pltpu.VMEM((2, TOP_K, MOE_TOK, d // LANES, LANES), F32),
                        pltpu.SemaphoreType.DMA((2,))],
        compiler_params=_cparams(("arbitrary",)),
        name="moe_combine",
    )(dest3, dest3, y_sorted, wts, x_new, mods, final_g)


def _row_plan(ids, counts, n_tok):
    counts = counts[:, 0].astype(jnp.int32)
    padded = (counts + MOE_TILE - 1) // MOE_TILE * MOE_TILE
    pad_end = jnp.cumsum(padded)
    pad_start = pad_end - padded
    experts = jnp.arange(N_EXPERTS, dtype=jnp.int32)[:, None]
    dest = [jnp.sum(jnp.where(ids[k][None, :] == experts, pad_start[:, None], 0), axis=0) + ids[TOP_K + k]
            for k in range(TOP_K)]
    dest = jnp.stack(dest, axis=1).astype(jnp.int32)
    n_blocks = -(-(n_tok * TOP_K) // MOE_TILE) + N_EXPERTS
    block_start = jnp.arange(n_blocks, dtype=jnp.int32) * MOE_TILE
    block_e = jnp.minimum(jnp.sum(pad_end[None, :] <= block_start[:, None], axis=1), N_EXPERTS - 1)
    n_valid = (pad_end[-1] // MOE_TILE).reshape(1)
    return dest, block_e.astype(jnp.int32), n_valid.astype(jnp.int32), n_blocks


def moe_layer(h2, logits_t, router_b, layer, w_gate, w_up, w_down, x_new, mods, final_g, final_norm,
              mod_idx_tok):
    n_tok = h2.shape[0]
    ids, wts8, counts = moe_router(logits_t, router_b)
    dest, block_e, n_valid, n_blocks = _row_plan(ids, counts, n_tok)
    wts = wts8[:TOP_K].T
    x_sorted = moe_dispatch(h2, dest, n_blocks * MOE_TILE)
    y_sorted = moe_experts(x_sorted, block_e, n_valid, layer, w_gate, w_up, w_down)
    return moe_combine(y_sorted, dest, wts, x_new, mods, final_g, final_norm, mod_idx_tok)


def _rope_tables(batch, t_len, c_len):
    n_rows = t_len // GRID_W
    row = jnp.repeat(jnp.arange(n_rows), GRID_W).astype(F32)
    col = jnp.tile(jnp.arange(GRID_W), n_rows).astype(F32)
    half = B_HALF // 2
    inv = 1.0 / (ROPE_BASE ** (jnp.arange(0, half, 2, dtype=F32) / half))
    ar, ac = row[:, None] * inv, col[:, None] * inv
    ang = jnp.concatenate([ar, ar, ac, ac], axis=-1)
    cos, sin = jnp.cos(ang), jnp.sin(ang)
    first = (jnp.arange(B_HALF) % 32) < 16
    sin_a = jnp.where(first, -sin, 0.0)
    sin_b = jnp.where(first, 0.0, sin)

    def full(tab, ctx_val):
        lat = jnp.tile(jnp.tile(tab, (1, LANES // B_HALF)), (batch, 1))
        ctx = jnp.full((batch * c_len, LANES), ctx_val, F32)
        return jnp.concatenate([lat, ctx], axis=0)

    return full(cos, 1.0), full(sin_a, 0.0), full(sin_b, 0.0)


def kernel(x, c, ctx, c_ctx, ada_w, ada_b, norm_mix_g, norm_ffn_g, even_w_in, even_w_out, hgrn_lb_logits,
           hgrn_onorm_g, diff_lambda, diff_subln_g, odd_w_qkv, odd_w_out, swa_sink, router_w, router_b,
           moe_w_gate, moe_w_up, moe_w_down, final_norm_g):
    batch, t_len, d = x.shape
    c_len = ctx.shape[1]
    n_lat = batch * t_len
    assert t_len % ROW_TILE == 0 and (batch * c_len) % ROW_TILE == 0 and c_len == HGRN_TILE
    assert batch < 8 and t_len % DIFF_TK == 0 and t_len % GRID_W == 0

    x_all = jnp.concatenate([x.reshape(n_lat, d), ctx.reshape(batch * c_len, d)], axis=0)
    rope = _rope_tables(batch, t_len, c_len)

    def mod_idx_for(tile):
        per_batch = t_len // tile
        return lambda i: jnp.minimum(i // per_batch, batch)

    mod_idx = mod_idx_for(ROW_TILE)
    mod_idx_tok = mod_idx_for(MOE_TOK)

    cvec = jnp.zeros((8, d), F32).at[:batch].set(c).at[batch].set(c_ctx)
    mods_all = ada_modulation(cvec, ada_w, ada_b)
    lower_bounds = jnp.cumsum(jax.nn.softmax(hgrn_lb_logits.astype(F32), axis=0), axis=0)
    router_wt = router_w.astype(F32).T

    mods = mods_all[0].reshape(8, 1, 6 * d)
    g_mix = norm_mix_g[0].reshape(1, d)
    w_in = even_w_in[0].astype(BF16)
    n_a = 5 * A_WIDTH
    p_a = norm_mod_matmul(x_all, g_mix, mods, 1, 0, w_in[:, :n_a], rope, (0, 0), F32, mod_idx)
    n_qk = n_a + 2 * B_WIDTH
    p_qk, v_t = norm_mod_matmul(x_all, g_mix, mods, 1, 0, w_in[:, n_a:n_qk], rope, (0, 2 * B_WIDTH), BF16,
                                mod_idx, wt=w_in[:, n_qk:].T)
    o_f, o_b = hgrn_scan(p_a, lower_bounds[0], batch, t_len, c_len)
    ya = hgrn_finish(o_f, o_b, p_a, hgrn_onorm_g[0].reshape(1, A_DK))
    lambda_init = 0.8 - 0.6 * math.exp(-0.3 * 0)
    yb, yb_ctx = diff_attention(p_qk, v_t, diff_lambda[0], diff_subln_g[0], lambda_init, batch, t_len, c_len)
    na = x_all.shape[0]
    x_new, h2, logits = out_proj(ya, yb, 0, even_w_out[0].astype(BF16), x_all, mods,
                                 norm_ffn_g[0].reshape(1, d), router_wt, na, mod_idx, yb_tail=yb_ctx)
    x_all = moe_layer(h2, logits, router_b, 0, moe_w_gate, moe_w_up, moe_w_down, x_new, mods,
                      final_norm_g.reshape(1, d), False, mod_idx_tok)

    mods = mods_all[1].reshape(8, 1, 6 * d)
    w_qkv = odd_w_qkv[0]
    q_cols = C_HEADS * C_HEAD_DIM
    kv_cols = C_KV_HEADS * C_HEAD_DIM

    n_qk1 = q_cols + kv_cols
    w_qkv = w_qkv.astype(BF16)
    p_qk1, v_t1 = norm_mod_matmul(x_all, norm_mix_g[1].reshape(1, d), mods, 1, 0, w_qkv[:, :n_qk1], rope,
                                  (0, n_qk1), BF16, mod_idx, tn=kv_cols, wt=w_qkv[:, n_qk1:].T)
    o1 = window_attention(p_qk1, v_t1, swa_sink[0], batch, t_len, c_len)
    x_new, h2, logits = out_proj(o1, o1, 1, odd_w_out[0].astype(BF16), x_all, mods,
                                 norm_ffn_g[1].reshape(1, d), router_wt, n_lat, mod_idx)
    out = moe_layer(h2, logits, router_b, 1, moe_w_gate, moe_w_up, moe_w_down, x_new, mods,
                    final_norm_g.reshape(1, d), True, mod_idx_tok)
    return out.reshape(batch, t_len, d)
```

```python
import functools
import math

import jax
import jax.numpy as jnp
from jax import lax
from jax.experimental import pallas as pl
from jax.experimental.pallas import tpu as pltpu

F32 = jnp.float32
BF16 = jnp.bfloat16
EPS = 1e-6
ROPE_BASE = 10000.0
GRID_W = 64

A_HEADS, A_DK = 4, 128
A_WIDTH = A_HEADS * A_DK
HGRN_CHUNK = 16
B_HEADS, B_HALF = 4, 64
B_DV = 2 * B_HALF
B_WIDTH = B_HEADS * B_DV
C_HEADS, C_KV_HEADS, C_HEAD_DIM, C_WINDOW = 16, 4, 64, 128
C_GROUP = C_HEADS // C_KV_HEADS
N_EXPERTS, N_GROUPS, TOP_K = 16, 4, 2
EXPERTS_PER_GROUP = N_EXPERTS // N_GROUPS

LANES = 128
ROW_TILE = 512
HGRN_TILE = 256
HGRN_HEADS_PER_STEP = 1
ATTN_TQ = 256
DIFF_TK = 512
DIFF_UNROLL = 4
DIFF_TQ = 512
MOE_TILE = 256
MOE_SPLIT = 1
MOE_TOK = 512
DMA_UNROLL = 8
VMEM_LIMIT = 56 * 1024 * 1024
HIGHEST = lax.Precision.HIGHEST
NEG_INF = float("-inf")
LOG2_E = math.log2(math.e)


def _cparams(sem):
    return pltpu.CompilerParams(dimension_semantics=sem, vmem_limit_bytes=VMEM_LIMIT)


def _nt_dot(a, b):
    return lax.dot_general(a, b, (((1,), (1,)), ((), ())), preferred_element_type=F32)


def _sigmoid(x):
    return 0.5 * jnp.tanh(0.5 * x) + 0.5


def _silu(x):
    return x * _sigmoid(x)


def _ada_kernel(c_ref, w_ref, b_ref, o_ref):
    s = _silu(c_ref[...])
    o_ref[...] = jnp.dot(s, w_ref[...], preferred_element_type=F32, precision=HIGHEST) + b_ref[...]


def ada_modulation(cvec, ada_w, ada_b):
    n_layers, d, n6 = ada_w.shape
    tn = 512
    return pl.pallas_call(
        _ada_kernel,
        grid=(n_layers, n6 // tn),
        in_specs=[
            pl.BlockSpec((8, d), lambda l, j: (0, 0)),
            pl.BlockSpec((None, d, tn), lambda l, j: (l, 0, j)),
            pl.BlockSpec((None, 1, tn), lambda l, j: (l, 0, j)),
        ],
        out_specs=pl.BlockSpec((None, 8, tn), lambda l, j: (l, 0, j)),
        out_shape=jax.ShapeDtypeStruct((n_layers, 8, n6), F32),
        compiler_params=_cparams(("parallel", "parallel")),
        name="ada_modulation",
    )(cvec, ada_w, ada_b.reshape(n_layers, 1, n6))


def _rope_slab(x, cos, sin_a, sin_b):
    return x * cos + pltpu.roll(x, LANES - 16, 1) * sin_a + pltpu.roll(x, 16, 1) * sin_b


def _nmm_kernel(x_ref, g_ref, sc_ref, sh_ref, w_ref, cos_ref, sa_ref, sb_ref, *rest, rope_cols, tn):
    x = x_ref[...]
    y = x * lax.rsqrt(jnp.mean(x * x, axis=-1, keepdims=True) + EPS) * g_ref[...]
    h = (y * (1.0 + sc_ref[...]) + sh_ref[...]).astype(BF16)
    if len(rest) == 3:
        wt_ref, o_ref, ot_ref = rest
        ot_ref[...] = _nt_dot(wt_ref[...], h).astype(ot_ref.dtype)
    else:
        (o_ref,) = rest
    n = w_ref.shape[1]
    for c0 in range(0, n, tn):
        acc = jnp.dot(h, w_ref[:, c0:c0 + tn], preferred_element_type=F32)
        if rope_cols[0] <= c0 < rope_cols[1]:
            cos, sa, sb = cos_ref[...], sa_ref[...], sb_ref[...]
            for c in range(tn // LANES):
                sl = slice(c * LANES, (c + 1) * LANES)
                o_ref[:, c0 + c * LANES:c0 + (c + 1) * LANES] = (
                    _rope_slab(acc[:, sl], cos, sa, sb).astype(o_ref.dtype))
        else:
            o_ref[:, c0:c0 + tn] = acc.astype(o_ref.dtype)


def norm_mod_matmul(x_all, g, mods, sc_chunk, sh_chunk, w, rope, rope_cols, out_dtype, mod_idx, tn=512,
                    wt=None):
    na, d = x_all.shape
    n = w.shape[1]
    cos, sa, sb = rope
    assert n % tn == 0 and rope_cols[0] % tn == 0 and rope_cols[1] % tn == 0
    kern = functools.partial(_nmm_kernel, rope_cols=rope_cols, tn=tn)
    row = lambda i: (i, 0)
    in_specs = [
        pl.BlockSpec((ROW_TILE, d), row),
        pl.BlockSpec((1, d), lambda i: (0, 0)),
        pl.BlockSpec((None, 1, d), lambda i: (mod_idx(i), 0, sc_chunk)),
        pl.BlockSpec((None, 1, d), lambda i: (mod_idx(i), 0, sh_chunk)),
        pl.BlockSpec((d, n), lambda i: (0, 0)),
        pl.BlockSpec((ROW_TILE, LANES), row),
        pl.BlockSpec((ROW_TILE, LANES), row),
        pl.BlockSpec((ROW_TILE, LANES), row),
    ]
    out_specs = [pl.BlockSpec((ROW_TILE, n), row)]
    out_shape = [jax.ShapeDtypeStruct((na, n), out_dtype)]
    args = [x_all, g, mods, mods, w, cos, sa, sb]
    if wt is not None:
        nt = wt.shape[0]
        in_specs.append(pl.BlockSpec((nt, d), lambda i: (0, 0)))
        out_specs.append(pl.BlockSpec((nt, ROW_TILE), lambda i: (0, i)))
        out_shape.append(jax.ShapeDtypeStruct((nt, na), out_dtype))
        args.append(wt)
    outs = pl.pallas_call(
        kern,
        grid=(na // ROW_TILE,),
        in_specs=in_specs,
        out_specs=out_specs,
        out_shape=out_shape,
        compiler_params=_cparams(("parallel",)),
        name="norm_mod_matmul",
    )(*args)
    return outs[0] if wt is None else outs


def _hgrn_chunk_matrices(tt):
    r = lax.broadcasted_iota(jnp.int32, (tt, tt), 0)
    c = lax.broadcasted_iota(jnp.int32, (tt, tt), 1)
    same = (r // HGRN_CHUNK) == (c // HGRN_CHUNK)
    mats = [jnp.logical_and(same, c <= r), jnp.logical_and(same, c >= r), same]
    return jnp.stack(mats).astype(BF16)


def _hgrn_direction(q_raw, f_raw, v, lb_row, st, m_cum, m_tot, reverse):
    tt = q_raw.shape[0]
    n_chunks = tt // HGRN_CHUNK
    qs = _silu(q_raw) * (A_DK ** -0.5)
    f = lb_row + (1.0 - lb_row) * _sigmoid(f_raw)
    kk = 1.0 - f
    lf = jnp.log(f)

    hi = lf.astype(BF16)
    r1 = lf - hi.astype(F32)
    mid = r1.astype(BF16)
    lo = (r1 - mid.astype(F32)).astype(BF16)

    def split_dot(m):
        return (jnp.dot(m, hi, preferred_element_type=F32) + jnp.dot(m, mid, preferred_element_type=F32)
                + jnp.dot(m, lo, preferred_element_type=F32))

    cum = split_dot(m_cum)
    tot = split_dot(m_tot)
    q_in = qs * jnp.exp(cum)
    k_out = kk * jnp.exp(tot - cum)

    half = HGRN_CHUNK // 2
    assert half == 8

    def halves(x):
        x4 = x.reshape(n_chunks, 2, half, A_DK)
        return (x4[:, 1], x4[:, 0]) if reverse else (x4[:, 0], x4[:, 1])

    def rot(x, d):
        if d == 0:
            return x
        return pltpu.roll(x, (half - d) if reverse else d, 1)

    sub = lax.broadcasted_iota(jnp.int32, (1, half, 1), 1)
    if reverse:
        sub = half - 1 - sub

    def contrib(q_t, c_t, k_s, c_s, v_s, valid):
        diff = c_t - c_s
        if valid is not None:
            diff = jnp.where(valid, diff, NEG_INF)
        return jnp.sum(q_t * k_s * jnp.exp(diff), axis=-1, keepdims=True) * v_s

    (q_n, q_f), (k_n, k_f), (c_n, c_f), (v_n, v_f) = halves(qs), halves(kk), halves(cum), halves(v)
    o_n = jnp.sum(q_n * k_n, axis=-1, keepdims=True) * v_n
    o_f = jnp.sum(q_f * k_f, axis=-1, keepdims=True) * v_f
    for dist in range(1, half):
        ok = sub >= dist
        kn, cn, vn = rot(k_n, dist), rot(c_n, dist), rot(v_n, dist)
        kf, cf, vf = rot(k_f, dist), rot(c_f, dist), rot(v_f, dist)
        o_n = o_n + contrib(q_n, c_n, kn, cn, vn, ok)
        o_f = o_f + contrib(q_f, c_f, jnp.where(ok, kf, kn), jnp.where(ok, cf, cn), jnp.where(ok, vf, vn), None)
    for dist in range(half, HGRN_CHUNK):
        e = dist - half
        o_f = o_f + contrib(q_f, c_f, rot(k_n, e), rot(c_n, e), rot(v_n, e), sub >= e)
    o = jnp.stack([o_f, o_n] if reverse else [o_n, o_f], axis=1).reshape(tt, A_DK)

    v_t = v.T.astype(BF16)
    q_in_t = q_in.T.astype(BF16)
    k_out_b = k_out.astype(BF16)
    lane_chunk = lax.broadcasted_iota(jnp.int32, (1, tt), 1) // HGRN_CHUNK
    updates = [jnp.dot(v_t * (lane_chunk == ci).astype(BF16), k_out_b, preferred_element_type=F32)
               for ci in range(n_chunks)]
    o_t = jnp.zeros((A_DK, tt), F32)
    order = range(n_chunks - 1, -1, -1) if reverse else range(n_chunks)
    for ci in order:
        lo_r = ci * HGRN_CHUNK
        from_state = jnp.dot(st.astype(BF16), q_in_t, preferred_element_type=F32)
        o_t = jnp.where(lane_chunk == ci, from_state, o_t)
        st = st * jnp.exp(tot[lo_r:lo_r + 1]) + updates[ci]
    return o + o_t.T, st


def _hgrn_kernel(qf_ref, ff_ref, vf_ref, qb_ref, fb_ref, vb_ref, lb_ref, mats_ref, of_ref, ob_ref, st_scr):
    @pl.when(pl.program_id(2) == 0)
    def _():
        st_scr[...] = jnp.zeros_like(st_scr)

    for hh in range(HGRN_HEADS_PER_STEP):
        sl = slice(hh * A_DK, (hh + 1) * A_DK)
        o_f, st_f = _hgrn_direction(qf_ref[:, sl], ff_ref[:, sl], vf_ref[:, sl], lb_ref[0:1, sl],
                                    st_scr[2 * hh], mats_ref[0], mats_ref[2], False)
        of_ref[:, sl] = o_f
        st_scr[2 * hh] = st_f
        o_b, st_b = _hgrn_direction(qb_ref[:, sl], fb_ref[:, sl], vb_ref[:, sl], lb_ref[1:2, sl],
                                    st_scr[2 * hh + 1], mats_ref[1], mats_ref[2], True)
        ob_ref[:, sl] = o_b
        st_scr[2 * hh + 1] = st_b


def hgrn_scan(p_a, lb, batch, t_len, c_len):
    na = p_a.shape[0]
    tt = HGRN_TILE
    n_c, n_t = c_len // tt, t_len // tt
    off_c = batch * n_t

    def rb_f(b, s):
        return jnp.where(s < n_c, off_c + b * n_c + s, b * n_t + (s - n_c))

    def rb_b(b, s):
        return jnp.where(s < n_c, off_c + b * n_c + (n_c - 1 - s), b * n_t + (n_t - 1 - (s - n_c)))

    hps = HGRN_HEADS_PER_STEP
    n_hb = A_HEADS // hps

    def spec(rb, group):
        return pl.BlockSpec((tt, hps * A_DK), lambda b, h, s: (rb(b, s), group * n_hb + h))

    out_sds = jax.ShapeDtypeStruct((na, A_WIDTH), F32)
    return pl.pallas_call(
        _hgrn_kernel,
        grid=(batch, n_hb, n_c + n_t),
        in_specs=[spec(rb_f, 0), spec(rb_f, 1), spec(rb_f, 3),
                  spec(rb_b, 0), spec(rb_b, 2), spec(rb_b, 3),
                  pl.BlockSpec((2, hps * A_DK), lambda b, h, s: (0, h)),
                  pl.BlockSpec((3, tt, tt), lambda b, h, s: (0, 0, 0))],
        out_specs=[spec(rb_f, 0), spec(rb_b, 0)],
        out_shape=[out_sds, out_sds],
        scratch_shapes=[pltpu.VMEM((2 * hps, A_DK, A_DK), F32)],
        compiler_params=_cparams(("parallel", "parallel", "arbitrary")),
        name="hgrn_scan",
    )(p_a, p_a, p_a, p_a, p_a, p_a, lb, _hgrn_chunk_matrices(tt))


def _diff_attn_kernel(q_ref, kc_ref, vtc_ref, k_ref, vt_ref, lam_ref, gn_ref, o_ref,
                      m_scr, l_scr, acc_scr, *, n_k, unroll, lambda_init):
    q_t = (q_ref[...].astype(F32) * (B_HALF ** -0.5 * LOG2_E)).T.astype(BF16)
    tq = q_t.shape[1]
    sub = lax.broadcasted_iota(jnp.int32, (B_DV, 1), 0)
    zero = jnp.zeros_like(q_t)
    qm = jnp.concatenate([jnp.where(sub < B_HALF, q_t, zero), jnp.where(sub >= B_HALF, q_t, zero)], axis=1)
    nc = 2 * tq

    m_scr[...] = jnp.full_like(m_scr, NEG_INF)
    l_scr[...] = jnp.zeros_like(l_scr)
    acc_scr[...] = jnp.zeros_like(acc_scr)

    def fold8(s, op):
        return op(s.reshape(s.shape[0] // 8, 8, nc), axis=0)

    def update(kvs):
        ss = [jnp.dot(kb, qm, preferred_element_type=F32) for kb, _ in kvs]
        part = functools.reduce(jnp.maximum, [fold8(s, jnp.max) for s in ss])
        m_old = m_scr[0:1, :]
        m_new = jnp.maximum(m_old, jnp.max(part, axis=0, keepdims=True))
        alpha = jnp.exp2(m_old - m_new)
        lsum = alpha * l_scr[...]
        acc = alpha * acc_scr[...]
        for s, (_, vtb) in zip(ss, kvs):
            p = jnp.exp2(s - m_new)
            lsum = lsum + fold8(p, jnp.sum)
            acc = acc + jnp.dot(vtb, p.astype(BF16), preferred_element_type=F32)
        l_scr[...] = lsum
        acc_scr[...] = acc
        m_scr[...] = jnp.broadcast_to(m_new, (8, nc))

    update([(kc_ref[...], vtc_ref[...])])
    for j in range(n_k):
        kvs = []
        for u in range(unroll):
            start = (j * unroll + u) * DIFF_TK
            kvs.append((k_ref[start:start + DIFF_TK, :], vt_ref[:, start:start + DIFF_TK]))
        update(kvs)

    lp = lam_ref[...]
    lam = (jnp.exp(jnp.sum(lp[0:1] * lp[1:2], axis=-1, keepdims=True))
           - jnp.exp(jnp.sum(lp[2:3] * lp[3:4], axis=-1, keepdims=True)) + lambda_init)
    on = acc_scr[...] / jnp.sum(l_scr[...], axis=0, keepdims=True)
    o = on[:, :tq] - lam * on[:, tq:]
    y = o * lax.rsqrt(jnp.mean(o * o, axis=0, keepdims=True) + EPS) * gn_ref[...]
    o_ref[...] = (y * (1.0 - lambda_init)).T.astype(o_ref.dtype)


def diff_attention(p_qk, v_t, lam_p, subln_g, lambda_init, batch, t_len, c_len):
    off_c = batch * (t_len // c_len)
    unroll = min(DIFF_UNROLL, t_len // DIFF_TK)
    assert t_len % (unroll * DIFF_TK) == 0 and t_len % DIFF_TQ == 0

    def call(tq, n_q, q_off, n_k):
        kern = functools.partial(_diff_attn_kernel, n_k=n_k, unroll=unroll, lambda_init=lambda_init)
        in_specs = [
            pl.BlockSpec((tq, B_DV), lambda b, h, i: (q_off + b * n_q + i, h)),
            pl.BlockSpec((c_len, B_DV), lambda b, h, i: (off_c + b, B_HEADS + h)),
            pl.BlockSpec((B_DV, c_len), lambda b, h, i: (h, off_c + b)),
            pl.BlockSpec((t_len, B_DV), lambda b, h, i: (b, B_HEADS + h)),
            pl.BlockSpec((B_DV, t_len), lambda b, h, i: (h, b)),
            pl.BlockSpec((4, B_HALF), lambda b, h, i: (0, 0)),
            pl.BlockSpec((B_DV, 1), lambda b, h, i: (0, 0)),
        ]
        args = [p_qk, p_qk, v_t, p_qk, v_t, lam_p, subln_g.reshape(B_DV, 1)]
        return pl.pallas_call(
            kern,
            grid=(batch, B_HEADS, n_q),
            in_specs=in_specs,
            out_specs=pl.BlockSpec((tq, B_DV), lambda b, h, i: (b * n_q + i, h)),
            out_shape=jax.ShapeDtypeStruct((batch * n_q * tq, B_WIDTH), BF16),
            scratch_shapes=[pltpu.VMEM((8, 2 * tq), F32), pltpu.VMEM((8, 2 * tq), F32),
                            pltpu.VMEM((B_DV, 2 * tq), F32)],
            compiler_params=_cparams(("parallel", "parallel", "arbitrary")),
            name="diff_attention",
        )(*args)

    y_lat = call(DIFF_TQ, t_len // DIFF_TQ, 0, t_len // (unroll * DIFF_TK))
    y_ctx = call(c_len, 1, off_c, 0)
    return y_lat, y_ctx


def _swa_kernel(q_ref, kc_ref, vtc_ref, k_ref, *rest, t_len, n_vt):
    vt_refs, (sink_ref, o_ref) = rest[:n_vt], rest[n_vt:]
    i = pl.program_id(1)
    tq = q_ref.shape[0]
    kvw = C_KV_HEADS * C_HEAD_DIM
    win = tq + 2 * C_WINDOW
    a = i * tq
    start = pl.multiple_of(jnp.clip(a - C_WINDOW, 0, t_len - win), C_WINDOW)
    kw = k_ref[pl.ds(start, win), :]
    vtw = jnp.concatenate([r[...] for r in vt_refs], axis=1)
    kc, vtc = kc_ref[...], vtc_ref[...]
    kpos = start + lax.broadcasted_iota(jnp.int32, (win, 1), 0)
    qpos = a + lax.broadcasted_iota(jnp.int32, (1, tq), 1)
    valid = jnp.abs(kpos - qpos) <= C_WINDOW
    q_t = (q_ref[...].astype(F32) * (C_HEAD_DIM ** -0.5)).T.astype(BF16)
    valid_g = jnp.concatenate([valid] * C_GROUP, axis=1)
    for g in range(C_KV_HEADS):
        lo, hi = g * C_HEAD_DIM, (g + 1) * C_HEAD_DIM
        heads = range(g * C_GROUP, (g + 1) * C_GROUP)
        q_g = jnp.concatenate([q_t[h * C_HEAD_DIM:(h + 1) * C_HEAD_DIM] for h in heads], axis=1)
        pieces = []
        if lo:
            pieces.append(jnp.zeros((lo, C_GROUP * tq), BF16))
        pieces.append(q_g)
        if kvw - hi:
            pieces.append(jnp.zeros((kvw - hi, C_GROUP * tq), BF16))
        qm = jnp.concatenate(pieces, axis=0)
        sink = jnp.concatenate([jnp.broadcast_to(sink_ref[h:h + 1, :], (1, tq)) for h in heads], axis=1)
        s_c = jnp.dot(kc, qm, preferred_element_type=F32)
        s_w = jnp.where(valid_g, jnp.dot(kw, qm, preferred_element_type=F32), NEG_INF)
        m = jnp.maximum(jnp.maximum(jnp.max(s_c, axis=0, keepdims=True),
                                    jnp.max(s_w, axis=0, keepdims=True)), sink)
        p_c = jnp.exp(s_c - m)
        p_w = jnp.exp(s_w - m)
        den = (jnp.exp(sink - m) + jnp.sum(p_c, axis=0, keepdims=True)
               + jnp.sum(p_w, axis=0, keepdims=True))
        o_t = (jnp.dot(vtc[lo:hi], p_c.astype(BF16), preferred_element_type=F32)
               + jnp.dot(vtw[lo:hi], p_w.astype(BF16), preferred_element_type=F32)) / den
        per_slab = LANES // C_HEAD_DIM
        for s in range(C_GROUP // per_slab):
            slab = jnp.concatenate([o_t[:, (s * per_slab + j) * tq:(s * per_slab + j + 1) * tq]
                                    for j in range(per_slab)], axis=0)
            c0 = (g * C_GROUP + s * per_slab) * C_HEAD_DIM
            o_ref[:, c0:c0 + LANES] = slab.T.astype(o_ref.dtype)


def window_attention(p_qk, v_t, sink, batch, t_len, c_len):
    tq = ATTN_TQ
    qw = C_HEADS * C_HEAD_DIM
    kvw = C_KV_HEADS * C_HEAD_DIM
    assert qw % kvw == 0
    k_col = qw // kvw
    n_q = t_len // tq
    off_c = batch * (t_len // c_len)
    win = tq + 2 * C_WINDOW
    n_vt = win // C_WINDOW
    per_b = t_len // C_WINDOW

    def vt_blk(j):
        def index(b, i):
            first = jnp.clip(i * (tq // C_WINDOW) - 1, 0, per_b - n_vt)
            return (0, b * per_b + first + j)
        return pl.BlockSpec((kvw, C_WINDOW), index)

    kern = functools.partial(_swa_kernel, t_len=t_len, n_vt=n_vt)
    return pl.pallas_call(
        kern,
        grid=(batch, n_q),
        in_specs=[
            pl.BlockSpec((tq, qw), lambda b, i: (b * n_q + i, 0)),
            pl.BlockSpec((c_len, kvw), lambda b, i: (off_c + b, k_col)),
            pl.BlockSpec((kvw, c_len), lambda b, i: (0, off_c + b)),
            pl.BlockSpec((t_len, kvw), lambda b, i: (b, k_col)),
            *[vt_blk(j) for j in range(n_vt)],
            pl.BlockSpec((C_HEADS, 1), lambda b, i: (0, 0)),
        ],
        out_specs=pl.BlockSpec((tq, qw), lambda b, i: (b * n_q + i, 0)),
        out_shape=jax.ShapeDtypeStruct((batch * t_len, qw), BF16),
        compiler_params=_cparams(("parallel", "arbitrary")),
        name="window_attention",
    )(p_qk, p_qk, v_t, p_qk, *([v_t] * n_vt), sink.astype(F32).reshape(C_HEADS, 1))


def _to_row_tiles(ref, val):
    ref[...] = val.reshape(ref.shape)


def _from_row_tiles(ref):
    x = ref[...]
    return x.reshape(x.shape[0], x.shape[1] * x.shape[2])


def _hgrn_gate_norm(o, gate_raw, gn):
    gate = _silu(gate_raw)
    parts = []
    for h in range(A_HEADS):
        sl = slice(h * A_DK, (h + 1) * A_DK)
        oh = o[:, sl]
        y = oh * lax.rsqrt(jnp.mean(oh * oh, axis=-1, keepdims=True) + EPS) * gn
        parts.append((y * gate[:, sl]).astype(BF16))
    return jnp.concatenate(parts, axis=1)


def _out_proj_kernel(*refs, n_main, from_hgrn, has_tail):
    refs = list(refs)
    if from_hgrn:
        of_ref, ob_ref, gate_ref, gn_ref = refs[:4]
        ya = _hgrn_gate_norm(of_ref[...] + ob_ref[...], gate_ref[...], gn_ref[...])
        refs = refs[4:]
    else:
        ya = refs[0][...]
        refs = refs[1:]
    yb_ref, wa_ref, wb_ref, x_ref, g1_ref, g_ref, sc_ref, sh_ref, rw_ref = refs[:9]
    if has_tail:
        yt_ref, xo_ref, h_ref, lg_ref = refs[9:]
        yb = jnp.where(pl.program_id(0) < n_main, yb_ref[...], yt_ref[...])
    else:
        xo_ref, h_ref, lg_ref = refs[9:]
        yb = yb_ref[...]
    y = (jnp.dot(ya, wa_ref[...], preferred_element_type=F32)
         + jnp.dot(yb, wb_ref[...], preferred_element_type=F32))
    xn = x_ref[...] + g1_ref[...] * y
    xo_ref[...] = xn
    hn = xn * lax.rsqrt(jnp.mean(xn * xn, axis=-1, keepdims=True) + EPS) * g_ref[...]
    h2 = hn * (1.0 + sc_ref[...]) + sh_ref[...]
    _to_row_tiles(h_ref, h2)
    lg_ref[...] = lax.dot_general(rw_ref[...], h2, (((1,), (1,)), ((), ())),
                                  preferred_element_type=F32, precision=HIGHEST)


def out_proj(ya, yb_arr, yb_col, w_out, x_all, mods, g2n, router_wt, rows, mod_idx, yb_tail=None):
    d = x_all.shape[1]
    half = d // 2
    ne = router_wt.shape[0]
    row = lambda i: (i, 0)
    n_main = yb_arr.shape[0] // ROW_TILE
    if isinstance(ya, tuple):
        o_f, o_b, p_a, onorm_g = ya
        first = [pl.BlockSpec((ROW_TILE, half), row), pl.BlockSpec((ROW_TILE, half), row),
                 pl.BlockSpec((ROW_TILE, half), lambda i: (i, 4)), pl.BlockSpec((1, A_DK), lambda i: (0, 0))]
        first_args = [o_f, o_b, p_a, onorm_g]
    else:
        first, first_args = [pl.BlockSpec((ROW_TILE, half), row)], [ya]
    in_specs = first + [
        pl.BlockSpec((ROW_TILE, half), lambda i: (jnp.minimum(i, n_main - 1), yb_col)),
        pl.BlockSpec((half, d), lambda i: (0, 0)),
        pl.BlockSpec((half, d), lambda i: (1, 0)),
        pl.BlockSpec((ROW_TILE, d), row),
        pl.BlockSpec((None, 1, d), lambda i: (mod_idx(i), 0, 2)),
        pl.BlockSpec((1, d), lambda i: (0, 0)),
        pl.BlockSpec((None, 1, d), lambda i: (mod_idx(i), 0, 4)),
        pl.BlockSpec((None, 1, d), lambda i: (mod_idx(i), 0, 3)),
        pl.BlockSpec((ne, d), lambda i: (0, 0)),
    ]
    args = first_args + [yb_arr, w_out, w_out, x_all, mods, g2n, mods, mods, router_wt]
    if yb_tail is not None:
        in_specs.append(pl.BlockSpec((ROW_TILE, half), lambda i: (jnp.maximum(i - n_main, 0), 0)))
        args.append(yb_tail)
    return pl.pallas_call(
        functools.partial(_out_proj_kernel, n_main=n_main, from_hgrn=isinstance(ya, tuple),
                          has_tail=yb_tail is not None),
        grid=(rows // ROW_TILE,),
        in_specs=in_specs,
        out_specs=[pl.BlockSpec((ROW_TILE, d), row),
                   pl.BlockSpec((ROW_TILE, d // LANES, LANES), lambda i: (i, 0, 0)),
                   pl.BlockSpec((ne, ROW_TILE), lambda i: (0, i))],
        out_shape=[jax.ShapeDtypeStruct((rows, d), F32),
                   jax.ShapeDtypeStruct((rows, d // LANES, LANES), F32),
                   jax.ShapeDtypeStruct((ne, rows), F32)],
        compiler_params=_cparams(("parallel",)),
        name="out_proj",
    )(*args)


def _router_kernel(lg_ref, rb_ref, ids_ref, w_ref, cnt_ref, carry):
    @pl.when(pl.program_id(0) == 0)
    def _():
        carry[...] = jnp.zeros_like(carry)

    tr = lg_ref.shape[1]
    aff = jax.nn.sigmoid(lg_ref[...])
    sel = aff + rb_ref[...]
    rows = [sel[e:e + 1] for e in range(N_EXPERTS)]

    def beats(a, b, a_first):
        return jnp.logical_or(a > b, jnp.logical_and(a == b, a_first))

    def rank_among(vals, j):
        r = jnp.zeros(vals[0].shape, jnp.int32)
        for i2 in range(len(vals)):
            if i2 != j:
                r = r + beats(vals[i2], vals[j], i2 < j).astype(jnp.int32)
        return r

    rank, gscore = [], []
    for g in range(N_GROUPS):
        grp_rows = rows[g * EXPERTS_PER_GROUP:(g + 1) * EXPERTS_PER_GROUP]
        grp_rank = [rank_among(grp_rows, j) for j in range(EXPERTS_PER_GROUP)]
        rank += grp_rank
        gscore.append(sum(jnp.where(grp_rank[j] < TOP_K, grp_rows[j], 0.0) for j in range(EXPERTS_PER_GROUP)))
    chosen = [rank_among(gscore, g) == 0 for g in range(N_GROUPS)]
    onehot = []
    for k in range(TOP_K):
        hot_rows = [jnp.logical_and(chosen[e // EXPERTS_PER_GROUP], rank[e] == k).astype(F32)
                    for e in range(N_EXPERTS)]
        onehot.append(jnp.concatenate(hot_rows, axis=0))

    e_idx = lax.broadcasted_iota(jnp.int32, (N_EXPERTS, 1), 0).astype(F32)
    picked = [jnp.sum(onehot[k] * aff, axis=0, keepdims=True) for k in range(TOP_K)]
    denom = picked[0] + picked[1]
    before = (lax.broadcasted_iota(jnp.int32, (tr, tr), 0)
              < lax.broadcasted_iota(jnp.int32, (tr, tr), 1)).astype(BF16)
    both = onehot[0] + onehot[1]
    seen = carry[:, 0:1] + jnp.dot(both.astype(BF16), before, preferred_element_type=F32)
    id_rows = [jnp.sum(onehot[k] * e_idx, axis=0, keepdims=True) for k in range(TOP_K)]
    id_rows += [jnp.sum(onehot[k] * seen, axis=0, keepdims=True) for k in range(TOP_K)]
    pad = jnp.zeros((8 - 2 * TOP_K, tr), F32)
    ids_ref[...] = jnp.concatenate(id_rows + [pad], axis=0).astype(jnp.int32)
    w_ref[...] = jnp.concatenate([picked[0] / denom, picked[1] / denom,
                                  jnp.zeros((8 - TOP_K, tr), F32)], axis=0)
    carry[...] = carry[...] + jnp.sum(both, axis=1, keepdims=True)
    cnt_ref[...] = carry[...]


def moe_router(logits_t, router_b):
    ne, rows = logits_t.shape
    tr = ROW_TILE
    return pl.pallas_call(
        _router_kernel,
        grid=(rows // tr,),
        in_specs=[pl.BlockSpec((ne, tr), lambda i: (0, i)), pl.BlockSpec((ne, 1), lambda i: (0, 0))],
        out_specs=[pl.BlockSpec((8, tr), lambda i: (0, i)), pl.BlockSpec((8, tr), lambda i: (0, i)),
                   pl.BlockSpec((ne, LANES), lambda i: (0, 0))],
        out_shape=[jax.ShapeDtypeStruct((8, rows), jnp.int32), jax.ShapeDtypeStruct((8, rows), F32),
                   jax.ShapeDtypeStruct((ne, LANES), F32)],
        scratch_shapes=[pltpu.VMEM((ne, LANES), F32)],
        compiler_params=_cparams(("arbitrary",)),
        name="moe_router",
    )(logits_t, router_b.astype(F32).reshape(ne, 1))


def _tile_copy(src, dst, sem):
    return pltpu.make_async_copy(src, dst, sem)


def _dispatch_kernel(dest_ref, h_ref, zero_hbm, xs_hbm, sem):
    del zero_hbm
    n_tok = h_ref.shape[0]

    def start(t, carry):
        for k in range(TOP_K):
            _tile_copy(h_ref.at[t], xs_hbm.at[dest_ref[0, 0, TOP_K * t + k]], sem).start(priority=k)
        return carry

    lax.fori_loop(0, n_tok, start, 0, unroll=DMA_UNROLL)
    for k in range(TOP_K):
        _tile_copy(h_ref, xs_hbm.at[pl.ds(0, n_tok)], sem).wait()


def moe_dispatch(h2, dest, n_rows):
    rows, ns, nl = h2.shape
    n_t = rows // MOE_TOK
    dest3 = dest.reshape(n_t, 1, MOE_TOK * TOP_K)
    zeros = jnp.zeros((n_rows, ns, nl), F32)
    return pl.pallas_call(
        _dispatch_kernel,
        grid=(n_t,),
        in_specs=[
            pl.BlockSpec((1, 1, MOE_TOK * TOP_K), lambda i: (i, 0, 0), memory_space=pltpu.SMEM),
            pl.BlockSpec((MOE_TOK, ns, nl), lambda i: (i, 0, 0)),
            pl.BlockSpec(memory_space=pl.ANY),
        ],
        out_specs=pl.BlockSpec(memory_space=pl.ANY),
        out_shape=jax.ShapeDtypeStruct((n_rows, ns, nl), F32),
        scratch_shapes=[pltpu.SemaphoreType.DMA],
        input_output_aliases={2: 0},
        compiler_params=_cparams(("arbitrary",)),
        name="moe_dispatch",
    )(dest3, h2, zeros)


def _expert_kernel(be_ref, nv_ref, x_ref, wg_ref, wu_ref, wd_ref, o_ref, wg_s, wu_s, wd_s):
    i = pl.program_id(0)
    new_expert = jnp.logical_or(i == 0, be_ref[i] != be_ref[jnp.maximum(i - 1, 0)])

    @pl.when(new_expert)
    def _():
        wg_s[...] = wg_ref[...].astype(BF16)
        wu_s[...] = wu_ref[...].astype(BF16)
        wd_s[...] = wd_ref[...].astype(BF16)

    @pl.when(i < nv_ref[0])
    def _():
        rows = x_ref.shape[0] // MOE_SPLIT
        for part in range(MOE_SPLIT):
            xr = x_ref.at[part * rows:(part + 1) * rows]
            x = _from_row_tiles(xr).astype(BF16)
            hid = (_silu(jnp.dot(x, wg_s[...], preferred_element_type=F32))
                   * jnp.dot(x, wu_s[...], preferred_element_type=F32))
            _to_row_tiles(o_ref.at[part * rows:(part + 1) * rows],
                          jnp.dot(hid.astype(BF16), wd_s[...], preferred_element_type=F32))

    @pl.when(i >= nv_ref[0])
    def _():
        o_ref[...] = jnp.zeros_like(o_ref)


def moe_experts(x_sorted, block_e, n_valid, layer, w_gate, w_up, w_down):
    n_rows, ns, nl = x_sorted.shape
    d, de = w_gate.shape[2], w_gate.shape[3]
    n_blocks = n_rows // MOE_TILE
    grid_spec = pltpu.PrefetchScalarGridSpec(
        num_scalar_prefetch=2,
        grid=(n_blocks,),
        in_specs=[
            pl.BlockSpec((MOE_TILE, ns, nl), lambda i, be, nv: (i, 0, 0)),
            pl.BlockSpec((None, None, d, de), lambda i, be, nv: (layer, be[i], 0, 0)),
            pl.BlockSpec((None, None, d, de), lambda i, be, nv: (layer, be[i], 0, 0)),
            pl.BlockSpec((None, None, de, d), lambda i, be, nv: (layer, be[i], 0, 0)),
        ],
        out_specs=pl.BlockSpec((MOE_TILE, ns, nl), lambda i, be, nv: (i, 0, 0)),
        scratch_shapes=[pltpu.VMEM((d, de), BF16), pltpu.VMEM((d, de), BF16), pltpu.VMEM((de, d), BF16)],
    )
    return pl.pallas_call(
        _expert_kernel,
        grid_spec=grid_spec,
        out_shape=jax.ShapeDtypeStruct((n_rows, ns, nl), F32),
        compiler_params=_cparams(("arbitrary",)),
        name="moe_experts",
    )(block_e, n_valid, x_sorted, w_gate, w_up, w_down)


def _combine_kernel(dest_ref, dnext_ref, ys_hbm, w_ref, x_ref, g2_ref, gf_ref, o_ref, buf, sem, *, final_norm):
    i = pl.program_id(0)
    n_tok = buf.shape[2]

    def gather(d_ref, slot):
        def start(t, carry):
            for k in range(TOP_K):
                _tile_copy(ys_hbm.at[d_ref[0, 0, TOP_K * t + k]], buf.at[slot, k, t],
                           sem.at[slot]).start(priority=k)
            return carry

        lax.fori_loop(0, n_tok, start, 0, unroll=DMA_UNROLL)

    slot = i % 2

    @pl.when(i == 0)
    def _():
        gather(dest_ref, slot)

    @pl.when(i + 1 < pl.num_programs(0))
    def _():
        gather(dnext_ref, 1 - slot)

    for k in range(TOP_K):
        _tile_copy(ys_hbm.at[pl.ds(0, n_tok)], buf.at[slot, k], sem.at[slot]).wait()
    w = w_ref[...]
    f = w[:, 0:1] * _from_row_tiles(buf.at[slot, 0]) + w[:, 1:2] * _from_row_tiles(buf.at[slot, 1])
    xn = x_ref[...] + g2_ref[...] * f
    if final_norm:
        xn = xn * lax.rsqrt(jnp.mean(xn * xn, axis=-1, keepdims=True) + EPS) * gf_ref[...]
    o_ref[...] = xn


def moe_combine(y_sorted, dest, wts, x_new, mods, final_g, final_norm, mod_idx_tok):
    rows, d = x_new.shape
    n_t = rows // MOE_TOK
    dest3 = dest.reshape(n_t, 1, MOE_TOK * TOP_K)
    kern = functools.partial(_combine_kernel, final_norm=final_norm)
    return pl.pallas_call(
        kern,
        grid=(n_t,),
        in_specs=[
            pl.BlockSpec((1, 1, MOE_TOK * TOP_K), lambda i: (i, 0, 0), memory_space=pltpu.SMEM),
            pl.BlockSpec((1, 1, MOE_TOK * TOP_K), lambda i: (jnp.minimum(i + 1, n_t - 1), 0, 0),
                         memory_space=pltpu.SMEM),
            pl.BlockSpec(memory_space=pl.ANY),
            pl.BlockSpec((MOE_TOK, TOP_K), lambda i: (i, 0)),
            pl.BlockSpec((MOE_TOK, d), lambda i: (i, 0)),
            pl.BlockSpec((None, 1, d), lambda i: (mod_idx_tok(i), 0, 5)),
            pl.BlockSpec((1, d), lambda i: (0, 0)),
        ],
        out_specs=pl.BlockSpec((MOE_TOK, d), lambda i: (i, 0)),
        out_shape=jax.ShapeDtypeStruct((rows, d), F32),
        scratch_shapes=[pltpu.VMEM((2, TOP_K, MOE_TOK, d // LANES, LANES), F32),
                        pltpu.SemaphoreType.DMA((2,))],
        compiler_params=_cparams(("arbitrary",)),
        name="moe_combine",
    )(dest3, dest3, y_sorted, wts, x_new, mods, final_g)


def _row_plan(ids, counts, n_tok):
    counts = counts[:, 0].astype(jnp.int32)
    padded = (counts + MOE_TILE - 1) // MOE_TILE * MOE_TILE
    pad_end = jnp.cumsum(padded)
    pad_start = pad_end - padded
    experts = jnp.arange(N_EXPERTS, dtype=jnp.int32)[:, None]
    dest = [jnp.sum(jnp.where(ids[k][None, :] == experts, pad_start[:, None], 0), axis=0) + ids[TOP_K + k]
            for k in range(TOP_K)]
    dest = jnp.stack(dest, axis=1).astype(jnp.int32)
    n_blocks = -(-(n_tok * TOP_K) // MOE_TILE) + N_EXPERTS
    block_start = jnp.arange(n_blocks, dtype=jnp.int32) * MOE_TILE
    block_e = jnp.minimum(jnp.sum(pad_end[None, :] <= block_start[:, None], axis=1), N_EXPERTS - 1)
    n_valid = (pad_end[-1] // MOE_TILE).reshape(1)
    return dest, block_e.astype(jnp.int32), n_valid.astype(jnp.int32), n_blocks


def moe_layer(h2, logits_t, router_b, layer, w_gate, w_up, w_down, x_new, mods, final_g, final_norm,
              mod_idx_tok):
    n_tok = h2.shape[0]
    ids, wts8, counts = moe_router(logits_t, router_b)
    dest, block_e, n_valid, n_blocks = _row_plan(ids, counts, n_tok)
    wts = wts8[:TOP_K].T
    x_sorted = moe_dispatch(h2, dest, n_blocks * MOE_TILE)
    y_sorted = moe_experts(x_sorted, block_e, n_valid, layer, w_gate, w_up, w_down)
    return moe_combine(y_sorted, dest, wts, x_new, mods, final_g, final_norm, mod_idx_tok)


def _rope_tables(batch, t_len, c_len):
    n_rows = t_len // GRID_W
    row = jnp.repeat(jnp.arange(n_rows), GRID_W).astype(F32)
    col = jnp.tile(jnp.arange(GRID_W), n_rows).astype(F32)
    half = B_HALF // 2
    inv = 1.0 / (ROPE_BASE ** (jnp.arange(0, half, 2, dtype=F32) / half))
    ar, ac = row[:, None] * inv, col[:, None] * inv
    ang = jnp.concatenate([ar, ar, ac, ac], axis=-1)
    cos, sin = jnp.cos(ang), jnp.sin(ang)
    first = (jnp.arange(B_HALF) % 32) < 16
    sin_a = jnp.where(first, -sin, 0.0)
    sin_b = jnp.where(first, 0.0, sin)

    def full(tab, ctx_val):
        lat = jnp.tile(jnp.tile(tab, (1, LANES // B_HALF)), (batch, 1))
        ctx = jnp.full((batch * c_len, LANES), ctx_val, F32)
        return jnp.concatenate([lat, ctx], axis=0)

    return full(cos, 1.0), full(sin_a, 0.0), full(sin_b, 0.0)


def kernel(x, c, ctx, c_ctx, ada_w, ada_b, norm_mix_g, norm_ffn_g, even_w_in, even_w_out, hgrn_lb_logits,
           hgrn_onorm_g, diff_lambda, diff_subln_g, odd_w_qkv, odd_w_out, swa_sink, router_w, router_b,
           moe_w_gate, moe_w_up, moe_w_down, final_norm_g):
    batch, t_len, d = x.shape
    c_len = ctx.shape[1]
    n_lat = batch * t_len
    assert t_len % ROW_TILE == 0 and (batch * c_len) % ROW_TILE == 0 and c_len == HGRN_TILE
    assert batch < 8 and t_len % DIFF_TK == 0 and t_len % GRID_W == 0

    x_all = jnp.concatenate([x.reshape(n_lat, d), ctx.reshape(batch * c_len, d)], axis=0)
    rope = _rope_tables(batch, t_len, c_len)

    def mod_idx_for(tile):
        per_batch = t_len // tile
        return lambda i: jnp.minimum(i // per_batch, batch)

    mod_idx = mod_idx_for(ROW_TILE)
    mod_idx_tok = mod_idx_for(MOE_TOK)

    cvec = jnp.zeros((8, d), F32).at[:batch].set(c).at[batch].set(c_ctx)
    mods_all = ada_modulation(cvec, ada_w, ada_b)
    lower_bounds = jnp.cumsum(jax.nn.softmax(hgrn_lb_logits.astype(F32), axis=0), axis=0)
    router_wt = router_w.astype(F32).T

    mods = mods_all[0].reshape(8, 1, 6 * d)
    g_mix = norm_mix_g[0].reshape(1, d)
    w_in = even_w_in[0].astype(BF16)
    n_a = 5 * A_WIDTH
    p_a = norm_mod_matmul(x_all, g_mix, mods, 1, 0, w_in[:, :n_a], rope, (0, 0), F32, mod_idx)
    n_qk = n_a + 2 * B_WIDTH
    p_qk, v_t = norm_mod_matmul(x_all, g_mix, mods, 1, 0, w_in[:, n_a:n_qk], rope, (0, 2 * B_WIDTH), BF16,
                                mod_idx, wt=w_in[:, n_qk:].T)
    o_f, o_b = hgrn_scan(p_a, lower_bounds[0], batch, t_len, c_len)
    ya = (o_f, o_b, p_a, hgrn_onorm_g[0].reshape(1, A_DK))
    lambda_init = 0.8 - 0.6 * math.exp(-0.3 * 0)
    yb, yb_ctx = diff_attention(p_qk, v_t, diff_lambda[0], diff_subln_g[0], lambda_init, batch, t_len, c_len)
    na = x_all.shape[0]
    x_new, h2, logits = out_proj(ya, yb, 0, even_w_out[0].astype(BF16), x_all, mods,
                                 norm_ffn_g[0].reshape(1, d), router_wt, na, mod_idx, yb_tail=yb_ctx)
    x_all = moe_layer(h2, logits, router_b, 0, moe_w_gate, moe_w_up, moe_w_down, x_new, mods,
                      final_norm_g.reshape(1, d), False, mod_idx_tok)

    mods = mods_all[1].reshape(8, 1, 6 * d)
    w_qkv = odd_w_qkv[0]
    q_cols = C_HEADS * C_HEAD_DIM
    kv_cols = C_KV_HEADS * C_HEAD_DIM

    n_qk1 = q_cols + kv_cols
    w_qkv = w_qkv.astype(BF16)
    p_qk1, v_t1 = norm_mod_matmul(x_all, norm_mix_g[1].reshape(1, d), mods, 1, 0, w_qkv[:, :n_qk1], rope,
                                  (0, n_qk1), BF16, mod_idx, tn=kv_cols, wt=w_qkv[:, n_qk1:].T)
    o1 = window_attention(p_qk1, v_t1, swa_sink[0], batch, t_len, c_len)
    x_new, h2, logits = out_proj(o1, o1, 1, odd_w_out[0].astype(BF16), x_all, mods,
                                 norm_ffn_g[1].reshape(1, d), router_wt, n_lat, mod_idx)
    out = moe_layer(h2, logits, router_b, 1, moe_w_gate, moe_w_up, moe_w_down, x_new, mods,
                    final_norm_g.reshape(1, d), True, mod_idx_tok)
    return out.reshape(batch, t_len, d)
```
